```python
import jax
import jax.numpy as jnp
from jax import lax
import numpy as np


D_MODEL = 2048
BATCH = 8
SEQ = 2048
DEPTH = 2

NSA_HEADS = 16
NSA_KV_GROUPS = 4
HEADS_PER_GROUP = NSA_HEADS // NSA_KV_GROUPS
HEAD_DIM = 128
NSA_WIDTH = NSA_HEADS * HEAD_DIM
KV_WIDTH = NSA_KV_GROUPS * HEAD_DIM
CMP_LEN = 32
CMP_STRIDE = 16
SLC_LEN = 64
SLC_TOP_N = 16
SLC_LOCAL = 2
FORCE_BONUS = 1e4
WINDOW = 512
WIN_Q_BLOCK = 128
SLC_Q_BLOCK = 32

LRU_WIDTH = 2048
LRU_BLOCKS = 16
LRU_BLOCK_DIM = LRU_WIDTH // LRU_BLOCKS
LRU_C = 8.0
LRU_CONV = 4

D_FF = 3 * D_MODEL
FFN_CONV = 3

IN_WIDTH = NSA_WIDTH + 6 * KV_WIDTH + 3 * NSA_HEADS + 2 * LRU_WIDTH + 2 * D_MODEL
NORM_EPS = 1e-6

kernel_name = "nsa_rglru_gated_hybrid_block"


def rmsnorm(x, g):
    x32 = x.astype(jnp.float32)
    inv = lax.rsqrt(jnp.mean(x32 * x32, axis=-1, keepdims=True) + NORM_EPS)
    return (x32 * inv).astype(x.dtype) * g


def causal_dwconv(x, w, b):
    k = w.shape[0]
    t = x.shape[1]
    xp = jnp.pad(x, ((0, 0), (k - 1, 0), (0, 0)))
    out = b
    for j in range(k):
        out = out + xp[:, j:j + t] * w[j]
    return out


def masked_softmax(s, mask):
    s = jnp.where(mask, s.astype(jnp.float32), -jnp.inf)
    m = jnp.max(s, axis=-1, keepdims=True)
    m = jnp.where(jnp.isfinite(m), m, 0.0)
    p = jnp.exp(s - m)
    return p / jnp.maximum(jnp.sum(p, axis=-1, keepdims=True), 1e-30)


def _split_points():
    sizes = (NSA_WIDTH,) + (KV_WIDTH,) * 6 + (3 * NSA_HEADS, LRU_WIDTH, LRU_WIDTH, D_MODEL, D_MODEL)
    return [int(v) for v in np.cumsum(sizes)[:-1]]


def compress_blocks(kv, pos, w1, w2):
    t = kv.shape[1]
    n_cmp = (t - CMP_LEN) // CMP_STRIDE + 1
    idx = jnp.arange(n_cmp)[:, None] * CMP_STRIDE + jnp.arange(CMP_LEN)[None, :]
    blocks = kv[:, idx] + pos[None, None, :, None, :]
    hid = jax.nn.gelu(jnp.einsum('bnlgd,lde->bnge', blocks, w1))
    return jnp.einsum('bnge,ef->bngf', hid, w2)


def selected_attention(q, k, v, sel):
    bn, t, g, hg, hd = q.shape
    n_slc = t // SLC_LEN
    n_top = sel.shape[-1]
    nc = t // SLC_Q_BLOCK
    kb = k.reshape(bn, n_slc, SLC_LEN, g, hd).transpose(0, 3, 1, 2, 4)
    vb = v.reshape(bn, n_slc, SLC_LEN, g, hd).transpose(0, 3, 1, 2, 4)
    b_ix = jnp.arange(bn)[:, None, None, None]
    g_ix = jnp.arange(g)[None, :, None, None]
    off = jnp.arange(SLC_LEN)
    m_tok = n_top * SLC_LEN

    def chunk(args):
        q_c, sel_c, start = args
        k_g = kb[b_ix, g_ix, sel_c].reshape(bn, g, SLC_Q_BLOCK, m_tok, hd)
        v_g = vb[b_ix, g_ix, sel_c].reshape(bn, g, SLC_Q_BLOCK, m_tok, hd)
        key_pos = (sel_c[..., None] * SLC_LEN + off).reshape(bn, g, SLC_Q_BLOCK, m_tok)
        q_pos = start + jnp.arange(SLC_Q_BLOCK)
        mask = key_pos <= q_pos[None, None, :, None]
        s = jnp.einsum('bqghd,bgqmd->bghqm', q_c, k_g)
        p = masked_softmax(s, mask[:, :, None])
        return jnp.einsum('bghqm,bgqmd->bqghd', p.astype(v_g.dtype), v_g)

    q_chunks = jnp.moveaxis(q.reshape(bn, nc, SLC_Q_BLOCK, g, hg, hd), 1, 0)
    sel_chunks = jnp.moveaxis(sel.reshape(bn, g, nc, SLC_Q_BLOCK, n_top), 2, 0)
    starts = jnp.arange(nc) * SLC_Q_BLOCK
    out = lax.map(chunk, (q_chunks, sel_chunks, starts))
    return jnp.moveaxis(out, 0, 1).reshape(bn, t, g, hg, hd)


def window_attention(q, k, v):
    bn, t, g, hg, hd = q.shape
    nb = t // WIN_Q_BLOCK
    n_prev = WINDOW // WIN_Q_BLOCK

    def band(z):
        zp = jnp.pad(z, ((0, 0), (WINDOW, 0), (0, 0), (0, 0)))
        zb = zp.reshape(bn, nb + n_prev, WIN_Q_BLOCK, g, hd)
        return jnp.concatenate([zb[:, j:j + nb] for j in range(n_prev + 1)], axis=2)

    kb, vb = band(k), band(v)
    qb = q.reshape(bn, nb, WIN_Q_BLOCK, g, hg, hd)
    q_pos = jnp.arange(t).reshape(nb, WIN_Q_BLOCK)
    k_pos = q_pos[:, :1] - WINDOW + jnp.arange((n_prev + 1) * WIN_Q_BLOCK)[None, :]
    diff = q_pos[:, :, None] - k_pos[:, None, :]
    mask = (diff >= 0) & (diff < WINDOW) & (k_pos[:, None, :] >= 0)
    s = jnp.einsum('bnqghd,bnmgd->bnghqm', qb, kb)
    p = masked_softmax(s, mask[None, :, None, None])
    o = jnp.einsum('bnghqm,bnmgd->bnqghd', p.astype(vb.dtype), vb)
    return o.reshape(bn, t, g, hg, hd)


def nsa_attention(q, kc, vc, ks, vs, kw, vw, gates, pos_k, w1_k, w2_k, pos_v, w1_v, w2_v):
    bn, t, _ = q.shape
    g, hg, hd = NSA_KV_GROUPS, HEADS_PER_GROUP, HEAD_DIM
    q = q.reshape(bn, t, g, hg, hd) * (hd ** -0.5)
    kc, vc = kc.reshape(bn, t, g, hd), vc.reshape(bn, t, g, hd)
    ks, vs = ks.reshape(bn, t, g, hd), vs.reshape(bn, t, g, hd)
    kw, vw = kw.reshape(bn, t, g, hd), vw.reshape(bn, t, g, hd)
    t_pos = jnp.arange(t)

    k_cmp = compress_blocks(kc, pos_k, w1_k, w2_k)
    v_cmp = compress_blocks(vc, pos_v, w1_v, w2_v)
    n_cmp = k_cmp.shape[1]
    cmp_start = jnp.arange(n_cmp) * CMP_STRIDE
    cmp_mask = (cmp_start + CMP_LEN - 1)[None, :] <= t_pos[:, None]
    s_cmp = jnp.einsum('btghd,bngd->bghtn', q, k_cmp)
    p_cmp = masked_softmax(s_cmp, cmp_mask)
    o_cmp = jnp.einsum('bghtn,bngd->btghd', p_cmp.astype(v_cmp.dtype), v_cmp)

    n_slc = t // SLC_LEN
    slc_start = jnp.arange(n_slc) * SLC_LEN
    overlap = ((cmp_start[:, None] < slc_start[None, :] + SLC_LEN)
               & (cmp_start[:, None] + CMP_LEN > slc_start[None, :])).astype(jnp.float32)
    imp = jnp.einsum('bghtn,nj->bgtj', p_cmp, overlap)
    blk = jnp.arange(n_slc)[None, :]
    cur = (t_pos // SLC_LEN)[:, None]
    valid = blk <= cur
    forced = (blk == 0) | (valid & (blk > cur - SLC_LOCAL))
    score = jnp.where(forced, imp + FORCE_BONUS, jnp.where(valid, imp, -1.0))
    n_top = min(SLC_TOP_N, n_slc)
    _, sel = lax.top_k(score, n_top)
    o_slc = selected_attention(q, ks, vs, sel)

    o_win = window_attention(q, kw, vw)

    gt = jax.nn.sigmoid(gates.reshape(bn, t, g, hg, 3))
    o = gt[..., 0:1] * o_cmp + gt[..., 1:2] * o_slc + gt[..., 2:3] * o_win
    return o.reshape(bn, t, NSA_WIDTH)


def rg_lru(x, w_a, b_a, w_i, b_i, lam):
    bn, t, w = x.shape
    xb = x.reshape(bn, t, LRU_BLOCKS, LRU_BLOCK_DIM)
    r = jax.nn.sigmoid(jnp.einsum('btnd,nde->btne', xb, w_a).reshape(bn, t, w) + b_a)
    i = jax.nn.sigmoid(jnp.einsum('btnd,nde->btne', xb, w_i).reshape(bn, t, w) + b_i)
    log_a = (LRU_C * r.astype(jnp.float32)) * jax.nn.log_sigmoid(lam.astype(jnp.float32))
    a = jnp.exp(log_a)
    b = jnp.sqrt(-jnp.expm1(2.0 * log_a)) * (i * x).astype(jnp.float32)

    def combine(left, right):
        a1, b1 = left
        a2, b2 = right
        return a1 * a2, a2 * b1 + b2

    _, h = lax.associative_scan(combine, (a, b), axis=1)
    return h.astype(x.dtype)


def token_mixer(h, w_in, pos_k, w1_k, w2_k, pos_v, w1_v, w2_v, conv_w, conv_b,
                w_a, b_a, w_i, b_i, lam, proj_a, proj_b, w_out):
    z = h @ w_in
    q, kc, vc, ks, vs, kw, vw, g_nsa, lx, ly, ga, gb = jnp.split(z, _split_points(), axis=-1)
    o_a = nsa_attention(q, kc, vc, ks, vs, kw, vw, g_nsa, pos_k, w1_k, w2_k, pos_v, w1_v, w2_v)
    u = causal_dwconv(lx, conv_w, conv_b)
    o_b = jax.nn.gelu(ly) * rg_lru(u, w_a, b_a, w_i, b_i, lam)
    merged = jax.nn.sigmoid(ga) * (o_a @ proj_a) + jax.nn.sigmoid(gb) * (o_b @ proj_b)
    return merged @ w_out


def conv_glu_ffn(h, w_up, conv_w, conv_b, w_down):
    gate, val = jnp.split(h @ w_up, 2, axis=-1)
    gate = causal_dwconv(gate, conv_w, conv_b)
    return (jax.nn.gelu(gate) * val) @ w_down


def setup_inputs(seed: int = 0) -> dict:
    key = jax.random.key(seed)
    ks = iter(jax.random.split(key, 40))

    def nrm(shape, fan_in, scale=1.0):
        return jax.random.normal(next(ks), shape, jnp.float32) * (scale * fan_in ** -0.5)

    def small(shape, s=0.02):
        return jax.random.normal(next(ks), shape, jnp.float32) * s

    x = jax.random.normal(next(ks), (BATCH, SEQ, D_MODEL), jnp.float32)
    c = jax.random.normal(next(ks), (BATCH, D_MODEL), jnp.float32)
    ada_w = nrm((DEPTH, D_MODEL, 6 * D_MODEL), D_MODEL, 0.5)
    ada_b = small((DEPTH, 6 * D_MODEL))
    norm1_g = 1.0 + small((DEPTH, D_MODEL))
    w_in = nrm((DEPTH, D_MODEL, IN_WIDTH), D_MODEL)
    cmp_pos_k = small((DEPTH, CMP_LEN, HEAD_DIM), 0.1)
    cmp_w1_k = nrm((DEPTH, CMP_LEN, HEAD_DIM, HEAD_DIM), CMP_LEN * HEAD_DIM)
    cmp_w2_k = nrm((DEPTH, HEAD_DIM, HEAD_DIM), HEAD_DIM)
    cmp_pos_v = small((DEPTH, CMP_LEN, HEAD_DIM), 0.1)
    cmp_w1_v = nrm((DEPTH, CMP_LEN, HEAD_DIM, HEAD_DIM), CMP_LEN * HEAD_DIM)
    cmp_w2_v = nrm((DEPTH, HEAD_DIM, HEAD_DIM), HEAD_DIM)
    lru_conv_w = nrm((DEPTH, LRU_CONV, LRU_WIDTH), LRU_CONV)
    lru_conv_b = small((DEPTH, LRU_WIDTH))
    lru_wa = nrm((DEPTH, LRU_BLOCKS, LRU_BLOCK_DIM, LRU_BLOCK_DIM), LRU_BLOCK_DIM)
    lru_ba = small((DEPTH, LRU_WIDTH))
    lru_wi = nrm((DEPTH, LRU_BLOCKS, LRU_BLOCK_DIM, LRU_BLOCK_DIM), LRU_BLOCK_DIM)
    lru_bi = small((DEPTH, LRU_WIDTH))
    u = jax.random.uniform(next(ks), (DEPTH, LRU_WIDTH), jnp.float32)
    rad = jnp.sqrt(u * (0.999 ** 2 - 0.9 ** 2) + 0.9 ** 2)
    lru_lambda = jnp.log(rad) - jnp.log1p(-rad)
    proj_a = nrm((DEPTH, NSA_WIDTH, D_MODEL), NSA_WIDTH)
    proj_b = nrm((DEPTH, LRU_WIDTH, D_MODEL), LRU_WIDTH)
    w_out = nrm((DEPTH, D_MODEL, D_MODEL), D_MODEL)
    norm2_g = 1.0 + small((DEPTH, D_MODEL))
    ffn_up = nrm((DEPTH, D_MODEL, 2 * D_FF), D_MODEL)
    ffn_conv_w = nrm((DEPTH, FFN_CONV, D_FF), FFN_CONV)
    ffn_conv_b = small((DEPTH, D_FF))
    ffn_down = nrm((DEPTH, D_FF, D_MODEL), D_FF)
    final_g = 1.0 + small((D_MODEL,))
    return {"x": x, "c": c, "ada_w": ada_w, "ada_b": ada_b, "norm1_g": norm1_g, "w_in": w_in,
            "cmp_pos_k": cmp_pos_k, "cmp_w1_k": cmp_w1_k, "cmp_w2_k": cmp_w2_k,
            "cmp_pos_v": cmp_pos_v, "cmp_w1_v": cmp_w1_v, "cmp_w2_v": cmp_w2_v,
            "lru_conv_w": lru_conv_w, "lru_conv_b": lru_conv_b, "lru_wa": lru_wa, "lru_ba": lru_ba,
            "lru_wi": lru_wi, "lru_bi": lru_bi, "lru_lambda": lru_lambda,
            "proj_a": proj_a, "proj_b": proj_b, "w_out": w_out, "norm2_g": norm2_g,
            "ffn_up": ffn_up, "ffn_conv_w": ffn_conv_w, "ffn_conv_b": ffn_conv_b, "ffn_down": ffn_down,
            "final_g": final_g}


def reference(x, c, ada_w, ada_b, norm1_g, w_in, cmp_pos_k, cmp_w1_k, cmp_w2_k, cmp_pos_v, cmp_w1_v,
              cmp_w2_v, lru_conv_w, lru_conv_b, lru_wa, lru_ba, lru_wi, lru_bi, lru_lambda, proj_a, proj_b,
              w_out, norm2_g, ffn_up, ffn_conv_w, ffn_conv_b, ffn_down, final_g):
    c_act = jax.nn.silu(c)
    for l in range(DEPTH):
        mod = (c_act @ ada_w[l] + ada_b[l])[:, None, :]
        sh1, sc1, g1, sh2, sc2, g2 = jnp.split(mod, 6, axis=-1)
        h = rmsnorm(x, norm1_g[l]) * (1.0 + sc1) + sh1
        y = token_mixer(h, w_in[l], cmp_pos_k[l], cmp_w1_k[l], cmp_w2_k[l], cmp_pos_v[l], cmp_w1_v[l],
                        cmp_w2_v[l], lru_conv_w[l], lru_conv_b[l], lru_wa[l], lru_ba[l], lru_wi[l],
                        lru_bi[l], lru_lambda[l], proj_a[l], proj_b[l], w_out[l])
        x = x + g1 * y
        h = rmsnorm(x, norm2_g[l]) * (1.0 + sc2) + sh2
        x = x + g2 * conv_glu_ffn(h, ffn_up[l], ffn_conv_w[l], ffn_conv_b[l], ffn_down[l])
    return rmsnorm(x, final_g)
```

```python
import functools

import jax
import jax.numpy as jnp
from jax import lax
from jax.experimental import pallas as pl
from jax.experimental.pallas import tpu as pltpu

F32 = jnp.float32
BF16 = jnp.bfloat16

NSA_HEADS = 16
NSA_KV_GROUPS = 4
HEADS_PER_GROUP = NSA_HEADS // NSA_KV_GROUPS
HEAD_DIM = 128
NSA_WIDTH = NSA_HEADS * HEAD_DIM
KV_WIDTH = NSA_KV_GROUPS * HEAD_DIM
CMP_LEN = 32
CMP_STRIDE = 16
SLC_LEN = 64
SLC_TOP_N = 16
SLC_LOCAL = 2
FORCE_BONUS = 1e4
WINDOW = 512
LRU_WIDTH = 2048
LRU_BLOCK_DIM = 128
LRU_C = 8.0
NORM_EPS = 1e-6
N_GATES = 3 * NSA_HEADS
GATES_PER_GROUP = 3 * HEADS_PER_GROUP

LANES = 128
SUBLANES = 8
BF16_ROWS = 16
VMEM_LIMIT_BYTES = 56 * 1024 * 1024

MASK_VALUE = -1e30

_NT = (((1,), (1,)), ((), ()))


def _params(*sem):
    return pltpu.CompilerParams(dimension_semantics=sem, vmem_limit_bytes=VMEM_LIMIT_BYTES)


def _tile(n, target, quantum=LANES):
    best = None
    t = quantum
    while t <= min(n, target):
        if n % t == 0:
            best = t
        t += quantum
    assert best is not None, (n, target, quantum)
    return best


def _dot(a, b):
    return jnp.dot(a, b, preferred_element_type=F32)


def _norm_mod(x, g, sc, sh):
    inv = lax.rsqrt(jnp.mean(x * x, axis=-1, keepdims=True) + NORM_EPS)
    return ((x * inv) * g) * (1.0 + sc) + sh


def _ada_kernel(c_ref, w_ref, b_ref, o_ref):
    c = c_ref[...]
    act = (c * jax.nn.sigmoid(c)).astype(BF16)
    o_ref[...] = _dot(act, w_ref[...].astype(BF16)) + b_ref[...]


def _ada(c, ada_w, ada_b):
    depth, d, n = ada_w.shape
    b = c.shape[0]
    tn = _tile(n, 1024)
    return pl.pallas_call(
        _ada_kernel,
        out_shape=jax.ShapeDtypeStruct((depth, b, n), F32),
        grid=(depth, n // tn),
        in_specs=[
            pl.BlockSpec((b, d), lambda l, j: (0, 0)),
            pl.BlockSpec((None, d, tn), lambda l, j: (l, 0, j)),
            pl.BlockSpec((None, 1, tn), lambda l, j: (l, 0, j)),
        ],
        out_specs=pl.BlockSpec((None, b, tn), lambda l, j: (l, 0, j)),
        compiler_params=_params("parallel", "parallel"),
        name="ada",
    )(c, ada_w, ada_b.reshape(depth, 1, n))


def _in_proj_kernel(x_ref, g_ref, sc_ref, sh_ref, w_ref, cs_ref, o_ref, h_ref):
    @pl.when(pl.program_id(1) == 0)
    def _():
        h_ref[...] = _norm_mod(x_ref[...], g_ref[...], sc_ref[...], sh_ref[...]).astype(BF16)

    o_ref[...] = (_dot(h_ref[...], w_ref[...]) * cs_ref[...]).astype(o_ref.dtype)


def _in_proj(x, g, sc, sh, w, colscale, out_dtype, seq, tm_target=512, tn_target=1024):
    m, d = x.shape
    n = w.shape[1]
    tm = _tile(seq, tm_target, BF16_ROWS)
    tn = _tile(n, tn_target)
    bidx = lambda i, j: ((i * tm) // seq, 0, 0)
    return pl.pallas_call(
        _in_proj_kernel,
        out_shape=jax.ShapeDtypeStruct((m, n), out_dtype),
        grid=(m // tm, n // tn),
        in_specs=[
            pl.BlockSpec((tm, d), lambda i, j: (i, 0)),
            pl.BlockSpec((1, d), lambda i, j: (0, 0)),
            pl.BlockSpec((None, 1, d), bidx),
            pl.BlockSpec((None, 1, d), bidx),
            pl.BlockSpec((d, tn), lambda i, j: (0, j)),
            pl.BlockSpec((1, tn), lambda i, j: (0, j)),
        ],
        out_specs=pl.BlockSpec((tm, tn), lambda i, j: (i, j)),
        scratch_shapes=[pltpu.VMEM((tm, d), BF16)],
        compiler_params=_params("parallel", "arbitrary"),
        name="in_proj",
    )(x, g, sc, sh, w, colscale)


def _compress_kernel(c_ref, pos_ref, w1_ref, w2_ref, o_ref):
    c = c_ref[...].astype(F32)
    lo = (c + pos_ref[0:1, :]).astype(BF16)
    hi = (c + pos_ref[1:2, :]).astype(BF16)
    first = _dot(lo, w1_ref[0])
    second = _dot(hi, w1_ref[1])
    n = first.shape[0]
    hid = first + pltpu.roll(second, n - 1, axis=0)
    o_ref[...] = _dot(jax.nn.gelu(hid).astype(BF16), w2_ref[...]).astype(o_ref.dtype)


def _compress(chunks, pos, w1, w2):
    b, _, g, n, cw = chunks.shape
    hd = w2.shape[-1]
    return pl.pallas_call(
        _compress_kernel,
        out_shape=jax.ShapeDtypeStruct((b, 2, g, n, hd), BF16),
        grid=(b, 2, g),
        in_specs=[
            pl.BlockSpec((None, None, None, n, cw), lambda bi, s, gi: (bi, s, gi, 0, 0)),
            pl.BlockSpec((None, 2, cw), lambda bi, s, gi: (s, 0, 0)),
            pl.BlockSpec((None, 2, cw, hd), lambda bi, s, gi: (s, 0, 0, 0)),
            pl.BlockSpec((None, hd, hd), lambda bi, s, gi: (s, 0, 0)),
        ],
        out_specs=pl.BlockSpec((None, None, None, n, hd), lambda bi, s, gi: (bi, s, gi, 0, 0)),
        compiler_params=_params("parallel", "parallel", "parallel"),
        name="compress",
    )(chunks, pos, w1, w2)


def _nsa_kernel(q_ref, ks_ref, vs_ref, kw_ref, vw_ref, kc_ref, vc_ref, zg_ref, o_ref,
                m_sc, l_sc, acc_sc, *, tq, tk):
    hd = HEAD_DIM
    hpg = HEADS_PER_GROUP
    rows = hpg * tq
    grp = pl.program_id(1)
    t0 = pl.program_id(2) * tq
    n_blk = LANES // 4

    q = q_ref[...]
    qs = jnp.concatenate([q[:, h * hd:(h + 1) * hd] for h in range(hpg)], axis=0)
    qpos1 = t0 + lax.broadcasted_iota(jnp.int32, (tq, 1), 0)
    qpos = jnp.concatenate([qpos1] * hpg, axis=0)

    n_cmp = kc_ref.shape[0]
    s = lax.dot_general(qs, kc_ref[...], _NT, preferred_element_type=F32)
    cmp_end = lax.broadcasted_iota(jnp.int32, (1, n_cmp), 1) * CMP_STRIDE + (CMP_LEN - 1)
    s = jnp.where(cmp_end <= qpos, s, -jnp.inf)
    mx = jnp.max(s, axis=-1, keepdims=True)
    mx = jnp.where(jnp.isfinite(mx), mx, 0.0)
    p = jnp.exp(s - mx)
    p = p / jnp.maximum(jnp.sum(p, axis=-1, keepdims=True), 1e-30)
    o_cmp = _dot(p.astype(BF16), vc_ref[...])

    p_heads = jnp.concatenate([p[h * tq:(h + 1) * tq] for h in range(hpg)], axis=1).astype(BF16)
    cstart = (lax.broadcasted_iota(jnp.int32, (hpg * n_cmp, LANES), 0) & (n_cmp - 1)) * CMP_STRIDE
    bstart = (lax.broadcasted_iota(jnp.int32, (hpg * n_cmp, LANES), 1) & (n_blk - 1)) * SLC_LEN
    overlap = jnp.where((cstart < bstart + SLC_LEN) & (cstart + CMP_LEN > bstart), 1.0, 0.0).astype(BF16)
    imp = _dot(p_heads, overlap)

    blk = lax.broadcasted_iota(jnp.int32, (tq, LANES), 1) & (n_blk - 1)
    cur = lax.shift_right_logical(qpos1, 6)
    valid = blk <= cur
    forced = (blk == 0) | (valid & (blk > cur - SLC_LOCAL))
    score = jnp.where(forced, imp + FORCE_BONUS, jnp.where(valid, imp, -1.0))
    rank = jnp.zeros((tq, LANES), jnp.int32)
    for r in range(1, n_blk):
        other = pltpu.roll(score, r, axis=1)
        before = (other > score) | ((other == score) & (blk >= r))
        rank = rank + jnp.where(before, 1, 0)
    sel = jnp.where((rank < SLC_TOP_N) & valid, 1.0, 0.0).astype(BF16)

    def attend(k_ref, v_ref, first_tile, last_tile, mask_fn):
        m_sc[...] = jnp.full(m_sc.shape, MASK_VALUE, F32)
        l_sc[...] = jnp.zeros(l_sc.shape, F32)
        acc_sc[...] = jnp.zeros(acc_sc.shape, F32)

        def body(kt, carry):
            k0 = pl.multiple_of(kt * tk, tk)
            k = k_ref[pl.ds(k0, tk), :]
            v = v_ref[pl.ds(k0, tk), :]
            sc = lax.dot_general(qs, k, _NT, preferred_element_type=F32)
            kpos = k0 + lax.broadcasted_iota(jnp.int32, (1, tk), 1)
            sc = jnp.where(mask_fn(kpos), sc, MASK_VALUE)
            m_prev = m_sc[...]
            m_new = jnp.maximum(m_prev, jnp.max(sc, axis=-1, keepdims=True))
            alpha = jnp.exp(m_prev - m_new)
            pe = jnp.exp(sc - m_new)
            l_sc[...] = alpha * l_sc[...] + jnp.sum(pe, axis=-1, keepdims=True)
            acc_sc[...] = alpha * acc_sc[...] + _dot(pe.astype(BF16), v)
            m_sc[...] = m_new
            return carry

        lax.fori_loop(first_tile, last_tile, body, 0)
        return acc_sc[...] / l_sc[...]

    last = (t0 + tq) // tk

    def slc_mask(kpos):
        expand = jnp.where(lax.shift_right_logical(kpos, 6) == lax.broadcasted_iota(jnp.int32, (LANES, tk), 0),
                           1.0, 0.0).astype(BF16)
        picked = _dot(sel, expand)
        picked = jnp.concatenate([picked] * hpg, axis=0)
        return (picked > 0.5) & (kpos <= qpos)

    o_slc = attend(ks_ref, vs_ref, 0, last, slc_mask)

    def win_mask(kpos):
        return (kpos <= qpos) & (qpos - kpos < WINDOW)

    o_win = attend(kw_ref, vw_ref, jnp.maximum(t0 - (WINDOW - 1), 0) // tk, last, win_mask)

    gt = jax.nn.sigmoid(pltpu.roll(zg_ref[...], (LANES - GATES_PER_GROUP * grp) & (LANES - 1), axis=1))
    outs = []
    for h in range(hpg):
        r0 = h * tq
        outs.append(gt[:, 3 * h:3 * h + 1] * o_cmp[r0:r0 + tq]
                    + gt[:, 3 * h + 1:3 * h + 2] * o_slc[r0:r0 + tq]
                    + gt[:, 3 * h + 2:3 * h + 3] * o_win[r0:r0 + tq])
    o_ref[...] = jnp.concatenate(outs, axis=1).astype(o_ref.dtype)


def _nsa(z_attn, kv_cmp, z_gates, batch, seq, tq=256, tk=256):
    m = z_attn.shape[0]
    g = NSA_KV_GROUPS
    hd = HEAD_DIM
    n_cmp = kv_cmp.shape[3]
    nq = seq // tq
    qcols = NSA_WIDTH // hd

    def kv_spec(slot):
        return pl.BlockSpec((seq, hd), lambda b, gi, i: (b, qcols + slot * g + gi))

    def cmp_spec(slot):
        return pl.BlockSpec((None, None, None, n_cmp, hd), lambda b, gi, i: (b, slot, gi, 0, 0))

    rows = HEADS_PER_GROUP * tq
    return pl.pallas_call(
        functools.partial(_nsa_kernel, tq=tq, tk=tk),
        out_shape=jax.ShapeDtypeStruct((m, NSA_WIDTH), BF16),
        grid=(batch, g, nq),
        in_specs=[
            pl.BlockSpec((tq, HEADS_PER_GROUP * hd), lambda b, gi, i: (b * nq + i, gi)),
            kv_spec(2), kv_spec(3), kv_spec(4), kv_spec(5),
            cmp_spec(0), cmp_spec(1),
            pl.BlockSpec((tq, LANES), lambda b, gi, i: (b * nq + i, 0)),
        ],
        out_specs=pl.BlockSpec((tq, HEADS_PER_GROUP * hd), lambda b, gi, i: (b * nq + i, gi)),
        scratch_shapes=[pltpu.VMEM((rows, 1), F32), pltpu.VMEM((rows, 1), F32), pltpu.VMEM((rows, hd), F32)],
        compiler_params=_params("parallel", "parallel", "arbitrary"),
        name="nsa",
    )(z_attn, z_attn, z_attn, z_attn, z_attn, kv_cmp, kv_cmp, z_gates)


def _lru_kernel(lx_ref, ly_ref, cw_ref, cb_ref, wa_ref, ba_ref, wi_ref, bi_ref, lam_ref, o_ref, a_sc, h_sc):
    seq, wt = lx_ref.shape
    x = lx_ref[...].astype(F32)
    row = lax.broadcasted_iota(jnp.int32, (seq, 1), 0)
    taps = cw_ref.shape[0]
    u = cb_ref[...]
    for j in range(taps):
        back = taps - 1 - j
        xs = x if back == 0 else jnp.where(row >= back, pltpu.roll(x, back, axis=0), 0.0)
        u = u + xs * cw_ref[j:j + 1, :]
    ub = u.astype(BF16)
    r = jax.nn.sigmoid(_dot(ub, wa_ref[0]) + ba_ref[...])
    gate_i = jax.nn.sigmoid(_dot(ub, wi_ref[0]) + bi_ref[...])
    log_a = (LRU_C * r) * jax.nn.log_sigmoid(lam_ref[...])
    a = jnp.exp(log_a)
    a_sc[...] = a
    h_sc[...] = jnp.sqrt(1.0 - a * a) * (gate_i * u)

    sub = lax.broadcasted_iota(jnp.int32, (SUBLANES, wt), 0)

    def chunk(k, h):
        r0 = pl.multiple_of(k * SUBLANES, SUBLANES)
        a = a_sc[pl.ds(r0, SUBLANES), :]
        b = h_sc[pl.ds(r0, SUBLANES), :]
        for s in (1, 2, 4):
            a_prev = jnp.where(sub >= s, pltpu.roll(a, s, axis=0), 1.0)
            b_prev = jnp.where(sub >= s, pltpu.roll(b, s, axis=0), 0.0)
            b = a * b_prev + b
            a = a * a_prev
        hs = a * h + b
        h_sc[pl.ds(r0, SUBLANES), :] = hs
        return hs[SUBLANES - 1:SUBLANES, :]

    lax.fori_loop(0, seq // SUBLANES, chunk, jnp.zeros((1, wt), F32), unroll=4)
    o_ref[...] = (jax.nn.gelu(ly_ref[...].astype(F32)) * h_sc[...]).astype(o_ref.dtype)


def _lru(z_rest, conv_w, conv_b, wa, ba, wi, bi, lam, batch, seq):
    m = z_rest.shape[0]
    wt = LRU_BLOCK_DIM
    nw = LRU_WIDTH // wt
    taps = conv_w.shape[0]
    vec = pl.BlockSpec((1, wt), lambda b, j: (0, j))
    blk = pl.BlockSpec((1, wt, wt), lambda b, j: (j, 0, 0))
    return pl.pallas_call(
        _lru_kernel,
        out_shape=jax.ShapeDtypeStruct((m, LRU_WIDTH), BF16),
        grid=(batch, nw),
        in_specs=[
            pl.BlockSpec((seq, wt), lambda b, j: (b, j)),
            pl.BlockSpec((seq, wt), lambda b, j: (b, nw + j)),
            pl.BlockSpec((taps, wt), lambda b, j: (0, j)),
            vec, blk, vec, blk, vec, vec,
        ],
        out_specs=pl.BlockSpec((seq, wt), lambda b, j: (b, j)),
        scratch_shapes=[pltpu.VMEM((seq, wt), F32), pltpu.VMEM((seq, wt), F32)],
        compiler_params=_params("parallel", "parallel"),
        name="lru",
    )(z_rest, z_rest, conv_w, conv_b, wa, ba, wi, bi, lam)


def _merge_kernel(oa_ref, ob_ref, pa_ref, pb_ref, ga_ref, gb_ref, o_ref):
    a = _dot(oa_ref[...], pa_ref[...])
    b = _dot(ob_ref[...], pb_ref[...])
    merged = jax.nn.sigmoid(ga_ref[...].astype(F32)) * a + jax.nn.sigmoid(gb_ref[...].astype(F32)) * b
    o_ref[...] = merged.astype(o_ref.dtype)


def _merge(o_a, o_b, proj_a, proj_b, z_rest, d_model, tm_target=512, tn_target=1024):
    m, ka = o_a.shape
    kb = o_b.shape[1]
    tm = _tile(m, tm_target, BF16_ROWS)
    tn = _tile(d_model, tn_target)
    ga0 = 2 * LRU_WIDTH // tn
    gb0 = (2 * LRU_WIDTH + d_model) // tn
    assert (2 * LRU_WIDTH) % tn == 0 and d_model % tn == 0
    return pl.pallas_call(
        _merge_kernel,
        out_shape=jax.ShapeDtypeStruct((m, d_model), BF16),
        grid=(m // tm, d_model // tn),
        in_specs=[
            pl.BlockSpec((tm, ka), lambda i, j: (i, 0)),
            pl.BlockSpec((tm, kb), lambda i, j: (i, 0)),
            pl.BlockSpec((ka, tn), lambda i, j: (0, j)),
            pl.BlockSpec((kb, tn), lambda i, j: (0, j)),
            pl.BlockSpec((tm, tn), lambda i, j: (i, ga0 + j)),
            pl.BlockSpec((tm, tn), lambda i, j: (i, gb0 + j)),
        ],
        out_specs=pl.BlockSpec((tm, tn), lambda i, j: (i, j)),
        compiler_params=_params("parallel", "parallel"),
        name="merge",
    )(o_a, o_b, proj_a, proj_b, z_rest, z_rest)


def _matmul_residual_kernel(a_ref, w_ref, x_ref, g_ref, o_ref):
    o_ref[...] = x_ref[...] + g_ref[...] * _dot(a_ref[...], w_ref[...])


def _matmul_residual(a, w, x, gate, seq, name, tm_target=512, tn_target=512):
    m, k = a.shape
    n = w.shape[1]
    tm = _tile(seq, tm_target, BF16_ROWS)
    tn = _tile(n, tn_target)
    return pl.pallas_call(
        _matmul_residual_kernel,
        out_shape=jax.ShapeDtypeStruct((m, n), F32),
        grid=(m // tm, n // tn),
        in_specs=[
            pl.BlockSpec((tm, k), lambda i, j: (i, 0)),
            pl.BlockSpec((k, tn), lambda i, j: (0, j)),
            pl.BlockSpec((tm, tn), lambda i, j: (i, j)),
            pl.BlockSpec((None, 1, tn), lambda i, j: ((i * tm) // seq, 0, j)),
        ],
        out_specs=pl.BlockSpec((tm, tn), lambda i, j: (i, j)),
        compiler_params=_params("parallel", "parallel"),
        name=name,
    )(a, w, x, gate)


def _ffn_up_kernel(x_ref, xh_ref, g_ref, sc_ref, sh_ref, wg_ref, wv_ref, cw_ref, cb_ref, o_ref, h_ref, *, seq):
    halo = BF16_ROWS
    tm = o_ref.shape[0]

    @pl.when(pl.program_id(1) == 0)
    def _():
        h_ref[0:halo, :] = _norm_mod(xh_ref[...], g_ref[...], sc_ref[...], sh_ref[...]).astype(BF16)
        h_ref[halo:, :] = _norm_mod(x_ref[...], g_ref[...], sc_ref[...], sh_ref[...]).astype(BF16)

    h = h_ref[...]
    gate = _dot(h, wg_ref[...])
    val = _dot(h[halo:], wv_ref[...])
    seq_start = (pl.program_id(0) * tm) % seq == 0
    row = lax.broadcasted_iota(jnp.int32, (tm, 1), 0)
    taps = cw_ref.shape[0]
    conv = cb_ref[...]
    for j in range(taps):
        back = taps - 1 - j
        gj = gate[halo - back:halo - back + tm]
        if back:
            gj = jnp.where(row >= jnp.where(seq_start, back, 0), gj, 0.0)
        conv = conv + gj * cw_ref[j:j + 1, :]
    o_ref[...] = (jax.nn.gelu(conv) * val).astype(o_ref.dtype)


def _ffn_up(x, g, sc, sh, w_up, conv_w, conv_b, seq, tm_target=512, tn_target=512):
    m, d = x.shape
    d_ff = w_up.shape[1] // 2
    tm = _tile(seq, tm_target, BF16_ROWS)
    tn = _tile(d_ff, tn_target)
    nj = d_ff // tn
    halo = BF16_ROWS
    taps = conv_w.shape[0]
    bidx = lambda i, j: ((i * tm) // seq, 0, 0)
    return pl.pallas_call(
        functools.partial(_ffn_up_kernel, seq=seq),
        out_shape=jax.ShapeDtypeStruct((m, d_ff), BF16),
        grid=(m // tm, nj),
        in_specs=[
            pl.BlockSpec((tm, d), lambda i, j: (i, 0)),
            pl.BlockSpec((halo, d), lambda i, j: (jnp.maximum(i * (tm // halo) - 1, 0), 0)),
            pl.BlockSpec((1, d), lambda i, j: (0, 0)),
            pl.BlockSpec((None, 1, d), bidx),
            pl.BlockSpec((None, 1, d), bidx),
            pl.BlockSpec((d, tn), lambda i, j: (0, j)),
            pl.BlockSpec((d, tn), lambda i, j: (0, nj + j)),
            pl.BlockSpec((taps, tn), lambda i, j: (0, j)),
            pl.BlockSpec((1, tn), lambda i, j: (0, j)),
        ],
        out_specs=pl.BlockSpec((tm, tn), lambda i, j: (i, j)),
        scratch_shapes=[pltpu.VMEM((halo + tm, d), BF16)],
        compiler_params=_params("parallel", "arbitrary"),
        name="ffn_up",
    )(x, x, g, sc, sh, w_up, w_up, conv_w, conv_b)


def _final_norm_kernel(x_ref, g_ref, o_ref):
    x = x_ref[...]
    inv = lax.rsqrt(jnp.mean(x * x, axis=-1, keepdims=True) + NORM_EPS)
    o_ref[...] = (x * inv) * g_ref[...]


def _final_norm(x, g, tm_target=512):
    m, d = x.shape
    tm = _tile(m, tm_target, SUBLANES)
    return pl.pallas_call(
        _final_norm_kernel,
        out_shape=jax.ShapeDtypeStruct((m, d), F32),
        grid=(m // tm,),
        in_specs=[pl.BlockSpec((tm, d), lambda i: (i, 0)), pl.BlockSpec((1, d), lambda i: (0, 0))],
        out_specs=pl.BlockSpec((tm, d), lambda i: (i, 0)),
        compiler_params=_params("parallel"),
        name="final_norm",
    )(x, g)


def _compress_weights(pos, w1, w2):
    half = CMP_LEN // 2
    return (pos.reshape(2, half * HEAD_DIM),
            w1.reshape(2, half * HEAD_DIM, HEAD_DIM).astype(BF16),
            w2.astype(BF16))


def kernel(x, c, ada_w, ada_b, norm1_g, w_in, cmp_pos_k, cmp_w1_k, cmp_w2_k, cmp_pos_v, cmp_w1_v, cmp_w2_v,
           lru_conv_w, lru_conv_b, lru_wa, lru_ba, lru_wi, lru_bi, lru_lambda, proj_a, proj_b, w_out, norm2_g,
           ffn_up, ffn_conv_w, ffn_conv_b, ffn_down, final_g):
    batch, seq, d = x.shape
    depth = ada_w.shape[0]
    m = batch * seq
    assert seq % SLC_LEN == 0 and seq // SLC_LEN == LANES // 4 and CMP_STRIDE * LANES == seq
    n_attn = NSA_WIDTH + 6 * KV_WIDTH
    n_gate0 = n_attn
    n_rest0 = n_attn + N_GATES

    mod = _ada(c, ada_w, ada_b)
    xf = x.reshape(m, d)
    q_scale = jnp.concatenate([jnp.full((1, NSA_WIDTH), HEAD_DIM ** -0.5, F32), jnp.ones((1, 6 * KV_WIDTH), F32)], axis=1)
    ones_gate = jnp.ones((1, LANES), F32)

    for l in range(depth):
        sh1, sc1, g1, sh2, sc2, g2 = [mod[l, :, k * d:(k + 1) * d].reshape(batch, 1, d) for k in range(6)]
        wl = w_in[l]
        w_attn = wl[:, :n_attn].astype(BF16)
        w_gate = jnp.pad(wl[:, n_gate0:n_rest0], ((0, 0), (0, LANES - N_GATES))).astype(BF16)
        w_rest = wl[:, n_rest0:].astype(BF16)
        n1g = norm1_g[l].reshape(1, d)

        z_attn = _in_proj(xf, n1g, sc1, sh1, w_attn, q_scale, BF16, seq)
        z_gates = _in_proj(xf, n1g, sc1, sh1, w_gate, ones_gate, F32, seq)
        z_rest = _in_proj(xf, n1g, sc1, sh1, w_rest, jnp.ones((1, w_rest.shape[1]), F32), F32, seq)

        n_chunk = seq // CMP_STRIDE
        kv_c = z_attn[:, NSA_WIDTH:NSA_WIDTH + 2 * KV_WIDTH]
        chunks = kv_c.reshape(batch, n_chunk, CMP_STRIDE, 2, NSA_KV_GROUPS, HEAD_DIM)
        chunks = chunks.transpose(0, 3, 4, 1, 2, 5).reshape(batch, 2, NSA_KV_GROUPS, n_chunk, CMP_STRIDE * HEAD_DIM)
        pk, w1k, w2k = _compress_weights(cmp_pos_k[l], cmp_w1_k[l], cmp_w2_k[l])
        pv, w1v, w2v = _compress_weights(cmp_pos_v[l], cmp_w1_v[l], cmp_w2_v[l])
        kv_cmp = _compress(chunks, jnp.stack([pk, pv]), jnp.stack([w1k, w1v]), jnp.stack([w2k, w2v]))

        o_a = _nsa(z_attn, kv_cmp, z_gates, batch, seq)
        o_b = _lru(z_rest, lru_conv_w[l], lru_conv_b[l].reshape(1, -1), lru_wa[l].astype(BF16),
                   lru_ba[l].reshape(1, -1), lru_wi[l].astype(BF16), lru_bi[l].reshape(1, -1),
                   lru_lambda[l].reshape(1, -1), batch, seq)
        merged = _merge(o_a, o_b, proj_a[l].astype(BF16), proj_b[l].astype(BF16), z_rest, d)
        xf = _matmul_residual(merged, w_out[l].astype(BF16), xf, g1, seq, "out_proj")

        act = _ffn_up(xf, norm2_g[l].reshape(1, d), sc2, sh2, ffn_up[l].astype(BF16), ffn_conv_w[l],
                      ffn_conv_b[l].reshape(1, -1), seq)
        xf = _matmul_residual(act, ffn_down[l].astype(BF16), xf, g2, seq, "ffn_down")

    return _final_norm(xf, final_g.reshape(1, d)).reshape(batch, seq, d)
```

```python
import functools

import jax
import jax.numpy as jnp
from jax import lax
from jax.experimental import pallas as pl
from jax.experimental.pallas import tpu as pltpu

F32 = jnp.float32
BF16 = jnp.bfloat16

NSA_HEADS = 16
NSA_KV_GROUPS = 4
HEADS_PER_GROUP = NSA_HEADS // NSA_KV_GROUPS
HEAD_DIM = 128
NSA_WIDTH = NSA_HEADS * HEAD_DIM
KV_WIDTH = NSA_KV_GROUPS * HEAD_DIM
CMP_LEN = 32
CMP_STRIDE = 16
SLC_LEN = 64
SLC_TOP_N = 16
SLC_LOCAL = 2
FORCE_BONUS = 1e4
WINDOW = 512
LRU_WIDTH = 2048
LRU_BLOCK_DIM = 128
LRU_C = 8.0
NORM_EPS = 1e-6
N_GATES = 3 * NSA_HEADS
GATES_PER_GROUP = 3 * HEADS_PER_GROUP

LANES = 128
SUBLANES = 8
BF16_ROWS = 16
VMEM_LIMIT_BYTES = 56 * 1024 * 1024

MASK_VALUE = -1e30

_NT = (((1,), (1,)), ((), ()))


def _params(*sem):
    return pltpu.CompilerParams(dimension_semantics=sem, vmem_limit_bytes=VMEM_LIMIT_BYTES)


def _tile(n, target, quantum=LANES):
    best = None
    t = quantum
    while t <= min(n, target):
        if n % t == 0:
            best = t
        t += quantum
    assert best is not None, (n, target, quantum)
    return best


def _dot(a, b):
    return jnp.dot(a, b, preferred_element_type=F32)


def _norm_mod(x, g, sc, sh):
    inv = lax.rsqrt(jnp.mean(x * x, axis=-1, keepdims=True) + NORM_EPS)
    return ((x * inv) * g) * (1.0 + sc) + sh


def _ada_kernel(c_ref, w_ref, b_ref, o_ref):
    c = c_ref[...]
    act = (c * jax.nn.sigmoid(c)).astype(BF16)
    o_ref[...] = _dot(act, w_ref[...].astype(BF16)) + b_ref[...]


def _ada(c, ada_w, ada_b):
    depth, d, n = ada_w.shape
    b = c.shape[0]
    tn = _tile(n, 1024)
    return pl.pallas_call(
        _ada_kernel,
        out_shape=jax.ShapeDtypeStruct((depth, b, n), F32),
        grid=(depth, n // tn),
        in_specs=[
            pl.BlockSpec((b, d), lambda l, j: (0, 0)),
            pl.BlockSpec((None, d, tn), lambda l, j: (l, 0, j)),
            pl.BlockSpec((None, 1, tn), lambda l, j: (l, 0, j)),
        ],
        out_specs=pl.BlockSpec((None, b, tn), lambda l, j: (l, 0, j)),
        compiler_params=_params("parallel", "parallel"),
        name="ada",
    )(c, ada_w, ada_b.reshape(depth, 1, n))


def _in_proj_kernel(x_ref, g_ref, sc_ref, sh_ref, w_ref, cs_ref, o_ref, h_ref):
    @pl.when(pl.program_id(1) == 0)
    def _():
        h_ref[...] = _norm_mod(x_ref[...], g_ref[...], sc_ref[...], sh_ref[...]).astype(BF16)

    o_ref[...] = (_dot(h_ref[...], w_ref[...]) * cs_ref[...]).astype(o_ref.dtype)


def _in_proj(x, g, sc, sh, w, colscale, out_dtype, seq, tm_target=512, tn_target=1024):
    m, d = x.shape
    n = w.shape[1]
    tm = _tile(seq, tm_target, BF16_ROWS)
    tn = _tile(n, tn_target)
    bidx = lambda i, j: ((i * tm) // seq, 0, 0)
    return pl.pallas_call(
        _in_proj_kernel,
        out_shape=jax.ShapeDtypeStruct((m, n), out_dtype),
        grid=(m // tm, n // tn),
        in_specs=[
            pl.BlockSpec((tm, d), lambda i, j: (i, 0)),
            pl.BlockSpec((1, d), lambda i, j: (0, 0)),
            pl.BlockSpec((None, 1, d), bidx),
            pl.BlockSpec((None, 1, d), bidx),
            pl.BlockSpec((d, tn), lambda i, j: (0, j)),
            pl.BlockSpec((1, tn), lambda i, j: (0, j)),
        ],
        out_specs=pl.BlockSpec((tm, tn), lambda i, j: (i, j)),
        scratch_shapes=[pltpu.VMEM((tm, d), BF16)],
        compiler_params=_params("parallel", "arbitrary"),
        name="in_proj",
    )(x, g, sc, sh, w, colscale)


def _compress_kernel(c_ref, pos_ref, w1_ref, w2_ref, o_ref):
    c = c_ref[...].astype(F32)
    lo = (c + pos_ref[0:1, :]).astype(BF16)
    hi = (c + pos_ref[1:2, :]).astype(BF16)
    first = _dot(lo, w1_ref[0])
    second = _dot(hi, w1_ref[1])
    n = first.shape[0]
    hid = first + pltpu.roll(second, n - 1, axis=0)
    o_ref[...] = _dot(jax.nn.gelu(hid).astype(BF16), w2_ref[...]).astype(o_ref.dtype)


def _compress(chunks, pos, w1, w2):
    b, _, g, n, cw = chunks.shape
    hd = w2.shape[-1]
    return pl.pallas_call(
        _compress_kernel,
        out_shape=jax.ShapeDtypeStruct((b, 2, g, n, hd), BF16),
        grid=(b, 2, g),
        in_specs=[
            pl.BlockSpec((None, None, None, n, cw), lambda bi, s, gi: (bi, s, gi, 0, 0)),
            pl.BlockSpec((None, 2, cw), lambda bi, s, gi: (s, 0, 0)),
            pl.BlockSpec((None, 2, cw, hd), lambda bi, s, gi: (s, 0, 0, 0)),
            pl.BlockSpec((None, hd, hd), lambda bi, s, gi: (s, 0, 0)),
        ],
        out_specs=pl.BlockSpec((None, None, None, n, hd), lambda bi, s, gi: (bi, s, gi, 0, 0)),
        compiler_params=_params("parallel", "parallel", "parallel"),
        name="compress",
    )(chunks, pos, w1, w2)


def _nsa_kernel(q_ref, ks_ref, vs_ref, kw_ref, vw_ref, kc_ref, vc_ref, zg_ref, o_ref,
                kaug_sc, vsaug_sc, vwaug_sc, ov_sc, bias_sc, qa_sc, m_sc, acc_sc, out_sc, *, tq):
    hd = HEAD_DIM
    hpg = HEADS_PER_GROUP
    tk = tq
    n_blk = LANES // 4
    sub = tq // 4
    seq = ks_ref.shape[0]
    n_cmp = kc_ref.shape[0]
    grp = pl.program_id(1)
    qi = pl.program_id(2)
    t0 = qi * tq

    @pl.when(qi == 0)
    def _():
        kaug_sc[:, :hd] = ks_ref[...]
        key_blk = lax.shift_right_logical(lax.broadcasted_iota(jnp.int32, (seq, LANES), 0), 6)
        lane_blk = lax.broadcasted_iota(jnp.int32, (seq, LANES), 1) & (n_blk - 1)
        kaug_sc[:, hd:] = jnp.where(key_blk == lane_blk, 1.0, 0.0).astype(BF16)
        ones = jnp.ones((seq, LANES), BF16)
        vsaug_sc[:, :hd] = vs_ref[...]
        vsaug_sc[:, hd:] = ones
        vwaug_sc[:, :hd] = vw_ref[...]
        vwaug_sc[:, hd:] = ones
        orow = lax.broadcasted_iota(jnp.int32, (hpg * n_cmp, LANES), 0) & (n_cmp - 1)
        ocol = lax.broadcasted_iota(jnp.int32, (hpg * n_cmp, LANES), 1)
        cstart = orow * CMP_STRIDE
        bstart = (ocol & (n_blk - 1)) * SLC_LEN
        hit = (cstart < bstart + SLC_LEN) & (cstart + CMP_LEN > bstart)
        for c in range(4):
            ov_sc[c] = jnp.where(hit & (lax.shift_right_logical(ocol, 5) == c), 1.0, 0.0).astype(BF16)
        r = lax.broadcasted_iota(jnp.int32, (tq, tk), 0)
        cidx = lax.broadcasted_iota(jnp.int32, (tq, tk), 1)
        bias_sc[0] = jnp.where(cidx <= r, 0.0, MASK_VALUE)
        bias_sc[1] = jnp.where(cidx > r, 0.0, MASK_VALUE)

    q = q_ref[...]
    qs = jnp.concatenate([q[:, h * hd:(h + 1) * hd] for h in range(hpg)], axis=0)
    qpos = t0 + (lax.broadcasted_iota(jnp.int32, (hpg * tq, 1), 0) & (tq - 1))
    gt = jax.nn.sigmoid(pltpu.roll(zg_ref[...], (LANES - GATES_PER_GROUP * grp) & (LANES - 1), axis=1))

    s = lax.dot_general(qs, kc_ref[...], _NT, preferred_element_type=F32)
    cmp_end = lax.broadcasted_iota(jnp.int32, (1, n_cmp), 1) * CMP_STRIDE + (CMP_LEN - 1)
    s = jnp.where(cmp_end <= qpos, s, -jnp.inf)
    mx = jnp.max(s, axis=-1, keepdims=True)
    mx = jnp.where(jnp.isfinite(mx), mx, 0.0)
    p = jnp.exp(s - mx)
    p = p / jnp.maximum(jnp.sum(p, axis=-1, keepdims=True), 1e-30)
    o_cmp = _dot(p.astype(BF16), vc_ref[...])
    for h in range(hpg):
        out_sc[:, h * hd:(h + 1) * hd] = gt[:, 3 * h:3 * h + 1] * o_cmp[h * tq:(h + 1) * tq]

    p_heads = jnp.concatenate([p[h * tq:(h + 1) * tq] for h in range(hpg)], axis=1).astype(BF16)
    imp = _dot(p_heads[0:sub], ov_sc[0])
    for c in range(1, 4):
        imp = imp + _dot(p_heads[c * sub:(c + 1) * sub], ov_sc[c])
    lane = lax.broadcasted_iota(jnp.int32, (sub, LANES), 1)
    blk = lane & (n_blk - 1)
    seg = lax.shift_right_logical(lane, 5)
    cur = lax.shift_right_logical(t0, 6) + seg
    valid = blk <= cur
    forced = (blk == 0) | (valid & (blk > cur - SLC_LOCAL))
    score = jnp.where(forced, imp + FORCE_BONUS, jnp.where(valid, imp, -1.0))
    key = lax.bitcast_convert_type(score, jnp.int32)
    rank = jnp.zeros((sub, LANES), jnp.int32)
    for r in range(1, n_blk):
        lower = blk >= r
        other = jnp.where(lower, pltpu.roll(key, r, axis=1), pltpu.roll(key, LANES - n_blk + r, axis=1))
        rank = rank + jnp.where(other + jnp.where(lower, 1, 0) > key, 1, 0)
    sel_bias = jnp.where((rank < SLC_TOP_N) & valid, 0.0, MASK_VALUE)
    sel_rows = jnp.concatenate([jnp.where(seg == c, sel_bias, 0.0) for c in range(4)], axis=0).astype(BF16)
    for h in range(hpg):
        qa_sc[h] = jnp.concatenate([q[:, h * hd:(h + 1) * hd], sel_rows], axis=1)

    def reset():
        m_sc[...] = jnp.full(m_sc.shape, MASK_VALUE, F32)
        acc_sc[...] = jnp.zeros(acc_sc.shape, F32)

    def update(h, qh, k_t, v_t, bias):
        sc = lax.dot_general(qh, k_t, _NT, preferred_element_type=F32)
        if bias is not None:
            sc = sc + bias
        m_prev = m_sc[h]
        m_new = jnp.maximum(m_prev, jnp.max(sc, axis=-1, keepdims=True))
        alpha = jnp.exp(m_prev - m_new)
        pe = jnp.exp(sc - jnp.concatenate([m_new] * (tk // LANES), axis=1))
        acc_sc[h] = jnp.concatenate([alpha, alpha], axis=1) * acc_sc[h] + _dot(pe.astype(BF16), v_t)
        m_sc[h] = m_new

    def finish(h):
        acc = acc_sc[h]
        return acc[:, :hd] / acc[:, hd:]

    reset()

    def slc_tile(kt, carry):
        k0 = pl.multiple_of(kt * tk, tk)
        k_t = kaug_sc[pl.ds(k0, tk), :]
        v_t = vsaug_sc[pl.ds(k0, tk), :]
        for h in range(hpg):
            update(h, qa_sc[h], k_t, v_t, None)
        return carry

    lax.fori_loop(0, qi, slc_tile, 0)
    d0 = pl.multiple_of(t0, tk)
    k_t = kaug_sc[pl.ds(d0, tk), :]
    v_t = vsaug_sc[pl.ds(d0, tk), :]
    for h in range(hpg):
        update(h, qa_sc[h], k_t, v_t, bias_sc[0])
        out_sc[:, h * hd:(h + 1) * hd] += gt[:, 3 * h + 1:3 * h + 2] * finish(h)

    assert WINDOW == 2 * tk
    reset()

    def win_tile(kt, bias):
        k0 = pl.multiple_of(kt * tk, tk)
        k_t = kw_ref[pl.ds(k0, tk), :]
        v_t = vwaug_sc[pl.ds(k0, tk), :]
        for h in range(hpg):
            update(h, qa_sc[h][:, :hd], k_t, v_t, bias)

    @pl.when(qi >= 2)
    def _():
        win_tile(qi - 2, bias_sc[1])

    @pl.when(qi >= 1)
    def _():
        win_tile(qi - 1, None)

    win_tile(qi, bias_sc[0])
    for h in range(hpg):
        o_h = out_sc[:, h * hd:(h + 1) * hd] + gt[:, 3 * h + 2:3 * h + 3] * finish(h)
        o_ref[:, h * hd:(h + 1) * hd] = o_h.astype(o_ref.dtype)


def _nsa(z_attn, kv_cmp, z_gates, batch, seq, tq=256):
    m = z_attn.shape[0]
    g = NSA_KV_GROUPS
    hd = HEAD_DIM
    hpg = HEADS_PER_GROUP
    n_cmp = kv_cmp.shape[3]
    nq = seq // tq
    qcols = NSA_WIDTH // hd
    assert tq // 4 == SLC_LEN and n_cmp == LANES

    def kv_spec(slot):
        return pl.BlockSpec((seq, hd), lambda b, gi, i: (b, qcols + slot * g + gi))

    def cmp_spec(slot):
        return pl.BlockSpec((None, None, None, n_cmp, hd), lambda b, gi, i: (b, slot, gi, 0, 0))

    return pl.pallas_call(
        functools.partial(_nsa_kernel, tq=tq),
        out_shape=jax.ShapeDtypeStruct((m, NSA_WIDTH), BF16),
        grid=(batch, g, nq),
        in_specs=[
            pl.BlockSpec((tq, hpg * hd), lambda b, gi, i: (b * nq + i, gi)),
            kv_spec(2), kv_spec(3), kv_spec(4), kv_spec(5),
            cmp_spec(0), cmp_spec(1),
            pl.BlockSpec((tq, LANES), lambda b, gi, i: (b * nq + i, 0)),
        ],
        out_specs=pl.BlockSpec((tq, hpg * hd), lambda b, gi, i: (b * nq + i, gi)),
        scratch_shapes=[
            pltpu.VMEM((seq, 2 * hd), BF16),
            pltpu.VMEM((seq, 2 * hd), BF16),
            pltpu.VMEM((seq, 2 * hd), BF16),
            pltpu.VMEM((4, hpg * n_cmp, LANES), BF16),
            pltpu.VMEM((2, tq, tq), F32),
            pltpu.VMEM((hpg, tq, 2 * hd), BF16),
            pltpu.VMEM((hpg, tq, LANES), F32),
            pltpu.VMEM((hpg, tq, 2 * hd), F32),
            pltpu.VMEM((tq, hpg * hd), F32),
        ],
        compiler_params=_params("parallel", "parallel", "arbitrary"),
        name="nsa",
    )(z_attn, z_attn, z_attn, z_attn, z_attn, kv_cmp, kv_cmp, z_gates)


def _lru_kernel(lx_ref, ly_ref, cw_ref, cb_ref, wa_ref, ba_ref, wi_ref, bi_ref, lam_ref, o_ref, a_sc, h_sc):
    seq, wt = lx_ref.shape
    x = lx_ref[...].astype(F32)
    row = lax.broadcasted_iota(jnp.int32, (seq, 1), 0)
    taps = cw_ref.shape[0]
    u = cb_ref[...]
    for j in range(taps):
        back = taps - 1 - j
        xs = x if back == 0 else jnp.where(row >= back, pltpu.roll(x, back, axis=0), 0.0)
        u = u + xs * cw_ref[j:j + 1, :]
    ub = u.astype(BF16)
    r = jax.nn.sigmoid(_dot(ub, wa_ref[0]) + ba_ref[...])
    gate_i = jax.nn.sigmoid(_dot(ub, wi_ref[0]) + bi_ref[...])
    log_a = (LRU_C * r) * jax.nn.log_sigmoid(lam_ref[...])
    a = jnp.exp(log_a)
    a_sc[...] = a
    h_sc[...] = jnp.sqrt(1.0 - a * a) * (gate_i * u)

    sub = lax.broadcasted_iota(jnp.int32, (SUBLANES, wt), 0)

    def chunk(k, h):
        r0 = pl.multiple_of(k * SUBLANES, SUBLANES)
        a = a_sc[pl.ds(r0, SUBLANES), :]
        b = h_sc[pl.ds(r0, SUBLANES), :]
        for s in (1, 2, 4):
            a_prev = jnp.where(sub >= s, pltpu.roll(a, s, axis=0), 1.0)
            b_prev = jnp.where(sub >= s, pltpu.roll(b, s, axis=0), 0.0)
            b = a * b_prev + b
            a = a * a_prev
        hs = a * h + b
        h_sc[pl.ds(r0, SUBLANES), :] = hs
        return hs[SUBLANES - 1:SUBLANES, :]

    lax.fori_loop(0, seq // SUBLANES, chunk, jnp.zeros((1, wt), F32), unroll=4)
    o_ref[...] = (jax.nn.gelu(ly_ref[...].astype(F32)) * h_sc[...]).astype(o_ref.dtype)


def _lru(z_rest, conv_w, conv_b, wa, ba, wi, bi, lam, batch, seq):
    m = z_rest.shape[0]
    wt = LRU_BLOCK_DIM
    nw = LRU_WIDTH // wt
    taps = conv_w.shape[0]
    vec = pl.BlockSpec((1, wt), lambda b, j: (0, j))
    blk = pl.BlockSpec((1, wt, wt), lambda b, j: (j, 0, 0))
    return pl.pallas_call(
        _lru_kernel,
        out_shape=jax.ShapeDtypeStruct((m, LRU_WIDTH), BF16),
        grid=(batch, nw),
        in_specs=[
            pl.BlockSpec((seq, wt), lambda b, j: (b, j)),
            pl.BlockSpec((seq, wt), lambda b, j: (b, nw + j)),
            pl.BlockSpec((taps, wt), lambda b, j: (0, j)),
            vec, blk, vec, blk, vec, vec,
        ],
        out_specs=pl.BlockSpec((seq, wt), lambda b, j: (b, j)),
        scratch_shapes=[pltpu.VMEM((seq, wt), F32), pltpu.VMEM((seq, wt), F32)],
        compiler_params=_params("parallel", "parallel"),
        name="lru",
    )(z_rest, z_rest, conv_w, conv_b, wa, ba, wi, bi, lam)


def _merge_kernel(oa_ref, ob_ref, pa_ref, pb_ref, ga_ref, gb_ref, o_ref):
    a = _dot(oa_ref[...], pa_ref[...])
    b = _dot(ob_ref[...], pb_ref[...])
    merged = jax.nn.sigmoid(ga_ref[...].astype(F32)) * a + jax.nn.sigmoid(gb_ref[...].astype(F32)) * b
    o_ref[...] = merged.astype(o_ref.dtype)


def _merge(o_a, o_b, proj_a, proj_b, z_rest, d_model, tm_target=512, tn_target=1024):
    m, ka = o_a.shape
    kb = o_b.shape[1]
    tm = _tile(m, tm_target, BF16_ROWS)
    tn = _tile(d_model, tn_target)
    ga0 = 2 * LRU_WIDTH // tn
    gb0 = (2 * LRU_WIDTH + d_model) // tn
    assert (2 * LRU_WIDTH) % tn == 0 and d_model % tn == 0
    return pl.pallas_call(
        _merge_kernel,
        out_shape=jax.ShapeDtypeStruct((m, d_model), BF16),
        grid=(m // tm, d_model // tn),
        in_specs=[
            pl.BlockSpec((tm, ka), lambda i, j: (i, 0)),
            pl.BlockSpec((tm, kb), lambda i, j: (i, 0)),
            pl.BlockSpec((ka, tn), lambda i, j: (0, j)),
            pl.BlockSpec((kb, tn), lambda i, j: (0, j)),
            pl.BlockSpec((tm, tn), lambda i, j: (i, ga0 + j)),
            pl.BlockSpec((tm, tn), lambda i, j: (i, gb0 + j)),
        ],
        out_specs=pl.BlockSpec((tm, tn), lambda i, j: (i, j)),
        compiler_params=_params("parallel", "parallel"),
        name="merge",
    )(o_a, o_b, proj_a, proj_b, z_rest, z_rest)


def _matmul_residual_kernel(a_ref, w_ref, x_ref, g_ref, o_ref):
    o_ref[...] = x_ref[...] + g_ref[...] * _dot(a_ref[...], w_ref[...])


def _matmul_residual(a, w, x, gate, seq, name, tm_target=512, tn_target=512):
    m, k = a.shape
    n = w.shape[1]
    tm = _tile(seq, tm_target, BF16_ROWS)
    tn = _tile(n, tn_target)
    return pl.pallas_call(
        _matmul_residual_kernel,
        out_shape=jax.ShapeDtypeStruct((m, n), F32),
        grid=(m // tm, n // tn),
        in_specs=[
            pl.BlockSpec((tm, k), lambda i, j: (i, 0)),
            pl.BlockSpec((k, tn), lambda i, j: (0, j)),
            pl.BlockSpec((tm, tn), lambda i, j: (i, j)),
            pl.BlockSpec((None, 1, tn), lambda i, j: ((i * tm) // seq, 0, j)),
        ],
        out_specs=pl.BlockSpec((tm, tn), lambda i, j: (i, j)),
        compiler_params=_params("parallel", "parallel"),
        name=name,
    )(a, w, x, gate)


def _ffn_up_kernel(x_ref, xh_ref, g_ref, sc_ref, sh_ref, wg_ref, wv_ref, cw_ref, cb_ref, o_ref, h_ref, *, seq):
    halo = BF16_ROWS
    tm = o_ref.shape[0]

    @pl.when(pl.program_id(1) == 0)
    def _():
        h_ref[0:halo, :] = _norm_mod(xh_ref[...], g_ref[...], sc_ref[...], sh_ref[...]).astype(BF16)
        h_ref[halo:, :] = _norm_mod(x_ref[...], g_ref[...], sc_ref[...], sh_ref[...]).astype(BF16)

    h = h_ref[...]
    gate = _dot(h, wg_ref[...])
    val = _dot(h[halo:], wv_ref[...])
    seq_start = (pl.program_id(0) * tm) % seq == 0
    row = lax.broadcasted_iota(jnp.int32, (tm, 1), 0)
    taps = cw_ref.shape[0]
    conv = cb_ref[...]
    for j in range(taps):
        back = taps - 1 - j
        gj = gate[halo - back:halo - back + tm]
        if back:
            gj = jnp.where(row >= jnp.where(seq_start, back, 0), gj, 0.0)
        conv = conv + gj * cw_ref[j:j + 1, :]
    o_ref[...] = (jax.nn.gelu(conv) * val).astype(o_ref.dtype)


def _ffn_up(x, g, sc, sh, w_up, conv_w, conv_b, seq, tm_target=512, tn_target=512):
    m, d = x.shape
    d_ff = w_up.shape[1] // 2
    tm = _tile(seq, tm_target, BF16_ROWS)
    tn = _tile(d_ff, tn_target)
    nj = d_ff // tn
    halo = BF16_ROWS
    taps = conv_w.shape[0]
    bidx = lambda i, j: ((i * tm) // seq, 0, 0)
    return pl.pallas_call(
        functools.partial(_ffn_up_kernel, seq=seq),
        out_shape=jax.ShapeDtypeStruct((m, d_ff), BF16),
        grid=(m // tm, nj),
        in_specs=[
            pl.BlockSpec((tm, d), lambda i, j: (i, 0)),
            pl.BlockSpec((halo, d), lambda i, j: (jnp.maximum(i * (tm // halo) - 1, 0), 0)),
            pl.BlockSpec((1, d), lambda i, j: (0, 0)),
            pl.BlockSpec((None, 1, d), bidx),
            pl.BlockSpec((None, 1, d), bidx),
            pl.BlockSpec((d, tn), lambda i, j: (0, j)),
            pl.BlockSpec((d, tn), lambda i, j: (0, nj + j)),
            pl.BlockSpec((taps, tn), lambda i, j: (0, j)),
            pl.BlockSpec((1, tn), lambda i, j: (0, j)),
        ],
        out_specs=pl.BlockSpec((tm, tn), lambda i, j: (i, j)),
        scratch_shapes=[pltpu.VMEM((halo + tm, d), BF16)],
        compiler_params=_params("parallel", "arbitrary"),
        name="ffn_up",
    )(x, x, g, sc, sh, w_up, w_up, conv_w, conv_b)


def _final_norm_kernel(x_ref, g_ref, o_ref):
    x = x_ref[...]
    inv = lax.rsqrt(jnp.mean(x * x, axis=-1, keepdims=True) + NORM_EPS)
    o_ref[...] = (x * inv) * g_ref[...]


def _final_norm(x, g, tm_target=512):
    m, d = x.shape
    tm = _tile(m, tm_target, SUBLANES)
    return pl.pallas_call(
        _final_norm_kernel,
        out_shape=jax.ShapeDtypeStruct((m, d), F32),
        grid=(m // tm,),
        in_specs=[pl.BlockSpec((tm, d), lambda i: (i, 0)), pl.BlockSpec((1, d), lambda i: (0, 0))],
        out_specs=pl.BlockSpec((tm, d), lambda i: (i, 0)),
        compiler_params=_params("parallel"),
        name="final_norm",
    )(x, g)


def _compress_weights(pos, w1, w2):
    half = CMP_LEN // 2
    return (pos.reshape(2, half * HEAD_DIM),
            w1.reshape(2, half * HEAD_DIM, HEAD_DIM).astype(BF16),
            w2.astype(BF16))


def kernel(x, c, ada_w, ada_b, norm1_g, w_in, cmp_pos_k, cmp_w1_k, cmp_w2_k, cmp_pos_v, cmp_w1_v, cmp_w2_v,
           lru_conv_w, lru_conv_b, lru_wa, lru_ba, lru_wi, lru_bi, lru_lambda, proj_a, proj_b, w_out, norm2_g,
           ffn_up, ffn_conv_w, ffn_conv_b, ffn_down, final_g):
    batch, seq, d = x.shape
    depth = ada_w.shape[0]
    m = batch * seq
    assert seq % SLC_LEN == 0 and seq // SLC_LEN == LANES // 4 and CMP_STRIDE * LANES == seq
    n_attn = NSA_WIDTH + 6 * KV_WIDTH
    n_gate0 = n_attn
    n_rest0 = n_attn + N_GATES

    mod = _ada(c, ada_w, ada_b)
    xf = x.reshape(m, d)
    q_scale = jnp.concatenate([jnp.full((1, NSA_WIDTH), HEAD_DIM ** -0.5, F32), jnp.ones((1, 6 * KV_WIDTH), F32)], axis=1)
    ones_gate = jnp.ones((1, LANES), F32)

    for l in range(depth):
        sh1, sc1, g1, sh2, sc2, g2 = [mod[l, :, k * d:(k + 1) * d].reshape(batch, 1, d) for k in range(6)]
        wl = w_in[l]
        w_attn = wl[:, :n_attn].astype(BF16)
        w_gate = jnp.pad(wl[:, n_gate0:n_rest0], ((0, 0), (0, LANES - N_GATES))).astype(BF16)
        w_rest = wl[:, n_rest0:].astype(BF16)
        n1g = norm1_g[l].reshape(1, d)

        z_attn = _in_proj(xf, n1g, sc1, sh1, w_attn, q_scale, BF16, seq)
        z_gates = _in_proj(xf, n1g, sc1, sh1, w_gate, ones_gate, F32, seq)
        z_rest = _in_proj(xf, n1g, sc1, sh1, w_rest, jnp.ones((1, w_rest.shape[1]), F32), F32, seq)

        n_chunk = seq // CMP_STRIDE
        kv_c = z_attn[:, NSA_WIDTH:NSA_WIDTH + 2 * KV_WIDTH]
        chunks = kv_c.reshape(batch, n_chunk, CMP_STRIDE, 2, NSA_KV_GROUPS, HEAD_DIM)
        chunks = chunks.transpose(0, 3, 4, 1, 2, 5).reshape(batch, 2, NSA_KV_GROUPS, n_chunk, CMP_STRIDE * HEAD_DIM)
        pk, w1k, w2k = _compress_weights(cmp_pos_k[l], cmp_w1_k[l], cmp_w2_k[l])
        pv, w1v, w2v = _compress_weights(cmp_pos_v[l], cmp_w1_v[l], cmp_w2_v[l])
        kv_cmp = _compress(chunks, jnp.stack([pk, pv]), jnp.stack([w1k, w1v]), jnp.stack([w2k, w2v]))

        o_a = _nsa(z_attn, kv_cmp, z_gates, batch, seq)
        o_b = _lru(z_rest, lru_conv_w[l], lru_conv_b[l].reshape(1, -1), lru_wa[l].astype(BF16),
                   lru_ba[l].reshape(1, -1), lru_wi[l].astype(BF16), lru_bi[l].reshape(1, -1),
                   lru_lambda[l].reshape(1, -1), batch, seq)
        merged = _merge(o_a, o_b, proj_a[l].astype(BF16), proj_b[l].astype(BF16), z_rest, d)
        xf = _matmul_residual(merged, w_out[l].astype(BF16), xf, g1, seq, "out_proj")

        act = _ffn_up(xf, norm2_g[l].reshape(1, d), sc2, sh2, ffn_up[l].astype(BF16), ffn_conv_w[l],
                      ffn_conv_b[l].reshape(1, -1), seq)
        xf = _matmul_residual(act, ffn_down[l].astype(BF16), xf, g2, seq, "ffn_down")

    return _final_norm(xf, final_g.reshape(1, d)).reshape(batch, seq, d)
```

```python
import functools

import jax
import jax.numpy as jnp
from jax import lax
from jax.experimental import pallas as pl
from jax.experimental.pallas import tpu as pltpu

F32 = jnp.float32
BF16 = jnp.bfloat16

NSA_HEADS = 16
NSA_KV_GROUPS = 4
HEADS_PER_GROUP = NSA_HEADS // NSA_KV_GROUPS
HEAD_DIM = 128
NSA_WIDTH = NSA_HEADS * HEAD_DIM
KV_WIDTH = NSA_KV_GROUPS * HEAD_DIM
CMP_LEN = 32
CMP_STRIDE = 16
SLC_LEN = 64
SLC_TOP_N = 16
SLC_LOCAL = 2
FORCE_BONUS = 1e4
WINDOW = 512
LRU_WIDTH = 2048
LRU_BLOCK_DIM = 128
LRU_C = 8.0
NORM_EPS = 1e-6
N_GATES = 3 * NSA_HEADS
GATES_PER_GROUP = 3 * HEADS_PER_GROUP

LANES = 128
SUBLANES = 8
BF16_ROWS = 16
VMEM_LIMIT_BYTES = 56 * 1024 * 1024

MASK_VALUE = -1e30

_NT = (((1,), (1,)), ((), ()))


def _params(*sem):
    return pltpu.CompilerParams(dimension_semantics=sem, vmem_limit_bytes=VMEM_LIMIT_BYTES)


def _tile(n, target, quantum=LANES):
    best = None
    t = quantum
    while t <= min(n, target):
        if n % t == 0:
            best = t
        t += quantum
    assert best is not None, (n, target, quantum)
    return best


def _dot(a, b):
    return jnp.dot(a, b, preferred_element_type=F32)


def _norm_mod(x, g, sc, sh):
    inv = lax.rsqrt(jnp.mean(x * x, axis=-1, keepdims=True) + NORM_EPS)
    return ((x * inv) * g) * (1.0 + sc) + sh


def _ada_kernel(c_ref, w_ref, b_ref, o_ref):
    c = c_ref[...]
    act = (c * jax.nn.sigmoid(c)).astype(BF16)
    o_ref[...] = _dot(act, w_ref[...].astype(BF16)) + b_ref[...]


def _ada(c, ada_w, ada_b):
    depth, d, n = ada_w.shape
    b = c.shape[0]
    tn = _tile(n, 1024)
    return pl.pallas_call(
        _ada_kernel,
        out_shape=jax.ShapeDtypeStruct((depth, b, n), F32),
        grid=(depth, n // tn),
        in_specs=[
            pl.BlockSpec((b, d), lambda l, j: (0, 0)),
            pl.BlockSpec((None, d, tn), lambda l, j: (l, 0, j)),
            pl.BlockSpec((None, 1, tn), lambda l, j: (l, 0, j)),
        ],
        out_specs=pl.BlockSpec((None, b, tn), lambda l, j: (l, 0, j)),
        compiler_params=_params("parallel", "parallel"),
        name="ada",
    )(c, ada_w, ada_b.reshape(depth, 1, n))


def _in_proj_kernel(x_ref, g_ref, sc_ref, sh_ref, w_ref, cs_ref, o_ref, h_ref):
    @pl.when(pl.program_id(1) == 0)
    def _():
        h_ref[...] = _norm_mod(x_ref[...], g_ref[...], sc_ref[...], sh_ref[...]).astype(BF16)

    o_ref[...] = (_dot(h_ref[...], w_ref[...]) * cs_ref[...]).astype(o_ref.dtype)


def _in_proj(x, g, sc, sh, w, colscale, out_dtype, seq, tm_target=1024, tn_target=1024):
    m, d = x.shape
    n = w.shape[1]
    tm = _tile(seq, tm_target, BF16_ROWS)
    tn = _tile(n, tn_target)
    bidx = lambda i, j: ((i * tm) // seq, 0, 0)
    return pl.pallas_call(
        _in_proj_kernel,
        out_shape=jax.ShapeDtypeStruct((m, n), out_dtype),
        grid=(m // tm, n // tn),
        in_specs=[
            pl.BlockSpec((tm, d), lambda i, j: (i, 0)),
            pl.BlockSpec((1, d), lambda i, j: (0, 0)),
            pl.BlockSpec((None, 1, d), bidx),
            pl.BlockSpec((None, 1, d), bidx),
            pl.BlockSpec((d, tn), lambda i, j: (0, j)),
            pl.BlockSpec((1, tn), lambda i, j: (0, j)),
        ],
        out_specs=pl.BlockSpec((tm, tn), lambda i, j: (i, j)),
        scratch_shapes=[pltpu.VMEM((tm, d), BF16)],
        compiler_params=_params("parallel", "arbitrary"),
        name="in_proj",
    )(x, g, sc, sh, w, colscale)


def _compress_kernel(c_ref, pos_ref, w1_ref, w2_ref, o_ref):
    c = c_ref[...].astype(F32)
    lo = (c + pos_ref[0:1, :]).astype(BF16)
    hi = (c + pos_ref[1:2, :]).astype(BF16)
    first = _dot(lo, w1_ref[0])
    second = _dot(hi, w1_ref[1])
    n = first.shape[0]
    hid = first + pltpu.roll(second, n - 1, axis=0)
    o_ref[...] = _dot(jax.nn.gelu(hid).astype(BF16), w2_ref[...]).astype(o_ref.dtype)


def _compress(chunks, pos, w1, w2):
    b, _, g, n, cw = chunks.shape
    hd = w2.shape[-1]
    return pl.pallas_call(
        _compress_kernel,
        out_shape=jax.ShapeDtypeStruct((b, 2, g, n, hd), BF16),
        grid=(b, 2, g),
        in_specs=[
            pl.BlockSpec((None, None, None, n, cw), lambda bi, s, gi: (bi, s, gi, 0, 0)),
            pl.BlockSpec((None, 2, cw), lambda bi, s, gi: (s, 0, 0)),
            pl.BlockSpec((None, 2, cw, hd), lambda bi, s, gi: (s, 0, 0, 0)),
            pl.BlockSpec((None, hd, hd), lambda bi, s, gi: (s, 0, 0)),
        ],
        out_specs=pl.BlockSpec((None, None, None, n, hd), lambda bi, s, gi: (bi, s, gi, 0, 0)),
        compiler_params=_params("parallel", "parallel", "parallel"),
        name="compress",
    )(chunks, pos, w1, w2)


def _nsa_kernel(q_ref, ks_ref, vs_ref, kw_ref, vw_ref, kc_ref, vc_ref, zg_ref, o_ref,
                kaug_sc, vsaug_sc, vwaug_sc, ov_sc, bias_sc, qa_sc, m_sc, acc_sc, out_sc, *, tq):
    hd = HEAD_DIM
    hpg = HEADS_PER_GROUP
    tk = tq
    n_blk = LANES // 4
    sub = tq // 4
    seq = ks_ref.shape[0]
    n_cmp = kc_ref.shape[0]
    grp = pl.program_id(1)
    qi = pl.program_id(2)
    t0 = qi * tq

    @pl.when(qi == 0)
    def _():
        kaug_sc[:, :hd] = ks_ref[...]
        key_blk = lax.shift_right_logical(lax.broadcasted_iota(jnp.int32, (seq, LANES), 0), 6)
        lane_blk = lax.broadcasted_iota(jnp.int32, (seq, LANES), 1) & (n_blk - 1)
        kaug_sc[:, hd:] = jnp.where(key_blk == lane_blk, 1.0, 0.0).astype(BF16)
        ones = jnp.ones((seq, LANES), BF16)
        vsaug_sc[:, :hd] = vs_ref[...]
        vsaug_sc[:, hd:] = ones
        vwaug_sc[:, :hd] = vw_ref[...]
        vwaug_sc[:, hd:] = ones
        orow = lax.broadcasted_iota(jnp.int32, (hpg * n_cmp, LANES), 0) & (n_cmp - 1)
        ocol = lax.broadcasted_iota(jnp.int32, (hpg * n_cmp, LANES), 1)
        cstart = orow * CMP_STRIDE
        bstart = (ocol & (n_blk - 1)) * SLC_LEN
        hit = (cstart < bstart + SLC_LEN) & (cstart + CMP_LEN > bstart)
        for c in range(4):
            ov_sc[c] = jnp.where(hit & (lax.shift_right_logical(ocol, 5) == c), 1.0, 0.0).astype(BF16)
        r = lax.broadcasted_iota(jnp.int32, (tq, tk), 0)
        cidx = lax.broadcasted_iota(jnp.int32, (tq, tk), 1)
        bias_sc[0] = jnp.where(cidx <= r, 0.0, MASK_VALUE)
        bias_sc[1] = jnp.where(cidx > r, 0.0, MASK_VALUE)

    q = q_ref[...]
    qs = jnp.concatenate([q[:, h * hd:(h + 1) * hd] for h in range(hpg)], axis=0)
    qpos = t0 + (lax.broadcasted_iota(jnp.int32, (hpg * tq, 1), 0) & (tq - 1))
    gt = jax.nn.sigmoid(pltpu.roll(zg_ref[...].astype(F32), (LANES - GATES_PER_GROUP * grp) & (LANES - 1), axis=1))

    s = lax.dot_general(qs, kc_ref[...], _NT, preferred_element_type=F32)
    cmp_end = lax.broadcasted_iota(jnp.int32, (1, n_cmp), 1) * CMP_STRIDE + (CMP_LEN - 1)
    s = jnp.where(cmp_end <= qpos, s, -jnp.inf)
    mx = jnp.max(s, axis=-1, keepdims=True)
    mx = jnp.where(jnp.isfinite(mx), mx, 0.0)
    p = jnp.exp(s - mx)
    p = p / jnp.maximum(jnp.sum(p, axis=-1, keepdims=True), 1e-30)
    o_cmp = _dot(p.astype(BF16), vc_ref[...])
    for h in range(hpg):
        out_sc[:, h * hd:(h + 1) * hd] = gt[:, 3 * h:3 * h + 1] * o_cmp[h * tq:(h + 1) * tq]

    p_heads = jnp.concatenate([p[h * tq:(h + 1) * tq] for h in range(hpg)], axis=1).astype(BF16)
    imp = _dot(p_heads[0:sub], ov_sc[0])
    for c in range(1, 4):
        imp = imp + _dot(p_heads[c * sub:(c + 1) * sub], ov_sc[c])
    lane = lax.broadcasted_iota(jnp.int32, (sub, LANES), 1)
    blk = lane & (n_blk - 1)
    seg = lax.shift_right_logical(lane, 5)
    cur = lax.shift_right_logical(t0, 6) + seg
    valid = blk <= cur
    forced = (blk == 0) | (valid & (blk > cur - SLC_LOCAL))
    score = jnp.where(forced, imp + FORCE_BONUS, jnp.where(valid, imp, -1.0))
    key = lax.bitcast_convert_type(score, jnp.int32)
    rank = jnp.zeros((sub, LANES), jnp.int32)
    for r in range(1, n_blk):
        lower = blk >= r
        other = jnp.where(lower, pltpu.roll(key, r, axis=1), pltpu.roll(key, LANES - n_blk + r, axis=1))
        rank = rank + jnp.where(other + jnp.where(lower, 1, 0) > key, 1, 0)
    sel_bias = jnp.where((rank < SLC_TOP_N) & valid, 0.0, MASK_VALUE)
    sel_rows = jnp.concatenate([jnp.where(seg == c, sel_bias, 0.0) for c in range(4)], axis=0).astype(BF16)
    for h in range(hpg):
        qa_sc[h] = jnp.concatenate([q[:, h * hd:(h + 1) * hd], sel_rows], axis=1)

    def reset():
        m_sc[...] = jnp.full(m_sc.shape, MASK_VALUE, F32)
        acc_sc[...] = jnp.zeros(acc_sc.shape, F32)

    def update(h, qh, k_t, v_t, bias):
        sc = lax.dot_general(qh, k_t, _NT, preferred_element_type=F32)
        if bias is not None:
            sc = sc + bias
        m_prev = m_sc[h]
        m_new = jnp.maximum(m_prev, jnp.max(sc, axis=-1, keepdims=True))
        alpha = jnp.exp(m_prev - m_new)
        pe = jnp.exp(sc - jnp.concatenate([m_new] * (tk // LANES), axis=1))
        acc_sc[h] = jnp.concatenate([alpha, alpha], axis=1) * acc_sc[h] + _dot(pe.astype(BF16), v_t)
        m_sc[h] = m_new

    def finish(h):
        acc = acc_sc[h]
        return acc[:, :hd] / acc[:, hd:]

    reset()

    def slc_tile(kt, carry):
        k0 = pl.multiple_of(kt * tk, tk)
        k_t = kaug_sc[pl.ds(k0, tk), :]
        v_t = vsaug_sc[pl.ds(k0, tk), :]
        for h in range(hpg):
            update(h, qa_sc[h], k_t, v_t, None)
        return carry

    lax.fori_loop(0, qi, slc_tile, 0)
    d0 = pl.multiple_of(t0, tk)
    k_t = kaug_sc[pl.ds(d0, tk), :]
    v_t = vsaug_sc[pl.ds(d0, tk), :]
    for h in range(hpg):
        update(h, qa_sc[h], k_t, v_t, bias_sc[0])
        out_sc[:, h * hd:(h + 1) * hd] += gt[:, 3 * h + 1:3 * h + 2] * finish(h)

    assert WINDOW == 2 * tk
    reset()

    def win_tile(kt, bias):
        k0 = pl.multiple_of(kt * tk, tk)
        k_t = kw_ref[pl.ds(k0, tk), :]
        v_t = vwaug_sc[pl.ds(k0, tk), :]
        for h in range(hpg):
            update(h, qa_sc[h][:, :hd], k_t, v_t, bias)

    @pl.when(qi >= 2)
    def _():
        win_tile(qi - 2, bias_sc[1])

    @pl.when(qi >= 1)
    def _():
        win_tile(qi - 1, None)

    win_tile(qi, bias_sc[0])
    for h in range(hpg):
        o_h = out_sc[:, h * hd:(h + 1) * hd] + gt[:, 3 * h + 2:3 * h + 3] * finish(h)
        o_ref[:, h * hd:(h + 1) * hd] = o_h.astype(o_ref.dtype)


def _nsa(z, kv_cmp, gates_col, batch, seq, tq=256):
    m = z.shape[0]
    assert gates_col % LANES == 0
    g = NSA_KV_GROUPS
    hd = HEAD_DIM
    hpg = HEADS_PER_GROUP
    n_cmp = kv_cmp.shape[3]
    nq = seq // tq
    qcols = NSA_WIDTH // hd
    assert tq // 4 == SLC_LEN and n_cmp == LANES

    def kv_spec(slot):
        return pl.BlockSpec((seq, hd), lambda b, gi, i: (b, qcols + slot * g + gi))

    def cmp_spec(slot):
        return pl.BlockSpec((None, None, None, n_cmp, hd), lambda b, gi, i: (b, slot, gi, 0, 0))

    return pl.pallas_call(
        functools.partial(_nsa_kernel, tq=tq),
        out_shape=jax.ShapeDtypeStruct((m, NSA_WIDTH), BF16),
        grid=(batch, g, nq),
        in_specs=[
            pl.BlockSpec((tq, hpg * hd), lambda b, gi, i: (b * nq + i, gi)),
            kv_spec(2), kv_spec(3), kv_spec(4), kv_spec(5),
            cmp_spec(0), cmp_spec(1),
            pl.BlockSpec((tq, LANES), lambda b, gi, i: (b * nq + i, gates_col // LANES)),
        ],
        out_specs=pl.BlockSpec((tq, hpg * hd), lambda b, gi, i: (b * nq + i, gi)),
        scratch_shapes=[
            pltpu.VMEM((seq, 2 * hd), BF16),
            pltpu.VMEM((seq, 2 * hd), BF16),
            pltpu.VMEM((seq, 2 * hd), BF16),
            pltpu.VMEM((4, hpg * n_cmp, LANES), BF16),
            pltpu.VMEM((2, tq, tq), F32),
            pltpu.VMEM((hpg, tq, 2 * hd), BF16),
            pltpu.VMEM((hpg, tq, LANES), F32),
            pltpu.VMEM((hpg, tq, 2 * hd), F32),
            pltpu.VMEM((tq, hpg * hd), F32),
        ],
        compiler_params=_params("parallel", "parallel", "arbitrary"),
        name="nsa",
    )(z, z, z, z, z, kv_cmp, kv_cmp, z)


def _lru_kernel(lx_ref, ly_ref, cw_ref, cb_ref, wa_ref, ba_ref, wi_ref, bi_ref, lam_ref, o_ref, a_sc, h_sc):
    seq, wt = lx_ref.shape
    x = lx_ref[...].astype(F32)
    row = lax.broadcasted_iota(jnp.int32, (seq, 1), 0)
    taps = cw_ref.shape[0]
    u = cb_ref[...]
    for j in range(taps):
        back = taps - 1 - j
        xs = x if back == 0 else jnp.where(row >= back, pltpu.roll(x, back, axis=0), 0.0)
        u = u + xs * cw_ref[j:j + 1, :]
    ub = u.astype(BF16)
    r = jax.nn.sigmoid(_dot(ub, wa_ref[0]) + ba_ref[...])
    gate_i = jax.nn.sigmoid(_dot(ub, wi_ref[0]) + bi_ref[...])
    log_a = (LRU_C * r) * jax.nn.log_sigmoid(lam_ref[...])
    a = jnp.exp(log_a)
    a_sc[...] = a
    h_sc[...] = jnp.sqrt(1.0 - a * a) * (gate_i * u)

    sub = lax.broadcasted_iota(jnp.int32, (SUBLANES, wt), 0)

    def chunk(k, h):
        r0 = pl.multiple_of(k * SUBLANES, SUBLANES)
        a = a_sc[pl.ds(r0, SUBLANES), :]
        b = h_sc[pl.ds(r0, SUBLANES), :]
        for s in (1, 2, 4):
            a_prev = jnp.where(sub >= s, pltpu.roll(a, s, axis=0), 1.0)
            b_prev = jnp.where(sub >= s, pltpu.roll(b, s, axis=0), 0.0)
            b = a * b_prev + b
            a = a * a_prev
        hs = a * h + b
        h_sc[pl.ds(r0, SUBLANES), :] = hs
        return hs[SUBLANES - 1:SUBLANES, :]

    lax.fori_loop(0, seq // SUBLANES, chunk, jnp.zeros((1, wt), F32), unroll=4)
    o_ref[...] = (jax.nn.gelu(ly_ref[...].astype(F32)) * h_sc[...]).astype(o_ref.dtype)


def _lru(z, lx_col, ly_col, conv_w, conv_b, wa, ba, wi, bi, lam, batch, seq):
    m = z.shape[0]
    wt = LRU_BLOCK_DIM
    nw = LRU_WIDTH // wt
    assert lx_col % wt == 0 and ly_col % wt == 0
    taps = conv_w.shape[0]
    vec = pl.BlockSpec((1, wt), lambda b, j: (0, j))
    blk = pl.BlockSpec((1, wt, wt), lambda b, j: (j, 0, 0))
    return pl.pallas_call(
        _lru_kernel,
        out_shape=jax.ShapeDtypeStruct((m, LRU_WIDTH), BF16),
        grid=(batch, nw),
        in_specs=[
            pl.BlockSpec((seq, wt), lambda b, j: (b, lx_col // wt + j)),
            pl.BlockSpec((seq, wt), lambda b, j: (b, ly_col // wt + j)),
            pl.BlockSpec((taps, wt), lambda b, j: (0, j)),
            vec, blk, vec, blk, vec, vec,
        ],
        out_specs=pl.BlockSpec((seq, wt), lambda b, j: (b, j)),
        scratch_shapes=[pltpu.VMEM((seq, wt), F32), pltpu.VMEM((seq, wt), F32)],
        compiler_params=_params("parallel", "parallel"),
        name="lru",
    )(z, z, conv_w, conv_b, wa, ba, wi, bi, lam)


def _merge_kernel(oa_ref, ob_ref, pa_ref, pb_ref, ga_ref, gb_ref, o_ref):
    a = _dot(oa_ref[...], pa_ref[...])
    b = _dot(ob_ref[...], pb_ref[...])
    merged = jax.nn.sigmoid(ga_ref[...].astype(F32)) * a + jax.nn.sigmoid(gb_ref[...].astype(F32)) * b
    o_ref[...] = merged.astype(o_ref.dtype)


def _merge(o_a, o_b, proj_a, proj_b, z, ga_col, gb_col, d_model, tm_target=1024, tn_target=512):
    m, ka = o_a.shape
    kb = o_b.shape[1]
    tm = _tile(m, tm_target, BF16_ROWS)
    tn = _tile(d_model, tn_target)
    ga0 = ga_col // tn
    gb0 = gb_col // tn
    assert ga_col % tn == 0 and gb_col % tn == 0
    return pl.pallas_call(
        _merge_kernel,
        out_shape=jax.ShapeDtypeStruct((m, d_model), BF16),
        grid=(m // tm, d_model // tn),
        in_specs=[
            pl.BlockSpec((tm, ka), lambda i, j: (i, 0)),
            pl.BlockSpec((tm, kb), lambda i, j: (i, 0)),
            pl.BlockSpec((ka, tn), lambda i, j: (0, j)),
            pl.BlockSpec((kb, tn), lambda i, j: (0, j)),
            pl.BlockSpec((tm, tn), lambda i, j: (i, ga0 + j)),
            pl.BlockSpec((tm, tn), lambda i, j: (i, gb0 + j)),
        ],
        out_specs=pl.BlockSpec((tm, tn), lambda i, j: (i, j)),
        compiler_params=_params("parallel", "parallel"),
        name="merge",
    )(o_a, o_b, proj_a, proj_b, z, z)


def _matmul_residual_kernel(a_ref, w_ref, x_ref, g_ref, o_ref):
    o_ref[...] = x_ref[...] + g_ref[...] * _dot(a_ref[...], w_ref[...])


def _matmul_residual(a, w, x, gate, seq, name, tm_target=1024, tn_target=512):
    m, k = a.shape
    n = w.shape[1]
    tm = _tile(seq, tm_target, BF16_ROWS)
    tn = _tile(n, tn_target)
    return pl.pallas_call(
        _matmul_residual_kernel,
        out_shape=jax.ShapeDtypeStruct((m, n), F32),
        grid=(m // tm, n // tn),
        in_specs=[
            pl.BlockSpec((tm, k), lambda i, j: (i, 0)),
            pl.BlockSpec((k, tn), lambda i, j: (0, j)),
            pl.BlockSpec((tm, tn), lambda i, j: (i, j)),
            pl.BlockSpec((None, 1, tn), lambda i, j: ((i * tm) // seq, 0, j)),
        ],
        out_specs=pl.BlockSpec((tm, tn), lambda i, j: (i, j)),
        compiler_params=_params("parallel", "parallel"),
        name=name,
    )(a, w, x, gate)


def _ffn_up_kernel(x_ref, xh_ref, g_ref, sc_ref, sh_ref, wg_ref, wv_ref, cw_ref, cb_ref, o_ref, h_ref, *, seq):
    halo = BF16_ROWS
    tm = o_ref.shape[0]

    @pl.when(pl.program_id(1) == 0)
    def _():
        h_ref[0:halo, :] = _norm_mod(xh_ref[...], g_ref[...], sc_ref[...], sh_ref[...]).astype(BF16)
        h_ref[halo:, :] = _norm_mod(x_ref[...], g_ref[...], sc_ref[...], sh_ref[...]).astype(BF16)

    h = h_ref[...]
    gate = _dot(h, wg_ref[...])
    val = _dot(h[halo:], wv_ref[...])
    seq_start = (pl.program_id(0) * tm) % seq == 0
    row = lax.broadcasted_iota(jnp.int32, (tm, 1), 0)
    taps = cw_ref.shape[0]
    conv = cb_ref[...]
    for j in range(taps):
        back = taps - 1 - j
        gj = gate[halo - back:halo - back + tm]
        if back:
            gj = jnp.where(row >= jnp.where(seq_start, back, 0), gj, 0.0)
        conv = conv + gj * cw_ref[j:j + 1, :]
    o_ref[...] = (jax.nn.gelu(conv) * val).astype(o_ref.dtype)


def _ffn_up(x, g, sc, sh, w_up, conv_w, conv_b, seq, tm_target=1024, tn_target=512):
    m, d = x.shape
    d_ff = w_up.shape[1] // 2
    tm = _tile(seq, tm_target, BF16_ROWS)
    tn = _tile(d_ff, tn_target)
    nj = d_ff // tn
    halo = BF16_ROWS
    taps = conv_w.shape[0]
    bidx = lambda i, j: ((i * tm) // seq, 0, 0)
    return pl.pallas_call(
        functools.partial(_ffn_up_kernel, seq=seq),
        out_shape=jax.ShapeDtypeStruct((m, d_ff), BF16),
        grid=(m // tm, nj),
        in_specs=[
            pl.BlockSpec((tm, d), lambda i, j: (i, 0)),
            pl.BlockSpec((halo, d), lambda i, j: (jnp.maximum(i * (tm // halo) - 1, 0), 0)),
            pl.BlockSpec((1, d), lambda i, j: (0, 0)),
            pl.BlockSpec((None, 1, d), bidx),
            pl.BlockSpec((None, 1, d), bidx),
            pl.BlockSpec((d, tn), lambda i, j: (0, j)),
            pl.BlockSpec((d, tn), lambda i, j: (0, nj + j)),
            pl.BlockSpec((taps, tn), lambda i, j: (0, j)),
            pl.BlockSpec((1, tn), lambda i, j: (0, j)),
        ],
        out_specs=pl.BlockSpec((tm, tn), lambda i, j: (i, j)),
        scratch_shapes=[pltpu.VMEM((halo + tm, d), BF16)],
        compiler_params=_params("parallel", "arbitrary"),
        name="ffn_up",
    )(x, x, g, sc, sh, w_up, w_up, conv_w, conv_b)


def _final_norm_kernel(x_ref, g_ref, o_ref):
    x = x_ref[...]
    inv = lax.rsqrt(jnp.mean(x * x, axis=-1, keepdims=True) + NORM_EPS)
    o_ref[...] = (x * inv) * g_ref[...]


def _final_norm(x, g, tm_target=512):
    m, d = x.shape
    tm = _tile(m, tm_target, SUBLANES)
    return pl.pallas_call(
        _final_norm_kernel,
        out_shape=jax.ShapeDtypeStruct((m, d), F32),
        grid=(m // tm,),
        in_specs=[pl.BlockSpec((tm, d), lambda i: (i, 0)), pl.BlockSpec((1, d), lambda i: (0, 0))],
        out_specs=pl.BlockSpec((tm, d), lambda i: (i, 0)),
        compiler_params=_params("parallel"),
        name="final_norm",
    )(x, g)


def _compress_weights(pos, w1, w2):
    half = CMP_LEN // 2
    return (pos.reshape(2, half * HEAD_DIM),
            w1.reshape(2, half * HEAD_DIM, HEAD_DIM).astype(BF16),
            w2.astype(BF16))


def kernel(x, c, ada_w, ada_b, norm1_g, w_in, cmp_pos_k, cmp_w1_k, cmp_w2_k, cmp_pos_v, cmp_w1_v, cmp_w2_v,
           lru_conv_w, lru_conv_b, lru_wa, lru_ba, lru_wi, lru_bi, lru_lambda, proj_a, proj_b, w_out, norm2_g,
           ffn_up, ffn_conv_w, ffn_conv_b, ffn_down, final_g):
    batch, seq, d = x.shape
    depth = ada_w.shape[0]
    m = batch * seq
    assert seq % SLC_LEN == 0 and seq // SLC_LEN == LANES // 4 and CMP_STRIDE * LANES == seq
    n_attn = NSA_WIDTH + 6 * KV_WIDTH
    n_rest = 2 * LRU_WIDTH + 2 * d
    z_cols = dict(lx=n_attn, ly=n_attn + LRU_WIDTH, ga=n_attn + 2 * LRU_WIDTH, gb=n_attn + 2 * LRU_WIDTH + d,
                  gates=n_attn + n_rest)

    mod = _ada(c, ada_w, ada_b)
    xf = x.reshape(m, d)
    q_scale = jnp.concatenate([jnp.full((1, NSA_WIDTH), HEAD_DIM ** -0.5, F32),
                               jnp.ones((1, n_attn - NSA_WIDTH + n_rest + LANES), F32)], axis=1)

    for l in range(depth):
        sh1, sc1, g1, sh2, sc2, g2 = [mod[l, :, k * d:(k + 1) * d].reshape(batch, 1, d) for k in range(6)]
        wl = w_in[l]
        w_all = jnp.concatenate([wl[:, :n_attn], wl[:, n_attn + N_GATES:],
                                 jnp.pad(wl[:, n_attn:n_attn + N_GATES], ((0, 0), (0, LANES - N_GATES)))],
                                axis=1).astype(BF16)
        z = _in_proj(xf, norm1_g[l].reshape(1, d), sc1, sh1, w_all, q_scale, BF16, seq)

        n_chunk = seq // CMP_STRIDE
        kv_c = z[:, NSA_WIDTH:NSA_WIDTH + 2 * KV_WIDTH]
        chunks = kv_c.reshape(batch, n_chunk, CMP_STRIDE, 2, NSA_KV_GROUPS, HEAD_DIM)
        chunks = chunks.transpose(0, 3, 4, 1, 2, 5).reshape(batch, 2, NSA_KV_GROUPS, n_chunk, CMP_STRIDE * HEAD_DIM)
        pk, w1k, w2k = _compress_weights(cmp_pos_k[l], cmp_w1_k[l], cmp_w2_k[l])
        pv, w1v, w2v = _compress_weights(cmp_pos_v[l], cmp_w1_v[l], cmp_w2_v[l])
        kv_cmp = _compress(chunks, jnp.stack([pk, pv]), jnp.stack([w1k, w1v]), jnp.stack([w2k, w2v]))

        o_a = _nsa(z, kv_cmp, z_cols["gates"], batch, seq)
        o_b = _lru(z, z_cols["lx"], z_cols["ly"], lru_conv_w[l], lru_conv_b[l].reshape(1, -1), lru_wa[l].astype(BF16),
                   lru_ba[l].reshape(1, -1), lru_wi[l].astype(BF16), lru_bi[l].reshape(1, -1),
                   lru_lambda[l].reshape(1, -1), batch, seq)
        merged = _merge(o_a, o_b, proj_a[l].astype(BF16), proj_b[l].astype(BF16), z, z_cols["ga"], z_cols["gb"], d)
        xf = _matmul_residual(merged, w_out[l].astype(BF16), xf, g1, seq, "out_proj", tn_target=512)

        act = _ffn_up(xf, norm2_g[l].reshape(1, d), sc2, sh2, ffn_up[l].astype(BF16), ffn_conv_w[l],
                      ffn_conv_b[l].reshape(1, -1), seq)
        xf = _matmul_residual(act, ffn_down[l].astype(BF16), xf, g2, seq, "ffn_down", tn_target=256)

    return _final_norm(xf, final_g.reshape(1, d)).reshape(batch, seq, d)
```

```python
import functools

import jax
import jax.numpy as jnp
from jax import lax
from jax.experimental import pallas as pl
from jax.experimental.pallas import tpu as pltpu

F32 = jnp.float32
BF16 = jnp.bfloat16

NSA_HEADS = 16
NSA_KV_GROUPS = 4
HEADS_PER_GROUP = NSA_HEADS // NSA_KV_GROUPS
HEAD_DIM = 128
NSA_WIDTH = NSA_HEADS * HEAD_DIM
KV_WIDTH = NSA_KV_GROUPS * HEAD_DIM
CMP_LEN = 32
CMP_STRIDE = 16
SLC_LEN = 64
SLC_TOP_N = 16
SLC_LOCAL = 2
FORCE_BONUS = 1e4
WINDOW = 512
LRU_WIDTH = 2048
LRU_BLOCK_DIM = 128
LRU_C = 8.0
NORM_EPS = 1e-6
N_GATES = 3 * NSA_HEADS
GATES_PER_GROUP = 3 * HEADS_PER_GROUP

LANES = 128
SUBLANES = 8
BF16_ROWS = 16
VMEM_LIMIT_BYTES = 56 * 1024 * 1024

MASK_VALUE = -1e30

_NT = (((1,), (1,)), ((), ()))


def _params(*sem):
    return pltpu.CompilerParams(dimension_semantics=sem, vmem_limit_bytes=VMEM_LIMIT_BYTES)


def _tile(n, target, quantum=LANES):
    best = None
    t = quantum
    while t <= min(n, target):
        if n % t == 0:
            best = t
        t += quantum
    assert best is not None, (n, target, quantum)
    return best


def _dot(a, b):
    return jnp.dot(a, b, preferred_element_type=F32)


def _norm_mod(x, g, sc, sh):
    inv = lax.rsqrt(jnp.mean(x * x, axis=-1, keepdims=True) + NORM_EPS)
    return ((x * inv) * g) * (1.0 + sc) + sh


def _ada_kernel(c_ref, w_ref, b_ref, o_ref):
    c = c_ref[...]
    act = (c * jax.nn.sigmoid(c)).astype(BF16)
    o_ref[...] = _dot(act, w_ref[...].astype(BF16)) + b_ref[...]


def _ada(c, ada_w, ada_b):
    depth, d, n = ada_w.shape
    b = c.shape[0]
    tn = _tile(n, 1024)
    return pl.pallas_call(
        _ada_kernel,
        out_shape=jax.ShapeDtypeStruct((depth, b, n), F32),
        grid=(depth, n // tn),
        in_specs=[
            pl.BlockSpec((b, d), lambda l, j: (0, 0)),
            pl.BlockSpec((None, d, tn), lambda l, j: (l, 0, j)),
            pl.BlockSpec((None, 1, tn), lambda l, j: (l, 0, j)),
        ],
        out_specs=pl.BlockSpec((None, b, tn), lambda l, j: (l, 0, j)),
        compiler_params=_params("parallel", "parallel"),
        name="ada",
    )(c, ada_w, ada_b.reshape(depth, 1, n))


def _in_proj_kernel(x_ref, g_ref, sc_ref, sh_ref, w_ref, cs_ref, o_ref, h_ref):
    @pl.when(pl.program_id(1) == 0)
    def _():
        h_ref[...] = _norm_mod(x_ref[...], g_ref[...], sc_ref[...], sh_ref[...]).astype(BF16)

    o_ref[...] = (_dot(h_ref[...], w_ref[...]) * cs_ref[...]).astype(o_ref.dtype)


def _in_proj(x, g, sc, sh, w, layer, colscale, out_dtype, seq, tm_target=1024, tn_target=1536):
    m, d = x.shape
    n = w.shape[-1]
    tm = _tile(seq, tm_target, BF16_ROWS)
    tn = _tile(n, tn_target)
    bidx = lambda i, j: ((i * tm) // seq, 0, 0)
    return pl.pallas_call(
        _in_proj_kernel,
        out_shape=jax.ShapeDtypeStruct((m, n), out_dtype),
        grid=(m // tm, n // tn),
        in_specs=[
            pl.BlockSpec((tm, d), lambda i, j: (i, 0)),
            pl.BlockSpec((1, d), lambda i, j: (0, 0)),
            pl.BlockSpec((None, 1, d), bidx),
            pl.BlockSpec((None, 1, d), bidx),
            pl.BlockSpec((None, d, tn), lambda i, j: (layer, 0, j)),
            pl.BlockSpec((1, tn), lambda i, j: (0, j)),
        ],
        out_specs=pl.BlockSpec((tm, tn), lambda i, j: (i, j)),
        scratch_shapes=[pltpu.VMEM((tm, d), BF16)],
        compiler_params=_params("parallel", "arbitrary"),
        name="in_proj",
    )(x, g, sc, sh, w, colscale)


def _compress_kernel(kv_ref, pos_ref, w1_ref, w2_ref, o_ref, tok_sc, c_sc):
    hd = kv_ref.shape[1]
    n = c_sc.shape[0]
    tok_sc[...] = kv_ref[...].astype(F32)
    for j in range(CMP_STRIDE):
        c_sc[:, j * hd:(j + 1) * hd] = tok_sc[pl.ds(j, n, stride=CMP_STRIDE), :]
    c = c_sc[...]
    lo = (c + pos_ref[0:1, :]).astype(BF16)
    hi = (c + pos_ref[1:2, :]).astype(BF16)
    first = _dot(lo, w1_ref[0])
    second = _dot(hi, w1_ref[1])
    hid = first + pltpu.roll(second, n - 1, axis=0)
    o_ref[...] = _dot(jax.nn.gelu(hid).astype(BF16), w2_ref[...]).astype(o_ref.dtype)


def _compress(z, batch, seq, pos, w1, w2):
    g = NSA_KV_GROUPS
    hd = HEAD_DIM
    n = seq // CMP_STRIDE
    cw = CMP_STRIDE * hd
    qcols = NSA_WIDTH // hd
    return pl.pallas_call(
        _compress_kernel,
        out_shape=jax.ShapeDtypeStruct((batch, 2, g, n, hd), BF16),
        grid=(batch, 2, g),
        in_specs=[
            pl.BlockSpec((seq, hd), lambda bi, s, gi: (bi, qcols + s * g + gi)),
            pl.BlockSpec((None, 2, cw), lambda bi, s, gi: (s, 0, 0)),
            pl.BlockSpec((None, 2, cw, hd), lambda bi, s, gi: (s, 0, 0, 0)),
            pl.BlockSpec((None, hd, hd), lambda bi, s, gi: (s, 0, 0)),
        ],
        out_specs=pl.BlockSpec((None, None, None, n, hd), lambda bi, s, gi: (bi, s, gi, 0, 0)),
        scratch_shapes=[pltpu.VMEM((seq, hd), F32), pltpu.VMEM((n, cw), F32)],
        compiler_params=_params("parallel", "parallel", "parallel"),
        name="compress",
    )(z, pos, w1, w2)


def _nsa_kernel(q_ref, ks_ref, vs_ref, kw_ref, vw_ref, kc_ref, vc_ref, zg_ref, o_ref,
                kaug_sc, vsaug_sc, vwaug_sc, ov_sc, bias_sc, qa_sc, m_sc, acc_sc, out_sc, *, tq):
    hd = HEAD_DIM
    hpg = HEADS_PER_GROUP
    tk = tq
    n_blk = LANES // 4
    sub = tq // 4
    seq = ks_ref.shape[0]
    n_cmp = kc_ref.shape[0]
    grp = pl.program_id(1)
    qi = pl.program_id(2)
    t0 = qi * tq

    @pl.when(qi == 0)
    def _():
        kaug_sc[:, :hd] = ks_ref[...]
        key_blk = lax.shift_right_logical(lax.broadcasted_iota(jnp.int32, (seq, LANES), 0), 6)
        lane_blk = lax.broadcasted_iota(jnp.int32, (seq, LANES), 1) & (n_blk - 1)
        kaug_sc[:, hd:] = jnp.where(key_blk == lane_blk, 1.0, 0.0).astype(BF16)
        ones = jnp.ones((seq, LANES), BF16)
        vsaug_sc[:, :hd] = vs_ref[...]
        vsaug_sc[:, hd:] = ones
        vwaug_sc[:, :hd] = vw_ref[...]
        vwaug_sc[:, hd:] = ones
        orow = lax.broadcasted_iota(jnp.int32, (hpg * n_cmp, LANES), 0) & (n_cmp - 1)
        ocol = lax.broadcasted_iota(jnp.int32, (hpg * n_cmp, LANES), 1)
        cstart = orow * CMP_STRIDE
        bstart = (ocol & (n_blk - 1)) * SLC_LEN
        hit = (cstart < bstart + SLC_LEN) & (cstart + CMP_LEN > bstart)
        for c in range(4):
            ov_sc[c] = jnp.where(hit & (lax.shift_right_logical(ocol, 5) == c), 1.0, 0.0).astype(BF16)
        r = lax.broadcasted_iota(jnp.int32, (tq, tk), 0)
        cidx = lax.broadcasted_iota(jnp.int32, (tq, tk), 1)
        bias_sc[:, 0:tq] = jnp.where(cidx <= r, 0.0, MASK_VALUE)
        bias_sc[:, tq:2 * tq] = jnp.where(cidx > r, 0.0, MASK_VALUE)

    q = q_ref[...]
    qs = jnp.concatenate([q[:, h * hd:(h + 1) * hd] for h in range(hpg)], axis=0)
    qpos = t0 + (lax.broadcasted_iota(jnp.int32, (hpg * tq, 1), 0) & (tq - 1))
    gt = jax.nn.sigmoid(pltpu.roll(zg_ref[...].astype(F32), (LANES - GATES_PER_GROUP * grp) & (LANES - 1), axis=1))

    s = lax.dot_general(qs, kc_ref[...], _NT, preferred_element_type=F32)
    cmp_end = lax.broadcasted_iota(jnp.int32, (1, n_cmp), 1) * CMP_STRIDE + (CMP_LEN - 1)
    s = jnp.where(cmp_end <= qpos, s, -jnp.inf)
    mx = jnp.max(s, axis=-1, keepdims=True)
    mx = jnp.where(jnp.isfinite(mx), mx, 0.0)
    p = jnp.exp(s - mx)
    p = p / jnp.maximum(jnp.sum(p, axis=-1, keepdims=True), 1e-30)
    o_cmp = _dot(p.astype(BF16), vc_ref[...])
    for h in range(hpg):
        out_sc[:, h * hd:(h + 1) * hd] = gt[:, 3 * h:3 * h + 1] * o_cmp[h * tq:(h + 1) * tq]

    p_heads = jnp.concatenate([p[h * tq:(h + 1) * tq] for h in range(hpg)], axis=1).astype(BF16)
    imp = _dot(p_heads[0:sub], ov_sc[0])
    for c in range(1, 4):
        imp = imp + _dot(p_heads[c * sub:(c + 1) * sub], ov_sc[c])
    lane = lax.broadcasted_iota(jnp.int32, (sub, LANES), 1)
    blk = lane & (n_blk - 1)
    seg = lax.shift_right_logical(lane, 5)
    cur = (lax.shift_right_logical(t0, 6) + seg * (sub // SLC_LEN)
           + lax.shift_right_logical(lax.broadcasted_iota(jnp.int32, (sub, LANES), 0), 6))
    valid = blk <= cur
    forced = (blk == 0) | (valid & (blk > cur - SLC_LOCAL))
    score = jnp.where(forced, imp + FORCE_BONUS, jnp.where(valid, imp, -1.0))
    key = lax.bitcast_convert_type(score, jnp.int32)
    rank = jnp.zeros((sub, LANES), jnp.int32)
    for r in range(1, n_blk):
        lower = blk >= r
        other = jnp.where(lower, pltpu.roll(key, r, axis=1), pltpu.roll(key, LANES - n_blk + r, axis=1))
        rank = rank + jnp.where(other + jnp.where(lower, 1, 0) > key, 1, 0)
    sel_bias = jnp.where((rank < SLC_TOP_N) & valid, 0.0, MASK_VALUE)
    sel_rows = jnp.concatenate([jnp.where(seg == c, sel_bias, 0.0) for c in range(4)], axis=0).astype(BF16)
    for h in range(hpg):
        qa_sc[h] = jnp.concatenate([q[:, h * hd:(h + 1) * hd], sel_rows], axis=1)

    def reset():
        m_sc[...] = jnp.full(m_sc.shape, MASK_VALUE, F32)
        acc_sc[...] = jnp.zeros(acc_sc.shape, F32)

    def update(h, qh, k_t, v_t, bias):
        sc = lax.dot_general(qh, k_t, _NT, preferred_element_type=F32)
        if bias is not None:
            sc = sc + bias
        m_prev = m_sc[h]
        m_new = jnp.maximum(m_prev, jnp.max(sc, axis=-1, keepdims=True))
        alpha = jnp.exp(m_prev - m_new)
        pe = jnp.exp(sc - jnp.concatenate([m_new] * (k_t.shape[0] // LANES), axis=1))
        acc_sc[h] = jnp.concatenate([alpha, alpha], axis=1) * acc_sc[h] + _dot(pe.astype(BF16), v_t)
        m_sc[h] = m_new

    def finish(h):
        acc = acc_sc[h]
        return acc[:, :hd] / acc[:, hd:]

    def tile(k_ref, v_ref, start, mask, q_lanes):
        k0 = pl.multiple_of(start, tq)
        k_t = k_ref[pl.ds(k0, tq), :]
        v_t = v_ref[pl.ds(k0, tq), :]
        bias = None if mask is None else bias_sc[:, mask * tq:(mask + 1) * tq]
        for h in range(hpg):
            update(h, qa_sc[h][:, :q_lanes], k_t, v_t, bias)

    reset()

    def slc_tile(kt, carry):
        tile(kaug_sc, vsaug_sc, kt * tq, None, 2 * hd)
        return carry

    lax.fori_loop(0, qi, slc_tile, 0)
    tile(kaug_sc, vsaug_sc, t0, 0, 2 * hd)
    for h in range(hpg):
        out_sc[:, h * hd:(h + 1) * hd] += gt[:, 3 * h + 1:3 * h + 2] * finish(h)

    assert WINDOW == tq
    reset()

    @pl.when(qi >= 1)
    def _():
        tile(kw_ref, vwaug_sc, t0 - tq, 1, hd)

    tile(kw_ref, vwaug_sc, t0, 0, hd)
    for h in range(hpg):
        o_h = out_sc[:, h * hd:(h + 1) * hd] + gt[:, 3 * h + 2:3 * h + 3] * finish(h)
        o_ref[:, h * hd:(h + 1) * hd] = o_h.astype(o_ref.dtype)


def _nsa(z, kv_cmp, gates_col, batch, seq, tq=WINDOW):
    m = z.shape[0]
    assert gates_col % LANES == 0
    g = NSA_KV_GROUPS
    hd = HEAD_DIM
    hpg = HEADS_PER_GROUP
    n_cmp = kv_cmp.shape[3]
    nq = seq // tq
    qcols = NSA_WIDTH // hd
    assert (tq // 4) % SLC_LEN == 0 and n_cmp == LANES

    def kv_spec(slot):
        return pl.BlockSpec((seq, hd), lambda b, gi, i: (b, qcols + slot * g + gi))

    def cmp_spec(slot):
        return pl.BlockSpec((None, None, None, n_cmp, hd), lambda b, gi, i: (b, slot, gi, 0, 0))

    return pl.pallas_call(
        functools.partial(_nsa_kernel, tq=tq),
        out_shape=jax.ShapeDtypeStruct((m, NSA_WIDTH), BF16),
        grid=(batch, g, nq),
        in_specs=[
            pl.BlockSpec((tq, hpg * hd), lambda b, gi, i: (b * nq + i, gi)),
            kv_spec(2), kv_spec(3), kv_spec(4), kv_spec(5),
            cmp_spec(0), cmp_spec(1),
            pl.BlockSpec((tq, LANES), lambda b, gi, i: (b * nq + i, gates_col // LANES)),
        ],
        out_specs=pl.BlockSpec((tq, hpg * hd), lambda b, gi, i: (b * nq + i, gi)),
        scratch_shapes=[
            pltpu.VMEM((seq, 2 * hd), BF16),
            pltpu.VMEM((seq, 2 * hd), BF16),
            pltpu.VMEM((seq, 2 * hd), BF16),
            pltpu.VMEM((4, hpg * n_cmp, LANES), BF16),
            pltpu.VMEM((tq, 2 * tq), F32),
            pltpu.VMEM((hpg, tq, 2 * hd), BF16),
            pltpu.VMEM((hpg, tq, LANES), F32),
            pltpu.VMEM((hpg, tq, 2 * hd), F32),
            pltpu.VMEM((tq, hpg * hd), F32),
        ],
        compiler_params=_params("parallel", "parallel", "arbitrary"),
        name="nsa",
    )(z, z, z, z, z, kv_cmp, kv_cmp, z)


def _lru_kernel(lx_ref, ly_ref, cw_ref, cb_ref, wa_ref, ba_ref, wi_ref, bi_ref, lam_ref, o_ref, a_sc, h_sc):
    seq, wt = lx_ref.shape
    x = lx_ref[...].astype(F32)
    row = lax.broadcasted_iota(jnp.int32, (seq, 1), 0)
    taps = cw_ref.shape[0]
    u = cb_ref[...]
    for j in range(taps):
        back = taps - 1 - j
        xs = x if back == 0 else jnp.where(row >= back, pltpu.roll(x, back, axis=0), 0.0)
        u = u + xs * cw_ref[j:j + 1, :]
    ub = u.astype(BF16)
    r = jax.nn.sigmoid(_dot(ub, wa_ref[0]) + ba_ref[...])
    gate_i = jax.nn.sigmoid(_dot(ub, wi_ref[0]) + bi_ref[...])
    log_a = (LRU_C * r) * jax.nn.log_sigmoid(lam_ref[...])
    a = jnp.exp(log_a)
    a_sc[...] = a
    h_sc[...] = jnp.sqrt(1.0 - a * a) * (gate_i * u)

    sub = lax.broadcasted_iota(jnp.int32, (SUBLANES, wt), 0)

    def chunk(k, h):
        r0 = pl.multiple_of(k * SUBLANES, SUBLANES)
        a = a_sc[pl.ds(r0, SUBLANES), :]
        b = h_sc[pl.ds(r0, SUBLANES), :]
        for s in (1, 2, 4):
            a_prev = jnp.where(sub >= s, pltpu.roll(a, s, axis=0), 1.0)
            b_prev = jnp.where(sub >= s, pltpu.roll(b, s, axis=0), 0.0)
            b = a * b_prev + b
            a = a * a_prev
        hs = a * h + b
        h_sc[pl.ds(r0, SUBLANES), :] = hs
        return hs[SUBLANES - 1:SUBLANES, :]

    lax.fori_loop(0, seq // SUBLANES, chunk, jnp.zeros((1, wt), F32), unroll=4)
    o_ref[...] = (jax.nn.gelu(ly_ref[...].astype(F32)) * h_sc[...]).astype(o_ref.dtype)


def _lru(z, lx_col, ly_col, conv_w, conv_b, wa, ba, wi, bi, lam, batch, seq):
    m = z.shape[0]
    wt = LRU_BLOCK_DIM
    nw = LRU_WIDTH // wt
    assert lx_col % wt == 0 and ly_col % wt == 0
    taps = conv_w.shape[0]
    vec = pl.BlockSpec((1, wt), lambda b, j: (0, j))
    blk = pl.BlockSpec((1, wt, wt), lambda b, j: (j, 0, 0))
    return pl.pallas_call(
        _lru_kernel,
        out_shape=jax.ShapeDtypeStruct((m, LRU_WIDTH), BF16),
        grid=(batch, nw),
        in_specs=[
            pl.BlockSpec((seq, wt), lambda b, j: (b, lx_col // wt + j)),
            pl.BlockSpec((seq, wt), lambda b, j: (b, ly_col // wt + j)),
            pl.BlockSpec((taps, wt), lambda b, j: (0, j)),
            vec, blk, vec, blk, vec, vec,
        ],
        out_specs=pl.BlockSpec((seq, wt), lambda b, j: (b, j)),
        scratch_shapes=[pltpu.VMEM((seq, wt), F32), pltpu.VMEM((seq, wt), F32)],
        compiler_params=_params("parallel", "parallel"),
        name="lru",
    )(z, z, conv_w, conv_b, wa, ba, wi, bi, lam)


def _merge_kernel(oa_ref, ob_ref, pa_ref, pb_ref, ga_ref, gb_ref, o_ref):
    a = _dot(oa_ref[...], pa_ref[...])
    b = _dot(ob_ref[...], pb_ref[...])
    merged = jax.nn.sigmoid(ga_ref[...].astype(F32)) * a + jax.nn.sigmoid(gb_ref[...].astype(F32)) * b
    o_ref[...] = merged.astype(o_ref.dtype)


def _merge(o_a, o_b, proj_a, proj_b, layer, z, ga_col, gb_col, d_model, tm_target=1024, tn_target=512):
    m, ka = o_a.shape
    kb = o_b.shape[1]
    tm = _tile(m, tm_target, BF16_ROWS)
    tn = _tile(d_model, tn_target)
    ga0 = ga_col // tn
    gb0 = gb_col // tn
    assert ga_col % tn == 0 and gb_col % tn == 0
    return pl.pallas_call(
        _merge_kernel,
        out_shape=jax.ShapeDtypeStruct((m, d_model), BF16),
        grid=(m // tm, d_model // tn),
        in_specs=[
            pl.BlockSpec((tm, ka), lambda i, j: (i, 0)),
            pl.BlockSpec((tm, kb), lambda i, j: (i, 0)),
            pl.BlockSpec((None, ka, tn), lambda i, j: (layer, 0, j)),
            pl.BlockSpec((None, kb, tn), lambda i, j: (layer, 0, j)),
            pl.BlockSpec((tm, tn), lambda i, j: (i, ga0 + j)),
            pl.BlockSpec((tm, tn), lambda i, j: (i, gb0 + j)),
        ],
        out_specs=pl.BlockSpec((tm, tn), lambda i, j: (i, j)),
        compiler_params=_params("parallel", "parallel"),
        name="merge",
    )(o_a, o_b, proj_a, proj_b, z, z)


def _matmul_residual_kernel(a_ref, w_ref, x_ref, g_ref, o_ref):
    o_ref[...] = x_ref[...] + g_ref[...] * _dot(a_ref[...], w_ref[...])


def _matmul_residual(a, w, layer, x, gate, seq, name, tm_target=1024, tn_target=512):
    m, k = a.shape
    n = w.shape[-1]
    tm = _tile(seq, tm_target, BF16_ROWS)
    tn = _tile(n, tn_target)
    return pl.pallas_call(
        _matmul_residual_kernel,
        out_shape=jax.ShapeDtypeStruct((m, n), F32),
        grid=(m // tm, n // tn),
        in_specs=[
            pl.BlockSpec((tm, k), lambda i, j: (i, 0)),
            pl.BlockSpec((None, k, tn), lambda i, j: (layer, 0, j)),
            pl.BlockSpec((tm, tn), lambda i, j: (i, j)),
            pl.BlockSpec((None, 1, tn), lambda i, j: ((i * tm) // seq, 0, j)),
        ],
        out_specs=pl.BlockSpec((tm, tn), lambda i, j: (i, j)),
        compiler_params=_params("parallel", "parallel"),
        name=name,
    )(a, w, x, gate)


def _ffn_up_kernel(x_ref, xh_ref, g_ref, sc_ref, sh_ref, wg_ref, wv_ref, cw_ref, cb_ref, o_ref, h_ref, *, seq):
    halo = BF16_ROWS
    tm = o_ref.shape[0]

    @pl.when(pl.program_id(1) == 0)
    def _():
        h_ref[0:halo, :] = _norm_mod(xh_ref[...], g_ref[...], sc_ref[...], sh_ref[...]).astype(BF16)
        h_ref[halo:, :] = _norm_mod(x_ref[...], g_ref[...], sc_ref[...], sh_ref[...]).astype(BF16)

    h = h_ref[...]
    gate = _dot(h, wg_ref[...])
    val = _dot(h[halo:], wv_ref[...])
    seq_start = (pl.program_id(0) * tm) % seq == 0
    row = lax.broadcasted_iota(jnp.int32, (tm, 1), 0)
    taps = cw_ref.shape[0]
    conv = cb_ref[...]
    for j in range(taps):
        back = taps - 1 - j
        gj = gate[halo - back:halo - back + tm]
        if back:
            gj = jnp.where(row >= jnp.where(seq_start, back, 0), gj, 0.0)
        conv = conv + gj * cw_ref[j:j + 1, :]
    o_ref[...] = (jax.nn.gelu(conv) * val).astype(o_ref.dtype)


def _ffn_up(x, g, sc, sh, w_up, layer, conv_w, conv_b, seq, tm_target=1024, tn_target=768):
    m, d = x.shape
    d_ff = w_up.shape[-1] // 2
    tm = _tile(seq, tm_target, BF16_ROWS)
    tn = _tile(d_ff, tn_target)
    nj = d_ff // tn
    halo = BF16_ROWS
    taps = conv_w.shape[0]
    bidx = lambda i, j: ((i * tm) // seq, 0, 0)
    return pl.pallas_call(
        functools.partial(_ffn_up_kernel, seq=seq),
        out_shape=jax.ShapeDtypeStruct((m, d_ff), BF16),
        grid=(m // tm, nj),
        in_specs=[
            pl.BlockSpec((tm, d), lambda i, j: (i, 0)),
            pl.BlockSpec((halo, d), lambda i, j: (jnp.maximum(i * (tm // halo) - 1, 0), 0)),
            pl.BlockSpec((1, d), lambda i, j: (0, 0)),
            pl.BlockSpec((None, 1, d), bidx),
            pl.BlockSpec((None, 1, d), bidx),
            pl.BlockSpec((None, d, tn), lambda i, j: (layer, 0, j)),
            pl.BlockSpec((None, d, tn), lambda i, j: (layer, 0, nj + j)),
            pl.BlockSpec((taps, tn), lambda i, j: (0, j)),
            pl.BlockSpec((1, tn), lambda i, j: (0, j)),
        ],
        out_specs=pl.BlockSpec((tm, tn), lambda i, j: (i, j)),
        scratch_shapes=[pltpu.VMEM((halo + tm, d), BF16)],
        compiler_params=_params("parallel", "arbitrary"),
        name="ffn_up",
    )(x, x, g, sc, sh, w_up, w_up, conv_w, conv_b)


def _final_norm_kernel(x_ref, g_ref, o_ref):
    x = x_ref[...]
    inv = lax.rsqrt(jnp.mean(x * x, axis=-1, keepdims=True) + NORM_EPS)
    o_ref[...] = (x * inv) * g_ref[...]


def _final_norm(x, g, tm_target=512):
    m, d = x.shape
    tm = _tile(m, tm_target, SUBLANES)
    return pl.pallas_call(
        _final_norm_kernel,
        out_shape=jax.ShapeDtypeStruct((m, d), F32),
        grid=(m // tm,),
        in_specs=[pl.BlockSpec((tm, d), lambda i: (i, 0)), pl.BlockSpec((1, d), lambda i: (0, 0))],
        out_specs=pl.BlockSpec((tm, d), lambda i: (i, 0)),
        compiler_params=_params("parallel"),
        name="final_norm",
    )(x, g)


def _compress_weights(pos, w1, w2):
    half = CMP_LEN // 2
    return (pos.reshape(2, half * HEAD_DIM),
            w1.reshape(2, half * HEAD_DIM, HEAD_DIM).astype(BF16),
            w2.astype(BF16))


def kernel(x, c, ada_w, ada_b, norm1_g, w_in, cmp_pos_k, cmp_w1_k, cmp_w2_k, cmp_pos_v, cmp_w1_v, cmp_w2_v,
           lru_conv_w, lru_conv_b, lru_wa, lru_ba, lru_wi, lru_bi, lru_lambda, proj_a, proj_b, w_out, norm2_g,
           ffn_up, ffn_conv_w, ffn_conv_b, ffn_down, final_g):
    batch, seq, d = x.shape
    depth = ada_w.shape[0]
    m = batch * seq
    assert seq % SLC_LEN == 0 and seq // SLC_LEN == LANES // 4 and CMP_STRIDE * LANES == seq
    n_attn = NSA_WIDTH + 6 * KV_WIDTH
    n_rest = 2 * LRU_WIDTH + 2 * d
    n_z = -(-(n_attn + n_rest + N_GATES) // (4 * LANES)) * (4 * LANES)
    z_cols = dict(lx=n_attn, ly=n_attn + LRU_WIDTH, ga=n_attn + 2 * LRU_WIDTH, gb=n_attn + 2 * LRU_WIDTH + d,
                  gates=n_attn + n_rest)

    mod = _ada(c, ada_w, ada_b)
    xf = x.reshape(m, d)
    q_scale = jnp.concatenate([jnp.full((1, NSA_WIDTH), HEAD_DIM ** -0.5, F32), jnp.ones((1, n_z - NSA_WIDTH), F32)],
                              axis=1)
    w_all = jnp.concatenate([w_in[:, :, :n_attn], w_in[:, :, n_attn + N_GATES:],
                             jnp.pad(w_in[:, :, n_attn:n_attn + N_GATES],
                                     ((0, 0), (0, 0), (0, n_z - n_attn - n_rest - N_GATES)))], axis=2).astype(BF16)
    proj_a, proj_b, w_out, ffn_up, ffn_down, lru_wa, lru_wi = [
        w.astype(BF16) for w in (proj_a, proj_b, w_out, ffn_up, ffn_down, lru_wa, lru_wi)]

    for l in range(depth):
        sh1, sc1, g1, sh2, sc2, g2 = [mod[l, :, k * d:(k + 1) * d].reshape(batch, 1, d) for k in range(6)]
        z = _in_proj(xf, norm1_g[l].reshape(1, d), sc1, sh1, w_all, l, q_scale, BF16, seq)

        pk, w1k, w2k = _compress_weights(cmp_pos_k[l], cmp_w1_k[l], cmp_w2_k[l])
        pv, w1v, w2v = _compress_weights(cmp_pos_v[l], cmp_w1_v[l], cmp_w2_v[l])
        kv_cmp = _compress(z, batch, seq, jnp.stack([pk, pv]), jnp.stack([w1k, w1v]), jnp.stack([w2k, w2v]))

        o_a = _nsa(z, kv_cmp, z_cols["gates"], batch, seq)
        o_b = _lru(z, z_cols["lx"], z_cols["ly"], lru_conv_w[l], lru_conv_b[l].reshape(1, -1), lru_wa[l],
                   lru_ba[l].reshape(1, -1), lru_wi[l], lru_bi[l].reshape(1, -1),
                   lru_lambda[l].reshape(1, -1), batch, seq)
        merged = _merge(o_a, o_b, proj_a, proj_b, l, z, z_cols["ga"], z_cols["gb"], d)
        xf = _matmul_residual(merged, w_out, l, xf, g1, seq, "out_proj", tn_target=512)

        act = _ffn_up(xf, norm2_g[l].reshape(1, d), sc2, sh2, ffn_up, l, ffn_conv_w[l],
                      ffn_conv_b[l].reshape(1, -1), seq)
        xf = _matmul_residual(act, ffn_down, l, xf, g2, seq, "ffn_down", tn_target=256)

    return _final_norm(xf, final_g.reshape(1, d)).reshape(batch, seq, d)
```

```python
import functools

import jax
import jax.numpy as jnp
from jax import lax
from jax.experimental import pallas as pl
from jax.experimental.pallas import tpu as pltpu

F32 = jnp.float32
BF16 = jnp.bfloat16

NSA_HEADS = 16
NSA_KV_GROUPS = 4
HEADS_PER_GROUP = NSA_HEADS // NSA_KV_GROUPS
HEAD_DIM = 128
NSA_WIDTH = NSA_HEADS * HEAD_DIM
KV_WIDTH = NSA_KV_GROUPS * HEAD_DIM
CMP_LEN = 32
CMP_STRIDE = 16
SLC_LEN = 64
SLC_TOP_N = 16
SLC_LOCAL = 2
FORCE_BONUS = 1e4
WINDOW = 512
LRU_WIDTH = 2048
LRU_BLOCK_DIM = 128
LRU_C = 8.0
NORM_EPS = 1e-6
N_GATES = 3 * NSA_HEADS
GATES_PER_GROUP = 3 * HEADS_PER_GROUP

LANES = 128
SUBLANES = 8
BF16_ROWS = 16
VMEM_LIMIT_BYTES = 56 * 1024 * 1024

MASK_VALUE = -1e30

_NT = (((1,), (1,)), ((), ()))


def _params(*sem):
    return pltpu.CompilerParams(dimension_semantics=sem, vmem_limit_bytes=VMEM_LIMIT_BYTES)


def _tile(n, target, quantum=LANES):
    best = None
    t = quantum
    while t <= min(n, target):
        if n % t == 0:
            best = t
        t += quantum
    assert best is not None, (n, target, quantum)
    return best


def _dot(a, b):
    return jnp.dot(a, b, preferred_element_type=F32)


def _norm_mod(x, g, sc, sh):
    inv = lax.rsqrt(jnp.mean(x * x, axis=-1, keepdims=True) + NORM_EPS)
    return ((x * inv) * g) * (1.0 + sc) + sh


def _ada_kernel(c_ref, w_ref, b_ref, o_ref):
    c = c_ref[...]
    act = (c * jax.nn.sigmoid(c)).astype(BF16)
    o_ref[...] = _dot(act, w_ref[...].astype(BF16)) + b_ref[...]


def _ada(c, ada_w, ada_b):
    depth, d, n = ada_w.shape
    b = c.shape[0]
    tn = _tile(n, 1024)
    return pl.pallas_call(
        _ada_kernel,
        out_shape=jax.ShapeDtypeStruct((depth, b, n), F32),
        grid=(depth, n // tn),
        in_specs=[
            pl.BlockSpec((b, d), lambda l, j: (0, 0)),
            pl.BlockSpec((None, d, tn), lambda l, j: (l, 0, j)),
            pl.BlockSpec((None, 1, tn), lambda l, j: (l, 0, j)),
        ],
        out_specs=pl.BlockSpec((None, b, tn), lambda l, j: (l, 0, j)),
        compiler_params=_params("parallel", "parallel"),
        name="ada",
    )(c, ada_w, ada_b.reshape(depth, 1, n))


def _in_proj_kernel(x_ref, g_ref, sc_ref, sh_ref, w_ref, cs_ref, o_ref, h_ref):
    @pl.when(pl.program_id(1) == 0)
    def _():
        h_ref[...] = _norm_mod(x_ref[...], g_ref[...], sc_ref[...], sh_ref[...]).astype(BF16)

    o_ref[...] = (_dot(h_ref[...], w_ref[...]) * cs_ref[...]).astype(o_ref.dtype)


def _in_proj(x, g, sc, sh, w, layer, colscale, out_dtype, seq, tm_target=1024, tn_target=1536):
    m, d = x.shape
    n = w.shape[-1]
    tm = _tile(seq, tm_target, BF16_ROWS)
    tn = _tile(n, tn_target)
    bidx = lambda i, j: ((i * tm) // seq, 0, 0)
    return pl.pallas_call(
        _in_proj_kernel,
        out_shape=jax.ShapeDtypeStruct((m, n), out_dtype),
        grid=(m // tm, n // tn),
        in_specs=[
            pl.BlockSpec((tm, d), lambda i, j: (i, 0)),
            pl.BlockSpec((1, d), lambda i, j: (0, 0)),
            pl.BlockSpec((None, 1, d), bidx),
            pl.BlockSpec((None, 1, d), bidx),
            pl.BlockSpec((None, d, tn), lambda i, j: (layer, 0, j)),
            pl.BlockSpec((1, tn), lambda i, j: (0, j)),
        ],
        out_specs=pl.BlockSpec((tm, tn), lambda i, j: (i, j)),
        scratch_shapes=[pltpu.VMEM((tm, d), BF16)],
        compiler_params=_params("parallel", "arbitrary"),
        name="in_proj",
    )(x, g, sc, sh, w, colscale)


def _compress_kernel(kv_ref, pos_ref, w1_ref, w2_ref, o_ref, tok_sc, c_sc):
    hd = kv_ref.shape[1]
    n = c_sc.shape[0]
    tok_sc[...] = kv_ref[...].astype(F32)
    for j in range(CMP_STRIDE):
        c_sc[:, j * hd:(j + 1) * hd] = tok_sc[pl.ds(j, n, stride=CMP_STRIDE), :]
    c = c_sc[...]
    lo = (c + pos_ref[0:1, :]).astype(BF16)
    hi = (c + pos_ref[1:2, :]).astype(BF16)
    first = _dot(lo, w1_ref[0])
    second = _dot(hi, w1_ref[1])
    hid = first + pltpu.roll(second, n - 1, axis=0)
    o_ref[...] = _dot(jax.nn.gelu(hid).astype(BF16), w2_ref[...]).astype(o_ref.dtype)


def _compress(z, batch, seq, pos, w1, w2):
    g = NSA_KV_GROUPS
    hd = HEAD_DIM
    n = seq // CMP_STRIDE
    cw = CMP_STRIDE * hd
    qcols = NSA_WIDTH // hd
    return pl.pallas_call(
        _compress_kernel,
        out_shape=jax.ShapeDtypeStruct((batch, 2, g, n, hd), BF16),
        grid=(batch, 2, g),
        in_specs=[
            pl.BlockSpec((seq, hd), lambda bi, s, gi: (bi, qcols + s * g + gi)),
            pl.BlockSpec((None, 2, cw), lambda bi, s, gi: (s, 0, 0)),
            pl.BlockSpec((None, 2, cw, hd), lambda bi, s, gi: (s, 0, 0, 0)),
            pl.BlockSpec((None, hd, hd), lambda bi, s, gi: (s, 0, 0)),
        ],
        out_specs=pl.BlockSpec((None, None, None, n, hd), lambda bi, s, gi: (bi, s, gi, 0, 0)),
        scratch_shapes=[pltpu.VMEM((seq, hd), F32), pltpu.VMEM((n, cw), F32)],
        compiler_params=_params("parallel", "parallel", "parallel"),
        name="compress",
    )(z, pos, w1, w2)


def _nsa_kernel(q_ref, ks_ref, vs_ref, kw_ref, vw_ref, kc_ref, vc_ref, zg_ref, o_ref,
                kaug_sc, vsaug_sc, vwaug_sc, vcaug_sc, ov_sc, bias_sc, qa_sc, s_sc, sw_sc, m_sc, acc_sc, out_sc,
                *, tq):
    hd = HEAD_DIM
    hpg = HEADS_PER_GROUP
    tk = tq
    n_blk = LANES // 4
    seq = ks_ref.shape[0]
    n_cmp = kc_ref.shape[0]
    grp = pl.program_id(1)
    qi = pl.program_id(2)
    t0 = qi * tq

    @pl.when(qi == 0)
    def _():
        kaug_sc[:, :hd] = ks_ref[...]
        key_blk = lax.shift_right_logical(lax.broadcasted_iota(jnp.int32, (seq, LANES), 0), 6)
        lane_blk = lax.broadcasted_iota(jnp.int32, (seq, LANES), 1) & (n_blk - 1)
        kaug_sc[:, hd:] = jnp.where(key_blk == lane_blk, 1.0, 0.0).astype(BF16)
        ones = jnp.ones((seq, LANES), BF16)
        vsaug_sc[:, :hd] = vs_ref[...]
        vsaug_sc[:, hd:] = ones
        vwaug_sc[:, :hd] = vw_ref[...]
        vwaug_sc[:, hd:] = ones
        vcaug_sc[:, :hd] = vc_ref[...]
        vcaug_sc[:, hd:] = jnp.ones((n_cmp, LANES), BF16)
        cstart = (lax.broadcasted_iota(jnp.int32, (n_blk, hpg * n_cmp), 1) & (n_cmp - 1)) * CMP_STRIDE
        bstart = lax.broadcasted_iota(jnp.int32, (n_blk, hpg * n_cmp), 0) * SLC_LEN
        ov_sc[...] = jnp.where((cstart < bstart + SLC_LEN) & (cstart + CMP_LEN > bstart), 1.0, 0.0).astype(BF16)
        r = lax.broadcasted_iota(jnp.int32, (tq, tk), 0)
        cidx = lax.broadcasted_iota(jnp.int32, (tq, tk), 1)
        bias_sc[0] = jnp.where(cidx <= r, 0.0, MASK_VALUE)
        bias_sc[1] = jnp.where(cidx > r, 0.0, MASK_VALUE)

    q = q_ref[...]
    qs = jnp.concatenate([q[:, h * hd:(h + 1) * hd] for h in range(hpg)], axis=0)
    qpos = t0 + (lax.broadcasted_iota(jnp.int32, (hpg * tq, 1), 0) & (tq - 1))

    def put_scores(dst, slot, k_ref, start, lhs):
        k_t = k_ref[pl.ds(pl.multiple_of(start, tq), tq), :]
        for h in range(hpg):
            dst[slot, h] = lax.dot_general(lhs(h), k_t, _NT, preferred_element_type=F32)

    q_head = lambda h: q[:, h * hd:(h + 1) * hd]
    s = lax.dot_general(qs, kc_ref[...], _NT, preferred_element_type=F32)
    st = lax.dot_general(kc_ref[...], qs, _NT, preferred_element_type=F32)
    put_scores(sw_sc, 0, kw_ref, t0, q_head)

    gt = jax.nn.sigmoid(pltpu.roll(zg_ref[...].astype(F32), (LANES - GATES_PER_GROUP * grp) & (LANES - 1), axis=1))

    def gate(h, branch):
        c = 3 * h + branch
        return gt[:, c:c + 1]

    cmp_end = lax.broadcasted_iota(jnp.int32, (1, n_cmp), 1) * CMP_STRIDE + (CMP_LEN - 1)
    s = jnp.where(cmp_end <= qpos, s, -jnp.inf)
    mx = jnp.max(s, axis=-1, keepdims=True)
    mx = jnp.where(jnp.isfinite(mx), mx, 0.0)
    pe_cmp = jnp.exp(s - mx).astype(BF16)

    end_t = lax.broadcasted_iota(jnp.int32, (n_cmp, 1), 0) * CMP_STRIDE + (CMP_LEN - 1)
    qpos_t = t0 + (lax.broadcasted_iota(jnp.int32, (1, hpg * tq), 1) & (tq - 1))
    st = jnp.where(end_t <= qpos_t, st, -jnp.inf)
    mt = jnp.max(st, axis=0, keepdims=True)
    mt = jnp.where(jnp.isfinite(mt), mt, 0.0)
    pt = jnp.exp(st - mt)
    pt = pt / jnp.maximum(jnp.sum(pt, axis=0, keepdims=True), 1e-30)
    p_heads = jnp.concatenate([pt[:, h * tq:(h + 1) * tq] for h in range(hpg)], axis=0).astype(BF16)
    imp = _dot(ov_sc[...], p_heads)
    acc = _dot(pe_cmp, vcaug_sc[...])
    put_scores(sw_sc, 1, kw_ref, jnp.maximum(t0 - tq, 0), q_head)

    blk = lax.broadcasted_iota(jnp.int32, (n_blk, tq), 0)
    cur = lax.shift_right_logical(t0 + lax.broadcasted_iota(jnp.int32, (n_blk, tq), 1), 6)
    valid = blk <= cur
    forced = (blk == 0) | (valid & (blk > cur - SLC_LOCAL))
    score = jnp.where(forced, imp + FORCE_BONUS, jnp.where(valid, imp, -1.0))
    key = lax.bitcast_convert_type(score, jnp.int32)
    rank = jnp.zeros((n_blk, tq), jnp.int32)
    for r in range(1, n_blk):
        other = pltpu.roll(key, r, axis=0)
        rank = rank + jnp.where(other + jnp.where(blk >= r, 1, 0) > key, 1, 0)
    sel_bias = jnp.where((rank < SLC_TOP_N) & valid, 0.0, MASK_VALUE)
    sel_rows = jnp.concatenate([sel_bias, jnp.zeros((LANES - n_blk, tq), F32)], axis=0).T.astype(BF16)
    for h in range(hpg):
        qa_sc[h] = jnp.concatenate([q_head(h), sel_rows], axis=1)
    put_scores(s_sc, 0, kaug_sc, 0, lambda h: qa_sc[h])

    o_cmp = acc[:, :hd] / jnp.maximum(acc[:, hd:], 1e-30)
    for h in range(hpg):
        out_sc[:, h * hd:(h + 1) * hd] = gate(h, 0) * o_cmp[h * tq:(h + 1) * tq]

    def reset():
        m_sc[...] = jnp.full(m_sc.shape, MASK_VALUE, F32)
        acc_sc[...] = jnp.zeros(acc_sc.shape, F32)

    def fold(src, slot, v_ref, start, mask):
        v_t = v_ref[pl.ds(pl.multiple_of(start, tq), tq), :]
        bias = None if mask is None else bias_sc[mask]
        for h in range(hpg):
            sc = src[slot, h]
            if bias is not None:
                sc = sc + bias
            m_prev = m_sc[h]
            m_new = jnp.maximum(m_prev, jnp.max(sc, axis=-1, keepdims=True))
            alpha = jnp.exp(m_prev - m_new)
            pe = jnp.exp(sc - jnp.concatenate([m_new] * (tq // LANES), axis=1))
            acc_sc[h] = jnp.concatenate([alpha, alpha], axis=1) * acc_sc[h] + _dot(pe.astype(BF16), v_t)
            m_sc[h] = m_new

    def finish(h):
        acc = acc_sc[h]
        return acc[:, :hd] / acc[:, hd:]

    reset()

    def slc_pair(j, carry):
        e = 2 * j * tq
        put_scores(s_sc, 1, kaug_sc, e + tq, lambda h: qa_sc[h])
        fold(s_sc, 0, vsaug_sc, e, None)
        put_scores(s_sc, 0, kaug_sc, e + 2 * tq, lambda h: qa_sc[h])
        fold(s_sc, 1, vsaug_sc, e + tq, None)
        return carry

    lax.fori_loop(0, qi // 2, slc_pair, 0)

    @pl.when(qi % 2 == 1)
    def _():
        put_scores(s_sc, 1, kaug_sc, t0, lambda h: qa_sc[h])
        fold(s_sc, 0, vsaug_sc, t0 - tq, None)
        fold(s_sc, 1, vsaug_sc, t0, 0)

    @pl.when(qi % 2 == 0)
    def _():
        fold(s_sc, 0, vsaug_sc, t0, 0)

    for h in range(hpg):
        out_sc[:, h * hd:(h + 1) * hd] += gate(h, 1) * finish(h)

    assert WINDOW == tq
    reset()

    @pl.when(qi >= 1)
    def _():
        fold(sw_sc, 1, vwaug_sc, t0 - tq, 1)

    fold(sw_sc, 0, vwaug_sc, t0, 0)
    for h in range(hpg):
        o_h = out_sc[:, h * hd:(h + 1) * hd] + gate(h, 2) * finish(h)
        o_ref[:, h * hd:(h + 1) * hd] = o_h.astype(o_ref.dtype)


def _nsa(z, kv_cmp, gates_col, batch, seq, tq=WINDOW):
    m = z.shape[0]
    assert gates_col % LANES == 0
    g = NSA_KV_GROUPS
    hd = HEAD_DIM
    hpg = HEADS_PER_GROUP
    n_cmp = kv_cmp.shape[3]
    nq = seq // tq
    qcols = NSA_WIDTH // hd
    assert (tq // 4) % SLC_LEN == 0 and n_cmp == LANES

    def kv_spec(slot):
        return pl.BlockSpec((seq, hd), lambda b, gi, i: (b, qcols + slot * g + gi))

    def cmp_spec(slot):
        return pl.BlockSpec((None, None, None, n_cmp, hd), lambda b, gi, i: (b, slot, gi, 0, 0))

    return pl.pallas_call(
        functools.partial(_nsa_kernel, tq=tq),
        out_shape=jax.ShapeDtypeStruct((m, NSA_WIDTH), BF16),
        grid=(batch, g, nq),
        in_specs=[
            pl.BlockSpec((tq, hpg * hd), lambda b, gi, i: (b * nq + i, gi)),
            kv_spec(2), kv_spec(3), kv_spec(4), kv_spec(5),
            cmp_spec(0), cmp_spec(1),
            pl.BlockSpec((tq, LANES), lambda b, gi, i: (b * nq + i, gates_col // LANES)),
        ],
        out_specs=pl.BlockSpec((tq, hpg * hd), lambda b, gi, i: (b * nq + i, gi)),
        scratch_shapes=[
            pltpu.VMEM((seq, 2 * hd), BF16),
            pltpu.VMEM((seq, 2 * hd), BF16),
            pltpu.VMEM((seq, 2 * hd), BF16),
            pltpu.VMEM((n_cmp, 2 * hd), BF16),
            pltpu.VMEM((LANES // 4, hpg * n_cmp), BF16),
            pltpu.VMEM((2, tq, tq), F32),
            pltpu.VMEM((hpg, tq, 2 * hd), BF16),
            pltpu.VMEM((2, hpg, tq, tq), F32),
            pltpu.VMEM((2, hpg, tq, tq), F32),
            pltpu.VMEM((hpg, tq, LANES), F32),
            pltpu.VMEM((hpg, tq, 2 * hd), F32),
            pltpu.VMEM((tq, hpg * hd), F32),
        ],
        compiler_params=_params("parallel", "parallel", "arbitrary"),
        name="nsa",
    )(z, z, z, z, z, kv_cmp, kv_cmp, z)


def _lru_kernel(lx_ref, ly_ref, cw_ref, cb_ref, wa_ref, ba_ref, wi_ref, bi_ref, lam_ref, o_ref, a_sc, h_sc):
    seq, wt = lx_ref.shape
    x = lx_ref[...].astype(F32)
    row = lax.broadcasted_iota(jnp.int32, (seq, 1), 0)
    taps = cw_ref.shape[0]
    u = cb_ref[...]
    for j in range(taps):
        back = taps - 1 - j
        xs = x if back == 0 else jnp.where(row >= back, pltpu.roll(x, back, axis=0), 0.0)
        u = u + xs * cw_ref[j:j + 1, :]
    ub = u.astype(BF16)
    r = jax.nn.sigmoid(_dot(ub, wa_ref[0]) + ba_ref[...])
    gate_i = jax.nn.sigmoid(_dot(ub, wi_ref[0]) + bi_ref[...])
    log_a = (LRU_C * r) * jax.nn.log_sigmoid(lam_ref[...])
    a = jnp.exp(log_a)
    a_sc[...] = a
    h_sc[...] = jnp.sqrt(1.0 - a * a) * (gate_i * u)

    sub = lax.broadcasted_iota(jnp.int32, (SUBLANES, wt), 0)

    def chunk(k, h):
        r0 = pl.multiple_of(k * SUBLANES, SUBLANES)
        a = a_sc[pl.ds(r0, SUBLANES), :]
        b = h_sc[pl.ds(r0, SUBLANES), :]
        for s in (1, 2, 4):
            a_prev = jnp.where(sub >= s, pltpu.roll(a, s, axis=0), 1.0)
            b_prev = jnp.where(sub >= s, pltpu.roll(b, s, axis=0), 0.0)
            b = a * b_prev + b
            a = a * a_prev
        hs = a * h + b
        h_sc[pl.ds(r0, SUBLANES), :] = hs
        return hs[SUBLANES - 1:SUBLANES, :]

    lax.fori_loop(0, seq // SUBLANES, chunk, jnp.zeros((1, wt), F32), unroll=4)
    o_ref[...] = (jax.nn.gelu(ly_ref[...].astype(F32)) * h_sc[...]).astype(o_ref.dtype)


def _lru(z, lx_col, ly_col, conv_w, conv_b, wa, ba, wi, bi, lam, batch, seq):
    m = z.shape[0]
    wt = LRU_BLOCK_DIM
    nw = LRU_WIDTH // wt
    assert lx_col % wt == 0 and ly_col % wt == 0
    taps = conv_w.shape[0]
    vec = pl.BlockSpec((1, wt), lambda b, j: (0, j))
    blk = pl.BlockSpec((1, wt, wt), lambda b, j: (j, 0, 0))
    return pl.pallas_call(
        _lru_kernel,
        out_shape=jax.ShapeDtypeStruct((m, LRU_WIDTH), BF16),
        grid=(batch, nw),
        in_specs=[
            pl.BlockSpec((seq, wt), lambda b, j: (b, lx_col // wt + j)),
            pl.BlockSpec((seq, wt), lambda b, j: (b, ly_col // wt + j)),
            pl.BlockSpec((taps, wt), lambda b, j: (0, j)),
            vec, blk, vec, blk, vec, vec,
        ],
        out_specs=pl.BlockSpec((seq, wt), lambda b, j: (b, j)),
        scratch_shapes=[pltpu.VMEM((seq, wt), F32), pltpu.VMEM((seq, wt), F32)],
        compiler_params=_params("parallel", "parallel"),
        name="lru",
    )(z, z, conv_w, conv_b, wa, ba, wi, bi, lam)


def _merge_kernel(oa_ref, ob_ref, pa_ref, pb_ref, ga_ref, gb_ref, o_ref):
    a = _dot(oa_ref[...], pa_ref[...])
    b = _dot(ob_ref[...], pb_ref[...])
    merged = jax.nn.sigmoid(ga_ref[...].astype(F32)) * a + jax.nn.sigmoid(gb_ref[...].astype(F32)) * b
    o_ref[...] = merged.astype(o_ref.dtype)


def _merge(o_a, o_b, proj_a, proj_b, layer, z, ga_col, gb_col, d_model, tm_target=1024, tn_target=512):
    m, ka = o_a.shape
    kb = o_b.shape[1]
    tm = _tile(m, tm_target, BF16_ROWS)
    tn = _tile(d_model, tn_target)
    ga0 = ga_col // tn
    gb0 = gb_col // tn
    assert ga_col % tn == 0 and gb_col % tn == 0
    return pl.pallas_call(
        _merge_kernel,
        out_shape=jax.ShapeDtypeStruct((m, d_model), BF16),
        grid=(m // tm, d_model // tn),
        in_specs=[
            pl.BlockSpec((tm, ka), lambda i, j: (i, 0)),
            pl.BlockSpec((tm, kb), lambda i, j: (i, 0)),
            pl.BlockSpec((None, ka, tn), lambda i, j: (layer, 0, j)),
            pl.BlockSpec((None, kb, tn), lambda i, j: (layer, 0, j)),
            pl.BlockSpec((tm, tn), lambda i, j: (i, ga0 + j)),
            pl.BlockSpec((tm, tn), lambda i, j: (i, gb0 + j)),
        ],
        out_specs=pl.BlockSpec((tm, tn), lambda i, j: (i, j)),
        compiler_params=_params("parallel", "parallel"),
        name="merge",
    )(o_a, o_b, proj_a, proj_b, z, z)


def _matmul_residual_kernel(a_ref, w_ref, x_ref, g_ref, o_ref):
    o_ref[...] = x_ref[...] + g_ref[...] * _dot(a_ref[...], w_ref[...])


def _matmul_residual(a, w, layer, x, gate, seq, name, tm_target=1024, tn_target=512):
    m, k = a.shape
    n = w.shape[-1]
    tm = _tile(seq, tm_target, BF16_ROWS)
    tn = _tile(n, tn_target)
    return pl.pallas_call(
        _matmul_residual_kernel,
        out_shape=jax.ShapeDtypeStruct((m, n), F32),
        grid=(m // tm, n // tn),
        in_specs=[
            pl.BlockSpec((tm, k), lambda i, j: (i, 0)),
            pl.BlockSpec((None, k, tn), lambda i, j: (layer, 0, j)),
            pl.BlockSpec((tm, tn), lambda i, j: (i, j)),
            pl.BlockSpec((None, 1, tn), lambda i, j: ((i * tm) // seq, 0, j)),
        ],
        out_specs=pl.BlockSpec((tm, tn), lambda i, j: (i, j)),
        compiler_params=_params("parallel", "parallel"),
        name=name,
    )(a, w, x, gate)


def _ffn_up_kernel(x_ref, xh_ref, g_ref, sc_ref, sh_ref, wg_ref, wv_ref, cw_ref, cb_ref, o_ref, h_ref, *, seq):
    halo = BF16_ROWS
    tm = o_ref.shape[0]

    @pl.when(pl.program_id(1) == 0)
    def _():
        h_ref[0:halo, :] = _norm_mod(xh_ref[...], g_ref[...], sc_ref[...], sh_ref[...]).astype(BF16)
        h_ref[halo:, :] = _norm_mod(x_ref[...], g_ref[...], sc_ref[...], sh_ref[...]).astype(BF16)

    h = h_ref[...]
    gate = _dot(h, wg_ref[...])
    val = _dot(h[halo:], wv_ref[...])
    seq_start = (pl.program_id(0) * tm) % seq == 0
    row = lax.broadcasted_iota(jnp.int32, (tm, 1), 0)
    taps = cw_ref.shape[0]
    conv = cb_ref[...]
    for j in range(taps):
        back = taps - 1 - j
        gj = gate[halo - back:halo - back + tm]
        if back:
            gj = jnp.where(row >= jnp.where(seq_start, back, 0), gj, 0.0)
        conv = conv + gj * cw_ref[j:j + 1, :]
    o_ref[...] = (jax.nn.gelu(conv) * val).astype(o_ref.dtype)


def _ffn_up(x, g, sc, sh, w_up, layer, conv_w, conv_b, seq, tm_target=1024, tn_target=768):
    m, d = x.shape
    d_ff = w_up.shape[-1] // 2
    tm = _tile(seq, tm_target, BF16_ROWS)
    tn = _tile(d_ff, tn_target)
    nj = d_ff // tn
    halo = BF16_ROWS
    taps = conv_w.shape[0]
    bidx = lambda i, j: ((i * tm) // seq, 0, 0)
    return pl.pallas_call(
        functools.partial(_ffn_up_kernel, seq=seq),
        out_shape=jax.ShapeDtypeStruct((m, d_ff), BF16),
        grid=(m // tm, nj),
        in_specs=[
            pl.BlockSpec((tm, d), lambda i, j: (i, 0)),
            pl.BlockSpec((halo, d), lambda i, j: (jnp.maximum(i * (tm // halo) - 1, 0), 0)),
            pl.BlockSpec((1, d), lambda i, j: (0, 0)),
            pl.BlockSpec((None, 1, d), bidx),
            pl.BlockSpec((None, 1, d), bidx),
            pl.BlockSpec((None, d, tn), lambda i, j: (layer, 0, j)),
            pl.BlockSpec((None, d, tn), lambda i, j: (layer, 0, nj + j)),
            pl.BlockSpec((taps, tn), lambda i, j: (0, j)),
            pl.BlockSpec((1, tn), lambda i, j: (0, j)),
        ],
        out_specs=pl.BlockSpec((tm, tn), lambda i, j: (i, j)),
        scratch_shapes=[pltpu.VMEM((halo + tm, d), BF16)],
        compiler_params=_params("parallel", "arbitrary"),
        name="ffn_up",
    )(x, x, g, sc, sh, w_up, w_up, conv_w, conv_b)


def _final_norm_kernel(x_ref, g_ref, o_ref):
    x = x_ref[...]
    inv = lax.rsqrt(jnp.mean(x * x, axis=-1, keepdims=True) + NORM_EPS)
    o_ref[...] = (x * inv) * g_ref[...]


def _final_norm(x, g, tm_target=512):
    m, d = x.shape
    tm = _tile(m, tm_target, SUBLANES)
    return pl.pallas_call(
        _final_norm_kernel,
        out_shape=jax.ShapeDtypeStruct((m, d), F32),
        grid=(m // tm,),
        in_specs=[pl.BlockSpec((tm, d), lambda i: (i, 0)), pl.BlockSpec((1, d), lambda i: (0, 0))],
        out_specs=pl.BlockSpec((tm, d), lambda i: (i, 0)),
        compiler_params=_params("parallel"),
        name="final_norm",
    )(x, g)


def _compress_weights(pos, w1, w2):
    half = CMP_LEN // 2
    return (pos.reshape(2, half * HEAD_DIM),
            w1.reshape(2, half * HEAD_DIM, HEAD_DIM).astype(BF16),
            w2.astype(BF16))


def kernel(x, c, ada_w, ada_b, norm1_g, w_in, cmp_pos_k, cmp_w1_k, cmp_w2_k, cmp_pos_v, cmp_w1_v, cmp_w2_v,
           lru_conv_w, lru_conv_b, lru_wa, lru_ba, lru_wi, lru_bi, lru_lambda, proj_a, proj_b, w_out, norm2_g,
           ffn_up, ffn_conv_w, ffn_conv_b, ffn_down, final_g):
    batch, seq, d = x.shape
    depth = ada_w.shape[0]
    m = batch * seq
    assert seq % SLC_LEN == 0 and seq // SLC_LEN == LANES // 4 and CMP_STRIDE * LANES == seq
    n_attn = NSA_WIDTH + 6 * KV_WIDTH
    n_rest = 2 * LRU_WIDTH + 2 * d
    n_z = -(-(n_attn + n_rest + N_GATES) // (4 * LANES)) * (4 * LANES)
    z_cols = dict(lx=n_attn, ly=n_attn + LRU_WIDTH, ga=n_attn + 2 * LRU_WIDTH, gb=n_attn + 2 * LRU_WIDTH + d,
                  gates=n_attn + n_rest)

    mod = _ada(c, ada_w, ada_b)
    xf = x.reshape(m, d)
    q_scale = jnp.concatenate([jnp.full((1, NSA_WIDTH), HEAD_DIM ** -0.5, F32), jnp.ones((1, n_z - NSA_WIDTH), F32)],
                              axis=1)
    w_all = jnp.concatenate([w_in[:, :, :n_attn], w_in[:, :, n_attn + N_GATES:],
                             jnp.pad(w_in[:, :, n_attn:n_attn + N_GATES],
                                     ((0, 0), (0, 0), (0, n_z - n_attn - n_rest - N_GATES)))], axis=2).astype(BF16)
    proj_a, proj_b, w_out, ffn_up, ffn_down, lru_wa, lru_wi = [
        w.astype(BF16) for w in (proj_a, proj_b, w_out, ffn_up, ffn_down, lru_wa, lru_wi)]

    for l in range(depth):
        sh1, sc1, g1, sh2, sc2, g2 = [mod[l, :, k * d:(k + 1) * d].reshape(batch, 1, d) for k in range(6)]
        z = _in_proj(xf, norm1_g[l].reshape(1, d), sc1, sh1, w_all, l, q_scale, BF16, seq)

        pk, w1k, w2k = _compress_weights(cmp_pos_k[l], cmp_w1_k[l], cmp_w2_k[l])
        pv, w1v, w2v = _compress_weights(cmp_pos_v[l], cmp_w1_v[l], cmp_w2_v[l])
        kv_cmp = _compress(z, batch, seq, jnp.stack([pk, pv]), jnp.stack([w1k, w1v]), jnp.stack([w2k, w2v]))

        o_a = _nsa(z, kv_cmp, z_cols["gates"], batch, seq)
        o_b = _lru(z, z_cols["lx"], z_cols["ly"], lru_conv_w[l], lru_conv_b[l].reshape(1, -1), lru_wa[l],
                   lru_ba[l].reshape(1, -1), lru_wi[l], lru_bi[l].reshape(1, -1),
                   lru_lambda[l].reshape(1, -1), batch, seq)
        merged = _merge(o_a, o_b, proj_a, proj_b, l, z, z_cols["ga"], z_cols["gb"], d)
        xf = _matmul_residual(merged, w_out, l, xf, g1, seq, "out_proj", tn_target=512)

        act = _ffn_up(xf, norm2_g[l].reshape(1, d), sc2, sh2, ffn_up, l, ffn_conv_w[l],
                      ffn_conv_b[l].reshape(1, -1), seq)
        xf = _matmul_residual(act, ffn_down, l, xf, g2, seq, "ffn_down", tn_target=256)

    return _final_norm(xf, final_g.reshape(1, d)).reshape(batch, seq, d)
```

```python
import functools

import jax
import jax.numpy as jnp
from jax import lax
from jax.experimental import pallas as pl
from jax.experimental.pallas import tpu as pltpu

F32 = jnp.float32
BF16 = jnp.bfloat16

NSA_HEADS = 16
NSA_KV_GROUPS = 4
HEADS_PER_GROUP = NSA_HEADS // NSA_KV_GROUPS
HEAD_DIM = 128
NSA_WIDTH = NSA_HEADS * HEAD_DIM
KV_WIDTH = NSA_KV_GROUPS * HEAD_DIM
CMP_LEN = 32
CMP_STRIDE = 16
SLC_LEN = 64
SLC_TOP_N = 16
SLC_LOCAL = 2
FORCE_BONUS = 1e4
WINDOW = 512
LRU_WIDTH = 2048
LRU_BLOCK_DIM = 128
LRU_C = 8.0
NORM_EPS = 1e-6
N_GATES = 3 * NSA_HEADS
GATES_PER_GROUP = 3 * HEADS_PER_GROUP

LANES = 128
SUBLANES = 8
BF16_ROWS = 16
VMEM_LIMIT_BYTES = 56 * 1024 * 1024

MASK_VALUE = -1e30
LOG2_E = 1.4426950408889634

_NT = (((1,), (1,)), ((), ()))


def _params(*sem):
    return pltpu.CompilerParams(dimension_semantics=sem, vmem_limit_bytes=VMEM_LIMIT_BYTES)


def _tile(n, target, quantum=LANES):
    best = None
    t = quantum
    while t <= min(n, target):
        if n % t == 0:
            best = t
        t += quantum
    assert best is not None, (n, target, quantum)
    return best


def _dot(a, b):
    return jnp.dot(a, b, preferred_element_type=F32)


def _norm_mod(x, g, sc, sh):
    inv = lax.rsqrt(jnp.mean(x * x, axis=-1, keepdims=True) + NORM_EPS)
    return ((x * inv) * g) * (1.0 + sc) + sh


def _ada_kernel(c_ref, w_ref, b_ref, o_ref):
    c = c_ref[...]
    act = (c * jax.nn.sigmoid(c)).astype(BF16)
    o_ref[...] = _dot(act, w_ref[...].astype(BF16)) + b_ref[...]


def _ada(c, ada_w, ada_b):
    depth, d, n = ada_w.shape
    b = c.shape[0]
    tn = _tile(n, 1024)
    return pl.pallas_call(
        _ada_kernel,
        out_shape=jax.ShapeDtypeStruct((depth, b, n), F32),
        grid=(depth, n // tn),
        in_specs=[
            pl.BlockSpec((b, d), lambda l, j: (0, 0)),
            pl.BlockSpec((None, d, tn), lambda l, j: (l, 0, j)),
            pl.BlockSpec((None, 1, tn), lambda l, j: (l, 0, j)),
        ],
        out_specs=pl.BlockSpec((None, b, tn), lambda l, j: (l, 0, j)),
        compiler_params=_params("parallel", "parallel"),
        name="ada",
    )(c, ada_w, ada_b.reshape(depth, 1, n))


def _in_proj_kernel(x_ref, g_ref, sc_ref, sh_ref, w_ref, cs_ref, o_ref, h_ref):
    @pl.when(pl.program_id(1) == 0)
    def _():
        h_ref[...] = _norm_mod(x_ref[...], g_ref[...], sc_ref[...], sh_ref[...]).astype(BF16)

    o_ref[...] = (_dot(h_ref[...], w_ref[...]) * cs_ref[...]).astype(o_ref.dtype)


def _in_proj(x, g, sc, sh, w, layer, colscale, out_dtype, seq, tm_target=1024, tn_target=1536):
    m, d = x.shape
    n = w.shape[-1]
    tm = _tile(seq, tm_target, BF16_ROWS)
    tn = _tile(n, tn_target)
    bidx = lambda i, j: ((i * tm) // seq, 0, 0)
    return pl.pallas_call(
        _in_proj_kernel,
        out_shape=jax.ShapeDtypeStruct((m, n), out_dtype),
        grid=(m // tm, n // tn),
        in_specs=[
            pl.BlockSpec((tm, d), lambda i, j: (i, 0)),
            pl.BlockSpec((1, d), lambda i, j: (0, 0)),
            pl.BlockSpec((None, 1, d), bidx),
            pl.BlockSpec((None, 1, d), bidx),
            pl.BlockSpec((None, d, tn), lambda i, j: (layer, 0, j)),
            pl.BlockSpec((1, tn), lambda i, j: (0, j)),
        ],
        out_specs=pl.BlockSpec((tm, tn), lambda i, j: (i, j)),
        scratch_shapes=[pltpu.VMEM((tm, d), BF16)],
        compiler_params=_params("parallel", "arbitrary"),
        name="in_proj",
    )(x, g, sc, sh, w, colscale)


def _compress_kernel(kv_ref, pos_ref, w1_ref, w2_ref, o_ref, tok_sc, c_sc):
    hd = kv_ref.shape[1]
    n = c_sc.shape[0]
    tok_sc[...] = kv_ref[...].astype(F32)
    for j in range(CMP_STRIDE):
        c_sc[:, j * hd:(j + 1) * hd] = tok_sc[pl.ds(j, n, stride=CMP_STRIDE), :]
    c = c_sc[...]
    lo = (c + pos_ref[0:1, :]).astype(BF16)
    hi = (c + pos_ref[1:2, :]).astype(BF16)
    first = _dot(lo, w1_ref[0])
    second = _dot(hi, w1_ref[1])
    hid = first + pltpu.roll(second, n - 1, axis=0)
    o_ref[...] = _dot(jax.nn.gelu(hid).astype(BF16), w2_ref[...]).astype(o_ref.dtype)


def _compress(z, batch, seq, pos, w1, w2):
    g = NSA_KV_GROUPS
    hd = HEAD_DIM
    n = seq // CMP_STRIDE
    cw = CMP_STRIDE * hd
    qcols = NSA_WIDTH // hd
    return pl.pallas_call(
        _compress_kernel,
        out_shape=jax.ShapeDtypeStruct((batch, 2, g, n, hd), BF16),
        grid=(batch, 2, g),
        in_specs=[
            pl.BlockSpec((seq, hd), lambda bi, s, gi: (bi, qcols + s * g + gi)),
            pl.BlockSpec((None, 2, cw), lambda bi, s, gi: (s, 0, 0)),
            pl.BlockSpec((None, 2, cw, hd), lambda bi, s, gi: (s, 0, 0, 0)),
            pl.BlockSpec((None, hd, hd), lambda bi, s, gi: (s, 0, 0)),
        ],
        out_specs=pl.BlockSpec((None, None, None, n, hd), lambda bi, s, gi: (bi, s, gi, 0, 0)),
        scratch_shapes=[pltpu.VMEM((seq, hd), F32), pltpu.VMEM((n, cw), F32)],
        compiler_params=_params("parallel", "parallel", "parallel"),
        name="compress",
    )(z, pos, w1, w2)


def _nsa_kernel(q_ref, ks_ref, vs_ref, kw_ref, vw_ref, kc_ref, vc_ref, zg_ref, o_ref,
                kaug_sc, vsaug_sc, vwaug_sc, vcaug_sc, ov_sc, bias_sc, qa_sc, s_sc, sw_sc, m_sc, acc_sc, out_sc,
                *, tq):
    hd = HEAD_DIM
    hpg = HEADS_PER_GROUP
    tk = tq
    n_blk = LANES // 4
    seq = ks_ref.shape[0]
    n_cmp = kc_ref.shape[0]
    grp = pl.program_id(1)
    qi = pl.program_id(2)
    t0 = qi * tq

    @pl.when(qi == 0)
    def _():
        kaug_sc[:, :hd] = ks_ref[...]
        key_blk = lax.shift_right_logical(lax.broadcasted_iota(jnp.int32, (seq, LANES), 0), 6)
        lane_blk = lax.broadcasted_iota(jnp.int32, (seq, LANES), 1) & (n_blk - 1)
        kaug_sc[:, hd:] = jnp.where(key_blk == lane_blk, 1.0, 0.0).astype(BF16)
        ones = jnp.ones((seq, LANES), BF16)
        vsaug_sc[:, :hd] = vs_ref[...]
        vsaug_sc[:, hd:] = ones
        vwaug_sc[:, :hd] = vw_ref[...]
        vwaug_sc[:, hd:] = ones
        vcaug_sc[:, :hd] = vc_ref[...]
        vcaug_sc[:, hd:] = jnp.ones((n_cmp, LANES), BF16)
        cstart = (lax.broadcasted_iota(jnp.int32, (n_blk, hpg * n_cmp), 1) & (n_cmp - 1)) * CMP_STRIDE
        bstart = lax.broadcasted_iota(jnp.int32, (n_blk, hpg * n_cmp), 0) * SLC_LEN
        ov_sc[...] = jnp.where((cstart < bstart + SLC_LEN) & (cstart + CMP_LEN > bstart), 1.0, 0.0).astype(BF16)
        r = lax.broadcasted_iota(jnp.int32, (tq, tk), 0)
        cidx = lax.broadcasted_iota(jnp.int32, (tq, tk), 1)
        bias_sc[0] = jnp.where(cidx <= r, 0.0, MASK_VALUE)
        bias_sc[1] = jnp.where(cidx > r, 0.0, MASK_VALUE)

    q = q_ref[...]
    qs = jnp.concatenate([q[:, h * hd:(h + 1) * hd] for h in range(hpg)], axis=0)
    qpos = t0 + (lax.broadcasted_iota(jnp.int32, (hpg * tq, 1), 0) & (tq - 1))

    def put_scores(dst, slot, k_ref, start, lhs):
        k_t = k_ref[pl.ds(pl.multiple_of(start, tq), tq), :]
        for h in range(hpg):
            dst[slot, h] = lax.dot_general(lhs(h), k_t, _NT, preferred_element_type=F32)

    q_head = lambda h: q[:, h * hd:(h + 1) * hd]
    s = lax.dot_general(qs, kc_ref[...], _NT, preferred_element_type=F32)
    st = lax.dot_general(kc_ref[...], qs, _NT, preferred_element_type=F32)
    put_scores(sw_sc, 0, kw_ref, t0, q_head)

    gt = jax.nn.sigmoid(pltpu.roll(zg_ref[...].astype(F32), (LANES - GATES_PER_GROUP * grp) & (LANES - 1), axis=1))

    def gate(h, branch):
        c = 3 * h + branch
        return gt[:, c:c + 1]

    cmp_end = lax.broadcasted_iota(jnp.int32, (1, n_cmp), 1) * CMP_STRIDE + (CMP_LEN - 1)
    s = jnp.where(cmp_end <= qpos, s, -jnp.inf)
    mx = jnp.max(s, axis=-1, keepdims=True)
    mx = jnp.where(jnp.isfinite(mx), mx, 0.0)
    pe_cmp = jnp.exp2(s - mx).astype(BF16)

    end_t = lax.broadcasted_iota(jnp.int32, (n_cmp, 1), 0) * CMP_STRIDE + (CMP_LEN - 1)
    qpos_t = t0 + (lax.broadcasted_iota(jnp.int32, (1, hpg * tq), 1) & (tq - 1))
    st = jnp.where(end_t <= qpos_t, st, -jnp.inf)
    mt = jnp.max(st, axis=0, keepdims=True)
    mt = jnp.where(jnp.isfinite(mt), mt, 0.0)
    pt = jnp.exp2(st - mt)
    pt = pt / jnp.maximum(jnp.sum(pt, axis=0, keepdims=True), 1e-30)
    p_heads = jnp.concatenate([pt[:, h * tq:(h + 1) * tq] for h in range(hpg)], axis=0).astype(BF16)
    imp = _dot(ov_sc[...], p_heads)
    acc = _dot(pe_cmp, vcaug_sc[...])
    put_scores(sw_sc, 1, kw_ref, jnp.maximum(t0 - tq, 0), q_head)

    blk = lax.broadcasted_iota(jnp.int32, (n_blk, tq), 0)
    cur = lax.shift_right_logical(t0 + lax.broadcasted_iota(jnp.int32, (n_blk, tq), 1), 6)
    valid = blk <= cur
    forced = (blk == 0) | (valid & (blk > cur - SLC_LOCAL))
    score = jnp.where(forced, imp + FORCE_BONUS, jnp.where(valid, imp, -1.0))
    key = lax.bitcast_convert_type(score, jnp.int32)
    rank = jnp.zeros((n_blk, tq), jnp.int32)
    for r in range(1, n_blk):
        other = pltpu.roll(key, r, axis=0)
        rank = rank + jnp.where(other + jnp.where(blk >= r, 1, 0) > key, 1, 0)
    sel_bias = jnp.where((rank < SLC_TOP_N) & valid, 0.0, MASK_VALUE)
    sel_rows = jnp.concatenate([sel_bias, jnp.zeros((LANES - n_blk, tq), F32)], axis=0).T.astype(BF16)
    for h in range(hpg):
        qa_sc[h] = jnp.concatenate([q_head(h), sel_rows], axis=1)
    put_scores(s_sc, 0, kaug_sc, 0, lambda h: qa_sc[h])

    o_cmp = acc[:, :hd] / jnp.maximum(acc[:, hd:], 1e-30)
    for h in range(hpg):
        out_sc[:, h * hd:(h + 1) * hd] = gate(h, 0) * o_cmp[h * tq:(h + 1) * tq]

    def reset():
        m_sc[...] = jnp.full(m_sc.shape, MASK_VALUE, F32)
        acc_sc[...] = jnp.zeros(acc_sc.shape, F32)

    def fold(src, slot, v_ref, start, mask):
        v_t = v_ref[pl.ds(pl.multiple_of(start, tq), tq), :]
        bias = None if mask is None else bias_sc[mask]
        for h in range(hpg):
            sc = src[slot, h]
            if bias is not None:
                sc = sc + bias
            m_prev = m_sc[h]
            m_new = jnp.maximum(m_prev, jnp.max(sc, axis=-1, keepdims=True))
            alpha = jnp.exp2(m_prev - m_new)
            pe = jnp.exp2(sc - jnp.concatenate([m_new] * (tq // LANES), axis=1))
            acc_sc[h] = jnp.concatenate([alpha, alpha], axis=1) * acc_sc[h] + _dot(pe.astype(BF16), v_t)
            m_sc[h] = m_new

    def finish(h):
        acc = acc_sc[h]
        return acc[:, :hd] / acc[:, hd:]

    reset()

    def slc_pair(j, carry):
        e = 2 * j * tq
        put_scores(s_sc, 1, kaug_sc, e + tq, lambda h: qa_sc[h])
        fold(s_sc, 0, vsaug_sc, e, None)
        put_scores(s_sc, 0, kaug_sc, e + 2 * tq, lambda h: qa_sc[h])
        fold(s_sc, 1, vsaug_sc, e + tq, None)
        return carry

    lax.fori_loop(0, qi // 2, slc_pair, 0)

    @pl.when(qi % 2 == 1)
    def _():
        put_scores(s_sc, 1, kaug_sc, t0, lambda h: qa_sc[h])
        fold(s_sc, 0, vsaug_sc, t0 - tq, None)
        fold(s_sc, 1, vsaug_sc, t0, 0)

    @pl.when(qi % 2 == 0)
    def _():
        fold(s_sc, 0, vsaug_sc, t0, 0)

    for h in range(hpg):
        out_sc[:, h * hd:(h + 1) * hd] += gate(h, 1) * finish(h)

    assert WINDOW == tq
    reset()

    @pl.when(qi >= 1)
    def _():
        fold(sw_sc, 1, vwaug_sc, t0 - tq, 1)

    fold(sw_sc, 0, vwaug_sc, t0, 0)
    for h in range(hpg):
        o_h = out_sc[:, h * hd:(h + 1) * hd] + gate(h, 2) * finish(h)
        o_ref[:, h * hd:(h + 1) * hd] = o_h.astype(o_ref.dtype)


def _nsa(z, kv_cmp, gates_col, batch, seq, tq=WINDOW):
    m = z.shape[0]
    assert gates_col % LANES == 0
    g = NSA_KV_GROUPS
    hd = HEAD_DIM
    hpg = HEADS_PER_GROUP
    n_cmp = kv_cmp.shape[3]
    nq = seq // tq
    qcols = NSA_WIDTH // hd
    assert (tq // 4) % SLC_LEN == 0 and n_cmp == LANES

    def kv_spec(slot):
        return pl.BlockSpec((seq, hd), lambda b, gi, i: (b, qcols + slot * g + gi))

    def cmp_spec(slot):
        return pl.BlockSpec((None, None, None, n_cmp, hd), lambda b, gi, i: (b, slot, gi, 0, 0))

    return pl.pallas_call(
        functools.partial(_nsa_kernel, tq=tq),
        out_shape=jax.ShapeDtypeStruct((m, NSA_WIDTH), BF16),
        grid=(batch, g, nq),
        in_specs=[
            pl.BlockSpec((tq, hpg * hd), lambda b, gi, i: (b * nq + i, gi)),
            kv_spec(2), kv_spec(3), kv_spec(4), kv_spec(5),
            cmp_spec(0), cmp_spec(1),
            pl.BlockSpec((tq, LANES), lambda b, gi, i: (b * nq + i, gates_col // LANES)),
        ],
        out_specs=pl.BlockSpec((tq, hpg * hd), lambda b, gi, i: (b * nq + i, gi)),
        scratch_shapes=[
            pltpu.VMEM((seq, 2 * hd), BF16),
            pltpu.VMEM((seq, 2 * hd), BF16),
            pltpu.VMEM((seq, 2 * hd), BF16),
            pltpu.VMEM((n_cmp, 2 * hd), BF16),
            pltpu.VMEM((LANES // 4, hpg * n_cmp), BF16),
            pltpu.VMEM((2, tq, tq), F32),
            pltpu.VMEM((hpg, tq, 2 * hd), BF16),
            pltpu.VMEM((2, hpg, tq, tq), F32),
            pltpu.VMEM((2, hpg, tq, tq), F32),
            pltpu.VMEM((hpg, tq, LANES), F32),
            pltpu.VMEM((hpg, tq, 2 * hd), F32),
            pltpu.VMEM((tq, hpg * hd), F32),
        ],
        compiler_params=_params("parallel", "parallel", "arbitrary"),
        name="nsa",
    )(z, z, z, z, z, kv_cmp, kv_cmp, z)


def _lru_kernel(lx_ref, ly_ref, cw_ref, cb_ref, wa_ref, ba_ref, wi_ref, bi_ref, lam_ref, o_ref, x_sc, a_sc, h_sc):
    seq, wt = lx_ref.shape
    x_sc[0:SUBLANES, :] = jnp.zeros((SUBLANES, wt), F32)
    x_sc[SUBLANES:, :] = lx_ref[...].astype(F32)
    taps = cw_ref.shape[0]
    u = cb_ref[...]
    for j in range(taps):
        back = taps - 1 - j
        u = u + x_sc[SUBLANES - back:SUBLANES - back + seq, :] * cw_ref[j:j + 1, :]
    ub = u.astype(BF16)
    r = jax.nn.sigmoid(_dot(ub, wa_ref[0]) + ba_ref[...])
    gate_i = jax.nn.sigmoid(_dot(ub, wi_ref[0]) + bi_ref[...])
    log_a = (LRU_C * r) * jax.nn.log_sigmoid(lam_ref[...])
    a = jnp.exp(log_a)
    a_sc[...] = a
    y = 1.0 - a * a
    h_sc[...] = jnp.where(y > 0.0, y * lax.rsqrt(y), 0.0) * (gate_i * u)

    sub = lax.broadcasted_iota(jnp.int32, (SUBLANES, wt), 0)

    def chunk(k, h):
        r0 = pl.multiple_of(k * SUBLANES, SUBLANES)
        a = a_sc[pl.ds(r0, SUBLANES), :]
        b = h_sc[pl.ds(r0, SUBLANES), :]
        for s in (1, 2, 4):
            a_prev = jnp.where(sub >= s, pltpu.roll(a, s, axis=0), 1.0)
            b_prev = jnp.where(sub >= s, pltpu.roll(b, s, axis=0), 0.0)
            b = a * b_prev + b
            a = a * a_prev
        hs = a * h + b
        h_sc[pl.ds(r0, SUBLANES), :] = hs
        return hs[SUBLANES - 1:SUBLANES, :]

    lax.fori_loop(0, seq // SUBLANES, chunk, jnp.zeros((1, wt), F32), unroll=4)
    o_ref[...] = (jax.nn.gelu(ly_ref[...].astype(F32)) * h_sc[...]).astype(o_ref.dtype)


def _lru(z, lx_col, ly_col, conv_w, conv_b, wa, ba, wi, bi, lam, batch, seq):
    m = z.shape[0]
    wt = LRU_BLOCK_DIM
    nw = LRU_WIDTH // wt
    assert lx_col % wt == 0 and ly_col % wt == 0
    taps = conv_w.shape[0]
    vec = pl.BlockSpec((1, wt), lambda b, j: (0, j))
    blk = pl.BlockSpec((1, wt, wt), lambda b, j: (j, 0, 0))
    return pl.pallas_call(
        _lru_kernel,
        out_shape=jax.ShapeDtypeStruct((m, LRU_WIDTH), BF16),
        grid=(batch, nw),
        in_specs=[
            pl.BlockSpec((seq, wt), lambda b, j: (b, lx_col // wt + j)),
            pl.BlockSpec((seq, wt), lambda b, j: (b, ly_col // wt + j)),
            pl.BlockSpec((taps, wt), lambda b, j: (0, j)),
            vec, blk, vec, blk, vec, vec,
        ],
        out_specs=pl.BlockSpec((seq, wt), lambda b, j: (b, j)),
        scratch_shapes=[pltpu.VMEM((SUBLANES + seq, wt), F32), pltpu.VMEM((seq, wt), F32), pltpu.VMEM((seq, wt), F32)],
        compiler_params=_params("parallel", "parallel"),
        name="lru",
    )(z, z, conv_w, conv_b, wa, ba, wi, bi, lam)


def _merge_kernel(oa_ref, ob_ref, pa_ref, pb_ref, ga_ref, gb_ref, o_ref):
    a = _dot(oa_ref[...], pa_ref[...])
    b = _dot(ob_ref[...], pb_ref[...])
    merged = jax.nn.sigmoid(ga_ref[...].astype(F32)) * a + jax.nn.sigmoid(gb_ref[...].astype(F32)) * b
    o_ref[...] = merged.astype(o_ref.dtype)


def _merge(o_a, o_b, proj_a, proj_b, layer, z, ga_col, gb_col, d_model, tm_target=1024, tn_target=512):
    m, ka = o_a.shape
    kb = o_b.shape[1]
    tm = _tile(m, tm_target, BF16_ROWS)
    tn = _tile(d_model, tn_target)
    ga0 = ga_col // tn
    gb0 = gb_col // tn
    assert ga_col % tn == 0 and gb_col % tn == 0
    return pl.pallas_call(
        _merge_kernel,
        out_shape=jax.ShapeDtypeStruct((m, d_model), BF16),
        grid=(m // tm, d_model // tn),
        in_specs=[
            pl.BlockSpec((tm, ka), lambda i, j: (i, 0)),
            pl.BlockSpec((tm, kb), lambda i, j: (i, 0)),
            pl.BlockSpec((None, ka, tn), lambda i, j: (layer, 0, j)),
            pl.BlockSpec((None, kb, tn), lambda i, j: (layer, 0, j)),
            pl.BlockSpec((tm, tn), lambda i, j: (i, ga0 + j)),
            pl.BlockSpec((tm, tn), lambda i, j: (i, gb0 + j)),
        ],
        out_specs=pl.BlockSpec((tm, tn), lambda i, j: (i, j)),
        compiler_params=_params("parallel", "parallel"),
        name="merge",
    )(o_a, o_b, proj_a, proj_b, z, z)


def _matmul_residual_kernel(a_ref, w_ref, x_ref, g_ref, o_ref):
    o_ref[...] = x_ref[...] + g_ref[...] * _dot(a_ref[...], w_ref[...])


def _matmul_residual(a, w, layer, x, gate, seq, name, tm_target, tn_target):
    m, k = a.shape
    n = w.shape[-1]
    tm = _tile(seq, tm_target, BF16_ROWS)
    tn = _tile(n, tn_target)
    return pl.pallas_call(
        _matmul_residual_kernel,
        out_shape=jax.ShapeDtypeStruct((m, n), F32),
        grid=(n // tn, m // tm),
        in_specs=[
            pl.BlockSpec((tm, k), lambda j, i: (i, 0)),
            pl.BlockSpec((None, k, tn), lambda j, i: (layer, 0, j)),
            pl.BlockSpec((tm, tn), lambda j, i: (i, j)),
            pl.BlockSpec((None, 1, tn), lambda j, i: ((i * tm) // seq, 0, j)),
        ],
        out_specs=pl.BlockSpec((tm, tn), lambda j, i: (i, j)),
        compiler_params=_params("parallel", "parallel"),
        name=name,
    )(a, w, x, gate)


def _ffn_up_kernel(x_ref, xh_ref, g_ref, sc_ref, sh_ref, wg_ref, wv_ref, cw_ref, cb_ref, o_ref, h_ref, *, seq):
    halo = BF16_ROWS
    tm = o_ref.shape[0]

    @pl.when(pl.program_id(1) == 0)
    def _():
        h_ref[0:halo, :] = _norm_mod(xh_ref[...], g_ref[...], sc_ref[...], sh_ref[...]).astype(BF16)
        h_ref[halo:, :] = _norm_mod(x_ref[...], g_ref[...], sc_ref[...], sh_ref[...]).astype(BF16)

    h = h_ref[...]
    gate = _dot(h, wg_ref[...])
    val = _dot(h[halo:], wv_ref[...])
    seq_start = (pl.program_id(0) * tm) % seq == 0
    row = lax.broadcasted_iota(jnp.int32, (tm, 1), 0)
    taps = cw_ref.shape[0]
    conv = cb_ref[...]
    for j in range(taps):
        back = taps - 1 - j
        gj = gate[halo - back:halo - back + tm]
        if back:
            gj = jnp.where(row >= jnp.where(seq_start, back, 0), gj, 0.0)
        conv = conv + gj * cw_ref[j:j + 1, :]
    o_ref[...] = (jax.nn.gelu(conv) * val).astype(o_ref.dtype)


def _ffn_up(x, g, sc, sh, w_up, layer, conv_w, conv_b, seq, tm_target=1024, tn_target=768):
    m, d = x.shape
    d_ff = w_up.shape[-1] // 2
    tm = _tile(seq, tm_target, BF16_ROWS)
    tn = _tile(d_ff, tn_target)
    nj = d_ff // tn
    halo = BF16_ROWS
    taps = conv_w.shape[0]
    bidx = lambda i, j: ((i * tm) // seq, 0, 0)
    return pl.pallas_call(
        functools.partial(_ffn_up_kernel, seq=seq),
        out_shape=jax.ShapeDtypeStruct((m, d_ff), BF16),
        grid=(m // tm, nj),
        in_specs=[
            pl.BlockSpec((tm, d), lambda i, j: (i, 0)),
            pl.BlockSpec((halo, d), lambda i, j: (jnp.maximum(i * (tm // halo) - 1, 0), 0)),
            pl.BlockSpec((1, d), lambda i, j: (0, 0)),
            pl.BlockSpec((None, 1, d), bidx),
            pl.BlockSpec((None, 1, d), bidx),
            pl.BlockSpec((None, d, tn), lambda i, j: (layer, 0, j)),
            pl.BlockSpec((None, d, tn), lambda i, j: (layer, 0, nj + j)),
            pl.BlockSpec((taps, tn), lambda i, j: (0, j)),
            pl.BlockSpec((1, tn), lambda i, j: (0, j)),
        ],
        out_specs=pl.BlockSpec((tm, tn), lambda i, j: (i, j)),
        scratch_shapes=[pltpu.VMEM((halo + tm, d), BF16)],
        compiler_params=_params("parallel", "arbitrary"),
        name="ffn_up",
    )(x, x, g, sc, sh, w_up, w_up, conv_w, conv_b)


def _final_norm_kernel(x_ref, g_ref, o_ref):
    x = x_ref[...]
    inv = lax.rsqrt(jnp.mean(x * x, axis=-1, keepdims=True) + NORM_EPS)
    o_ref[...] = (x * inv) * g_ref[...]


def _final_norm(x, g, tm_target=512):
    m, d = x.shape
    tm = _tile(m, tm_target, SUBLANES)
    return pl.pallas_call(
        _final_norm_kernel,
        out_shape=jax.ShapeDtypeStruct((m, d), F32),
        grid=(m // tm,),
        in_specs=[pl.BlockSpec((tm, d), lambda i: (i, 0)), pl.BlockSpec((1, d), lambda i: (0, 0))],
        out_specs=pl.BlockSpec((tm, d), lambda i: (i, 0)),
        compiler_params=_params("parallel"),
        name="final_norm",
    )(x, g)


def _compress_weights(pos, w1, w2):
    half = CMP_LEN // 2
    return (pos.reshape(2, half * HEAD_DIM),
            w1.reshape(2, half * HEAD_DIM, HEAD_DIM).astype(BF16),
            w2.astype(BF16))


def kernel(x, c, ada_w, ada_b, norm1_g, w_in, cmp_pos_k, cmp_w1_k, cmp_w2_k, cmp_pos_v, cmp_w1_v, cmp_w2_v,
           lru_conv_w, lru_conv_b, lru_wa, lru_ba, lru_wi, lru_bi, lru_lambda, proj_a, proj_b, w_out, norm2_g,
           ffn_up, ffn_conv_w, ffn_conv_b, ffn_down, final_g):
    batch, seq, d = x.shape
    depth = ada_w.shape[0]
    m = batch * seq
    assert seq % SLC_LEN == 0 and seq // SLC_LEN == LANES // 4 and CMP_STRIDE * LANES == seq
    n_attn = NSA_WIDTH + 6 * KV_WIDTH
    n_rest = 2 * LRU_WIDTH + 2 * d
    n_z = -(-(n_attn + n_rest + N_GATES) // (4 * LANES)) * (4 * LANES)
    z_cols = dict(lx=n_attn, ly=n_attn + LRU_WIDTH, ga=n_attn + 2 * LRU_WIDTH, gb=n_attn + 2 * LRU_WIDTH + d,
                  gates=n_attn + n_rest)

    mod = _ada(c, ada_w, ada_b)
    xf = x.reshape(m, d)
    q_scale = jnp.concatenate([jnp.full((1, NSA_WIDTH), HEAD_DIM ** -0.5 * LOG2_E, F32), jnp.ones((1, n_z - NSA_WIDTH), F32)],
                              axis=1)
    w_all = jnp.concatenate([w_in[:, :, :n_attn], w_in[:, :, n_attn + N_GATES:],
                             jnp.pad(w_in[:, :, n_attn:n_attn + N_GATES],
                                     ((0, 0), (0, 0), (0, n_z - n_attn - n_rest - N_GATES)))], axis=2).astype(BF16)
    proj_a, proj_b, w_out, ffn_up, ffn_down, lru_wa, lru_wi = [
        w.astype(BF16) for w in (proj_a, proj_b, w_out, ffn_up, ffn_down, lru_wa, lru_wi)]

    for l in range(depth):
        sh1, sc1, g1, sh2, sc2, g2 = [mod[l, :, k * d:(k + 1) * d].reshape(batch, 1, d) for k in range(6)]
        z = _in_proj(xf, norm1_g[l].reshape(1, d), sc1, sh1, w_all, l, q_scale, BF16, seq)

        pk, w1k, w2k = _compress_weights(cmp_pos_k[l], cmp_w1_k[l], cmp_w2_k[l])
        pv, w1v, w2v = _compress_weights(cmp_pos_v[l], cmp_w1_v[l], cmp_w2_v[l])
        kv_cmp = _compress(z, batch, seq, jnp.stack([pk, pv]), jnp.stack([w1k, w1v]), jnp.stack([w2k, w2v]))

        o_a = _nsa(z, kv_cmp, z_cols["gates"], batch, seq)
        o_b = _lru(z, z_cols["lx"], z_cols["ly"], lru_conv_w[l], lru_conv_b[l].reshape(1, -1), lru_wa[l],
                   lru_ba[l].reshape(1, -1), lru_wi[l], lru_bi[l].reshape(1, -1),
                   lru_lambda[l].reshape(1, -1), batch, seq)
        merged = _merge(o_a, o_b, proj_a, proj_b, l, z, z_cols["ga"], z_cols["gb"], d)
        xf = _matmul_residual(merged, w_out, l, xf, g1, seq, "out_proj", tm_target=512, tn_target=2048)

        act = _ffn_up(xf, norm2_g[l].reshape(1, d), sc2, sh2, ffn_up, l, ffn_conv_w[l],
                      ffn_conv_b[l].reshape(1, -1), seq)
        xf = _matmul_residual(act, ffn_down, l, xf, g2, seq, "ffn_down", tm_target=512, tn_target=1024)

    return _final_norm(xf, final_g.reshape(1, d)).reshape(batch, seq, d)
```

```python
import functools

import jax
import jax.numpy as jnp
from jax import lax
from jax.experimental import pallas as pl
from jax.experimental.pallas import tpu as pltpu

F32 = jnp.float32
BF16 = jnp.bfloat16

NSA_HEADS = 16
NSA_KV_GROUPS = 4
HEADS_PER_GROUP = NSA_HEADS // NSA_KV_GROUPS
HEAD_DIM = 128
NSA_WIDTH = NSA_HEADS * HEAD_DIM
KV_WIDTH = NSA_KV_GROUPS * HEAD_DIM
CMP_LEN = 32
CMP_STRIDE = 16
SLC_LEN = 64
SLC_TOP_N = 16
SLC_LOCAL = 2
FORCE_BONUS = 1e4
WINDOW = 512
LRU_WIDTH = 2048
LRU_BLOCK_DIM = 128
LRU_C = 8.0
NORM_EPS = 1e-6
N_GATES = 3 * NSA_HEADS
GATES_PER_GROUP = 3 * HEADS_PER_GROUP

LANES = 128
SUBLANES = 8
BF16_ROWS = 16
VMEM_LIMIT_BYTES = 56 * 1024 * 1024

MASK_VALUE = -1e30
LOG2_E = 1.4426950408889634

_NT = (((1,), (1,)), ((), ()))


def _params(*sem):
    return pltpu.CompilerParams(dimension_semantics=sem, vmem_limit_bytes=VMEM_LIMIT_BYTES)


def _tile(n, target, quantum=LANES):
    best = None
    t = quantum
    while t <= min(n, target):
        if n % t == 0:
            best = t
        t += quantum
    assert best is not None, (n, target, quantum)
    return best


def _dot(a, b):
    return jnp.dot(a, b, preferred_element_type=F32)


def _store_norm_mod(h_ref, row0, x_ref, g_ref, sc_ref, sh_ref, inv_ref):
    rows, d = x_ref.shape
    x = x_ref[...]
    inv = lax.rsqrt(jnp.mean(x * x, axis=-1, keepdims=True) + NORM_EPS)
    inv_ref[0:rows, :] = jnp.broadcast_to(inv, (rows, LANES))
    gain = g_ref[...]
    scale = 1.0 + sc_ref[...]
    shift = sh_ref[...]

    def slab(i, carry):
        r0 = pl.multiple_of(i * BF16_ROWS, BF16_ROWS)
        inv_s = jnp.concatenate([inv_ref[pl.ds(r0, BF16_ROWS), :]] * (d // LANES), axis=1)
        h = ((x_ref[pl.ds(r0, BF16_ROWS), :] * inv_s) * gain) * scale + shift
        h_ref[pl.ds(row0 + r0, BF16_ROWS), :] = h.astype(BF16)
        return carry

    lax.fori_loop(0, rows // BF16_ROWS, slab, 0, unroll=4 if rows >= 4 * BF16_ROWS else 1)


def _ada_kernel(c_ref, w_ref, b_ref, o_ref):
    c = c_ref[...]
    act = (c * jax.nn.sigmoid(c)).astype(BF16)
    o_ref[...] = _dot(act, w_ref[...].astype(BF16)) + b_ref[...]


def _ada(c, ada_w, ada_b):
    depth, d, n = ada_w.shape
    b = c.shape[0]
    tn = _tile(n, 1024)
    return pl.pallas_call(
        _ada_kernel,
        out_shape=jax.ShapeDtypeStruct((depth, b, n), F32),
        grid=(depth, n // tn),
        in_specs=[
            pl.BlockSpec((b, d), lambda l, j: (0, 0)),
            pl.BlockSpec((None, d, tn), lambda l, j: (l, 0, j)),
            pl.BlockSpec((None, 1, tn), lambda l, j: (l, 0, j)),
        ],
        out_specs=pl.BlockSpec((None, b, tn), lambda l, j: (l, 0, j)),
        compiler_params=_params("parallel", "parallel"),
        name="ada",
    )(c, ada_w, ada_b.reshape(depth, 1, n))


def _in_proj_kernel(x_ref, g_ref, sc_ref, sh_ref, w_ref, cs_ref, o_ref, h_ref, inv_ref):
    @pl.when(pl.program_id(1) == 0)
    def _():
        _store_norm_mod(h_ref, 0, x_ref, g_ref, sc_ref, sh_ref, inv_ref)

    o_ref[...] = (_dot(h_ref[...], w_ref[...]) * cs_ref[...]).astype(o_ref.dtype)


def _in_proj(x, g, sc, sh, w, layer, colscale, out_dtype, seq, tm_target=1024, tn_target=1536):
    m, d = x.shape
    n = w.shape[-1]
    tm = _tile(seq, tm_target, BF16_ROWS)
    tn = _tile(n, tn_target)
    bidx = lambda i, j: ((i * tm) // seq, 0, 0)
    return pl.pallas_call(
        _in_proj_kernel,
        out_shape=jax.ShapeDtypeStruct((m, n), out_dtype),
        grid=(m // tm, n // tn),
        in_specs=[
            pl.BlockSpec((tm, d), lambda i, j: (i, 0)),
            pl.BlockSpec((1, d), lambda i, j: (0, 0)),
            pl.BlockSpec((None, 1, d), bidx),
            pl.BlockSpec((None, 1, d), bidx),
            pl.BlockSpec((None, d, tn), lambda i, j: (layer, 0, j)),
            pl.BlockSpec((1, tn), lambda i, j: (0, j)),
        ],
        out_specs=pl.BlockSpec((tm, tn), lambda i, j: (i, j)),
        scratch_shapes=[pltpu.VMEM((tm, d), BF16), pltpu.VMEM((tm, LANES), F32)],
        compiler_params=_params("parallel", "arbitrary"),
        name="in_proj",
    )(x, g, sc, sh, w, colscale)


def _compress_kernel(kv_ref, pos_ref, w1_ref, w2_ref, o_ref, tok_sc, c_sc):
    hd = kv_ref.shape[1]
    n = c_sc.shape[0]
    tok_sc[...] = kv_ref[...].astype(F32)
    for j in range(CMP_STRIDE):
        c_sc[:, j * hd:(j + 1) * hd] = tok_sc[pl.ds(j, n, stride=CMP_STRIDE), :]
    c = c_sc[...]
    lo = (c + pos_ref[0:1, :]).astype(BF16)
    hi = (c + pos_ref[1:2, :]).astype(BF16)
    first = _dot(lo, w1_ref[0])
    second = _dot(hi, w1_ref[1])
    hid = first + pltpu.roll(second, n - 1, axis=0)
    o_ref[...] = _dot(jax.nn.gelu(hid).astype(BF16), w2_ref[...]).astype(o_ref.dtype)


def _compress(z, batch, seq, pos, w1, w2):
    g = NSA_KV_GROUPS
    hd = HEAD_DIM
    n = seq // CMP_STRIDE
    cw = CMP_STRIDE * hd
    qcols = NSA_WIDTH // hd
    return pl.pallas_call(
        _compress_kernel,
        out_shape=jax.ShapeDtypeStruct((batch, 2, g, n, hd), BF16),
        grid=(batch, 2, g),
        in_specs=[
            pl.BlockSpec((seq, hd), lambda bi, s, gi: (bi, qcols + s * g + gi)),
            pl.BlockSpec((None, 2, cw), lambda bi, s, gi: (s, 0, 0)),
            pl.BlockSpec((None, 2, cw, hd), lambda bi, s, gi: (s, 0, 0, 0)),
            pl.BlockSpec((None, hd, hd), lambda bi, s, gi: (s, 0, 0)),
        ],
        out_specs=pl.BlockSpec((None, None, None, n, hd), lambda bi, s, gi: (bi, s, gi, 0, 0)),
        scratch_shapes=[pltpu.VMEM((seq, hd), F32), pltpu.VMEM((n, cw), F32)],
        compiler_params=_params("parallel", "parallel", "parallel"),
        name="compress",
    )(z, pos, w1, w2)


def _nsa_kernel(q_ref, ks_ref, vs_ref, kw_ref, vw_ref, kc_ref, vc_ref, zg_ref, o_ref,
                kaug_sc, vsaug_sc, vwaug_sc, vcaug_sc, ov_sc, bias_sc, qa_sc, s_sc, sw_sc, m_sc, acc_sc, out_sc,
                *, tq):
    hd = HEAD_DIM
    hpg = HEADS_PER_GROUP
    tk = tq
    n_blk = LANES // 4
    seq = ks_ref.shape[0]
    n_cmp = kc_ref.shape[0]
    grp = pl.program_id(1)
    qi = pl.program_id(2)
    t0 = qi * tq

    @pl.when(qi == 0)
    def _():
        kaug_sc[:, :hd] = ks_ref[...]
        key_blk = lax.shift_right_logical(lax.broadcasted_iota(jnp.int32, (seq, LANES), 0), 6)
        lane_blk = lax.broadcasted_iota(jnp.int32, (seq, LANES), 1) & (n_blk - 1)
        kaug_sc[:, hd:] = jnp.where(key_blk == lane_blk, 1.0, 0.0).astype(BF16)
        ones = jnp.ones((seq, LANES), BF16)
        vsaug_sc[:, :hd] = vs_ref[...]
        vsaug_sc[:, hd:] = ones
        vwaug_sc[:, :hd] = vw_ref[...]
        vwaug_sc[:, hd:] = ones
        vcaug_sc[:, :hd] = vc_ref[...]
        vcaug_sc[:, hd:] = jnp.ones((n_cmp, LANES), BF16)
        cstart = (lax.broadcasted_iota(jnp.int32, (n_blk, hpg * n_cmp), 1) & (n_cmp - 1)) * CMP_STRIDE
        bstart = lax.broadcasted_iota(jnp.int32, (n_blk, hpg * n_cmp), 0) * SLC_LEN
        ov_sc[...] = jnp.where((cstart < bstart + SLC_LEN) & (cstart + CMP_LEN > bstart), 1.0, 0.0).astype(BF16)
        r = lax.broadcasted_iota(jnp.int32, (tq, tk), 0)
        cidx = lax.broadcasted_iota(jnp.int32, (tq, tk), 1)
        bias_sc[0] = jnp.where(cidx <= r, 0.0, MASK_VALUE)
        bias_sc[1] = jnp.where(cidx > r, 0.0, MASK_VALUE)

    q = q_ref[...]
    qs = jnp.concatenate([q[:, h * hd:(h + 1) * hd] for h in range(hpg)], axis=0)
    qpos = t0 + (lax.broadcasted_iota(jnp.int32, (hpg * tq, 1), 0) & (tq - 1))

    def put_scores(dst, slot, k_ref, start, lhs):
        k_t = k_ref[pl.ds(pl.multiple_of(start, tq), tq), :]
        for h in range(hpg):
            dst[slot, h] = lax.dot_general(lhs(h), k_t, _NT, preferred_element_type=F32)

    def reset():
        m_sc[...] = jnp.full(m_sc.shape, MASK_VALUE, F32)
        acc_sc[...] = jnp.zeros(acc_sc.shape, F32)

    def fold_head(src, slot, h, v_t, bias):
        sc = src[slot, h]
        if bias is not None:
            sc = sc + bias
        m_prev = m_sc[h]
        m_new = jnp.maximum(m_prev, jnp.max(sc, axis=-1, keepdims=True))
        alpha = jnp.exp2(m_prev - m_new)
        pe = jnp.exp2(sc - jnp.concatenate([m_new] * (tq // LANES), axis=1))
        acc_sc[h] = jnp.concatenate([alpha, alpha], axis=1) * acc_sc[h] + _dot(pe.astype(BF16), v_t)
        m_sc[h] = m_new

    def fold(src, slot, v_ref, start, mask):
        v_t = v_ref[pl.ds(pl.multiple_of(start, tq), tq), :]
        bias = None if mask is None else bias_sc[mask]
        for h in range(hpg):
            fold_head(src, slot, h, v_t, bias)

    def finish(h):
        acc = acc_sc[h]
        return acc[:, :hd] / acc[:, hd:]

    q_head = lambda h: q[:, h * hd:(h + 1) * hd]
    s = lax.dot_general(qs, kc_ref[...], _NT, preferred_element_type=F32)
    st = lax.dot_general(kc_ref[...], qs, _NT, preferred_element_type=F32)
    put_scores(sw_sc, 0, kw_ref, t0, q_head)

    gt = jax.nn.sigmoid(pltpu.roll(zg_ref[...].astype(F32), (LANES - GATES_PER_GROUP * grp) & (LANES - 1), axis=1))

    def gate(h, branch):
        c = 3 * h + branch
        return gt[:, c:c + 1]

    cmp_end = lax.broadcasted_iota(jnp.int32, (1, n_cmp), 1) * CMP_STRIDE + (CMP_LEN - 1)
    s = jnp.where(cmp_end <= qpos, s, -jnp.inf)
    mx = jnp.max(s, axis=-1, keepdims=True)
    mx = jnp.where(jnp.isfinite(mx), mx, 0.0)
    pe_cmp = jnp.exp2(s - mx).astype(BF16)

    end_t = lax.broadcasted_iota(jnp.int32, (n_cmp, 1), 0) * CMP_STRIDE + (CMP_LEN - 1)
    qpos_t = t0 + (lax.broadcasted_iota(jnp.int32, (1, hpg * tq), 1) & (tq - 1))
    st = jnp.where(end_t <= qpos_t, st, -jnp.inf)
    mt = jnp.max(st, axis=0, keepdims=True)
    mt = jnp.where(jnp.isfinite(mt), mt, 0.0)
    pt = jnp.exp2(st - mt)
    pt = pt / jnp.maximum(jnp.sum(pt, axis=0, keepdims=True), 1e-30)
    p_heads = jnp.concatenate([pt[:, h * tq:(h + 1) * tq] for h in range(hpg)], axis=0).astype(BF16)
    imp = _dot(ov_sc[...], p_heads)
    acc = _dot(pe_cmp, vcaug_sc[...])
    put_scores(sw_sc, 1, kw_ref, jnp.maximum(t0 - tq, 0), q_head)

    blk = lax.broadcasted_iota(jnp.int32, (n_blk, tq), 0)
    cur = lax.shift_right_logical(t0 + lax.broadcasted_iota(jnp.int32, (n_blk, tq), 1), 6)
    valid = blk <= cur
    forced = (blk == 0) | (valid & (blk > cur - SLC_LOCAL))
    score = jnp.where(forced, imp + FORCE_BONUS, jnp.where(valid, imp, -1.0))
    key = lax.bitcast_convert_type(score, jnp.int32)
    rank = jnp.zeros((n_blk, tq), jnp.int32)
    for r in range(1, n_blk):
        other = pltpu.roll(key, r, axis=0)
        rank = rank + jnp.where(other + jnp.where(blk >= r, 1, 0) > key, 1, 0)
    sel_bias = jnp.where((rank < SLC_TOP_N) & valid, 0.0, MASK_VALUE)
    sel_rows = jnp.concatenate([sel_bias, jnp.zeros((LANES - n_blk, tq), F32)], axis=0).T.astype(BF16)
    for h in range(hpg):
        qa_sc[h] = jnp.concatenate([q_head(h), sel_rows], axis=1)
    o_cmp = acc[:, :hd] / jnp.maximum(acc[:, hd:], 1e-30)
    for h in range(hpg):
        out_sc[:, h * hd:(h + 1) * hd] = gate(h, 0) * o_cmp[h * tq:(h + 1) * tq]

    assert WINDOW == tq
    reset()
    k_first = kaug_sc[0:tq, :]
    v_diag = vwaug_sc[pl.ds(pl.multiple_of(t0, tq), tq), :]
    for h in range(hpg):
        s_sc[0, h] = lax.dot_general(qa_sc[h], k_first, _NT, preferred_element_type=F32)
        fold_head(sw_sc, 0, h, v_diag, bias_sc[0])

    @pl.when(qi >= 1)
    def _():
        fold(sw_sc, 1, vwaug_sc, t0 - tq, 1)

    for h in range(hpg):
        out_sc[:, h * hd:(h + 1) * hd] += gate(h, 2) * finish(h)

    reset()

    def slc_pair(j, carry):
        e = 2 * j * tq
        put_scores(s_sc, 1, kaug_sc, e + tq, lambda h: qa_sc[h])
        fold(s_sc, 0, vsaug_sc, e, None)
        put_scores(s_sc, 0, kaug_sc, e + 2 * tq, lambda h: qa_sc[h])
        fold(s_sc, 1, vsaug_sc, e + tq, None)
        return carry

    lax.fori_loop(0, qi // 2, slc_pair, 0)

    @pl.when(qi % 2 == 1)
    def _():
        put_scores(s_sc, 1, kaug_sc, t0, lambda h: qa_sc[h])
        fold(s_sc, 0, vsaug_sc, t0 - tq, None)
        fold(s_sc, 1, vsaug_sc, t0, 0)

    @pl.when(qi % 2 == 0)
    def _():
        fold(s_sc, 0, vsaug_sc, t0, 0)

    for h in range(hpg):
        o_h = out_sc[:, h * hd:(h + 1) * hd] + gate(h, 1) * finish(h)
        o_ref[:, h * hd:(h + 1) * hd] = o_h.astype(o_ref.dtype)


def _nsa(z, kv_cmp, gates_col, batch, seq, tq=WINDOW):
    m = z.shape[0]
    assert gates_col % LANES == 0
    g = NSA_KV_GROUPS
    hd = HEAD_DIM
    hpg = HEADS_PER_GROUP
    n_cmp = kv_cmp.shape[3]
    nq = seq // tq
    qcols = NSA_WIDTH // hd
    assert (tq // 4) % SLC_LEN == 0 and n_cmp == LANES

    def kv_spec(slot):
        return pl.BlockSpec((seq, hd), lambda b, gi, i: (b, qcols + slot * g + gi))

    def cmp_spec(slot):
        return pl.BlockSpec((None, None, None, n_cmp, hd), lambda b, gi, i: (b, slot, gi, 0, 0))

    return pl.pallas_call(
        functools.partial(_nsa_kernel, tq=tq),
        out_shape=jax.ShapeDtypeStruct((m, NSA_WIDTH), BF16),
        grid=(batch, g, nq),
        in_specs=[
            pl.BlockSpec((tq, hpg * hd), lambda b, gi, i: (b * nq + i, gi)),
            kv_spec(2), kv_spec(3), kv_spec(4), kv_spec(5),
            cmp_spec(0), cmp_spec(1),
            pl.BlockSpec((tq, LANES), lambda b, gi, i: (b * nq + i, gates_col // LANES)),
        ],
        out_specs=pl.BlockSpec((tq, hpg * hd), lambda b, gi, i: (b * nq + i, gi)),
        scratch_shapes=[
            pltpu.VMEM((seq, 2 * hd), BF16),
            pltpu.VMEM((seq, 2 * hd), BF16),
            pltpu.VMEM((seq, 2 * hd), BF16),
            pltpu.VMEM((n_cmp, 2 * hd), BF16),
            pltpu.VMEM((LANES // 4, hpg * n_cmp), BF16),
            pltpu.VMEM((2, tq, tq), F32),
            pltpu.VMEM((hpg, tq, 2 * hd), BF16),
            pltpu.VMEM((2, hpg, tq, tq), F32),
            pltpu.VMEM((2, hpg, tq, tq), F32),
            pltpu.VMEM((hpg, tq, LANES), F32),
            pltpu.VMEM((hpg, tq, 2 * hd), F32),
            pltpu.VMEM((tq, hpg * hd), F32),
        ],
        compiler_params=_params("parallel", "parallel", "arbitrary"),
        name="nsa",
    )(z, z, z, z, z, kv_cmp, kv_cmp, z)


def _lru_kernel(lx_ref, ly_ref, cw_ref, cb_ref, wa_ref, ba_ref, wi_ref, bi_ref, lam_ref, o_ref, x_sc, a_sc, h_sc):
    seq, wt = lx_ref.shape
    x_sc[0:SUBLANES, :] = jnp.zeros((SUBLANES, wt), F32)
    x_sc[SUBLANES:, :] = lx_ref[...].astype(F32)
    taps = cw_ref.shape[0]
    u = cb_ref[...]
    for j in range(taps):
        back = taps - 1 - j
        u = u + x_sc[SUBLANES - back:SUBLANES - back + seq, :] * cw_ref[j:j + 1, :]
    ub = u.astype(BF16)
    r = jax.nn.sigmoid(_dot(ub, wa_ref[0]) + ba_ref[...])
    gate_i = jax.nn.sigmoid(_dot(ub, wi_ref[0]) + bi_ref[...])
    log_a = (LRU_C * r) * jax.nn.log_sigmoid(lam_ref[...])
    a = jnp.exp(log_a)
    a_sc[...] = a
    y = 1.0 - a * a
    h_sc[...] = jnp.where(y > 0.0, y * lax.rsqrt(y), 0.0) * (gate_i * u)

    sub = lax.broadcasted_iota(jnp.int32, (SUBLANES, wt), 0)

    def chunk(k, h):
        r0 = pl.multiple_of(k * SUBLANES, SUBLANES)
        a = a_sc[pl.ds(r0, SUBLANES), :]
        b = h_sc[pl.ds(r0, SUBLANES), :]
        for s in (1, 2, 4):
            a_prev = jnp.where(sub >= s, pltpu.roll(a, s, axis=0), 1.0)
            b_prev = jnp.where(sub >= s, pltpu.roll(b, s, axis=0), 0.0)
            b = a * b_prev + b
            a = a * a_prev
        hs = a * h + b
        h_sc[pl.ds(r0, SUBLANES), :] = hs
        return hs[SUBLANES - 1:SUBLANES, :]

    lax.fori_loop(0, seq // SUBLANES, chunk, jnp.zeros((1, wt), F32), unroll=4)
    o_ref[...] = (jax.nn.gelu(ly_ref[...].astype(F32)) * h_sc[...]).astype(o_ref.dtype)


def _lru(z, lx_col, ly_col, conv_w, conv_b, wa, ba, wi, bi, lam, batch, seq):
    m = z.shape[0]
    wt = LRU_BLOCK_DIM
    nw = LRU_WIDTH // wt
    assert lx_col % wt == 0 and ly_col % wt == 0
    taps = conv_w.shape[0]
    vec = pl.BlockSpec((1, wt), lambda b, j: (0, j))
    blk = pl.BlockSpec((1, wt, wt), lambda b, j: (j, 0, 0))
    return pl.pallas_call(
        _lru_kernel,
        out_shape=jax.ShapeDtypeStruct((m, LRU_WIDTH), BF16),
        grid=(batch, nw),
        in_specs=[
            pl.BlockSpec((seq, wt), lambda b, j: (b, lx_col // wt + j)),
            pl.BlockSpec((seq, wt), lambda b, j: (b, ly_col // wt + j)),
            pl.BlockSpec((taps, wt), lambda b, j: (0, j)),
            vec, blk, vec, blk, vec, vec,
        ],
        out_specs=pl.BlockSpec((seq, wt), lambda b, j: (b, j)),
        scratch_shapes=[pltpu.VMEM((SUBLANES + seq, wt), F32), pltpu.VMEM((seq, wt), F32), pltpu.VMEM((seq, wt), F32)],
        compiler_params=_params("parallel", "parallel"),
        name="lru",
    )(z, z, conv_w, conv_b, wa, ba, wi, bi, lam)


def _merge_kernel(oa_ref, ob_ref, pa_ref, pb_ref, ga_ref, gb_ref, o_ref):
    a = _dot(oa_ref[...], pa_ref[...])
    b = _dot(ob_ref[...], pb_ref[...])
    merged = jax.nn.sigmoid(ga_ref[...].astype(F32)) * a + jax.nn.sigmoid(gb_ref[...].astype(F32)) * b
    o_ref[...] = merged.astype(o_ref.dtype)


def _merge(o_a, o_b, proj_a, proj_b, layer, z, ga_col, gb_col, d_model, tm_target=1024, tn_target=512):
    m, ka = o_a.shape
    kb = o_b.shape[1]
    tm = _tile(m, tm_target, BF16_ROWS)
    tn = _tile(d_model, tn_target)
    ga0 = ga_col // tn
    gb0 = gb_col // tn
    assert ga_col % tn == 0 and gb_col % tn == 0
    return pl.pallas_call(
        _merge_kernel,
        out_shape=jax.ShapeDtypeStruct((m, d_model), BF16),
        grid=(m // tm, d_model // tn),
        in_specs=[
            pl.BlockSpec((tm, ka), lambda i, j: (i, 0)),
            pl.BlockSpec((tm, kb), lambda i, j: (i, 0)),
            pl.BlockSpec((None, ka, tn), lambda i, j: (layer, 0, j)),
            pl.BlockSpec((None, kb, tn), lambda i, j: (layer, 0, j)),
            pl.BlockSpec((tm, tn), lambda i, j: (i, ga0 + j)),
            pl.BlockSpec((tm, tn), lambda i, j: (i, gb0 + j)),
        ],
        out_specs=pl.BlockSpec((tm, tn), lambda i, j: (i, j)),
        compiler_params=_params("parallel", "parallel"),
        name="merge",
    )(o_a, o_b, proj_a, proj_b, z, z)


def _matmul_residual_kernel(a_ref, w_ref, x_ref, g_ref, o_ref):
    o_ref[...] = x_ref[...] + g_ref[...] * _dot(a_ref[...], w_ref[...])


def _matmul_residual(a, w, layer, x, gate, seq, name, tm_target, tn_target):
    m, k = a.shape
    n = w.shape[-1]
    tm = _tile(seq, tm_target, BF16_ROWS)
    tn = _tile(n, tn_target)
    return pl.pallas_call(
        _matmul_residual_kernel,
        out_shape=jax.ShapeDtypeStruct((m, n), F32),
        grid=(n // tn, m // tm),
        in_specs=[
            pl.BlockSpec((tm, k), lambda j, i: (i, 0)),
            pl.BlockSpec((None, k, tn), lambda j, i: (layer, 0, j)),
            pl.BlockSpec((tm, tn), lambda j, i: (i, j)),
            pl.BlockSpec((None, 1, tn), lambda j, i: ((i * tm) // seq, 0, j)),
        ],
        out_specs=pl.BlockSpec((tm, tn), lambda j, i: (i, j)),
        compiler_params=_params("parallel", "parallel"),
        name=name,
    )(a, w, x, gate)


def _ffn_up_kernel(x_ref, xh_ref, g_ref, sc_ref, sh_ref, wg_ref, wv_ref, cw_ref, cb_ref, o_ref, h_ref, inv_ref, *, seq):
    halo = BF16_ROWS
    tm = o_ref.shape[0]

    @pl.when(pl.program_id(1) == 0)
    def _():
        _store_norm_mod(h_ref, 0, xh_ref, g_ref, sc_ref, sh_ref, inv_ref)
        _store_norm_mod(h_ref, halo, x_ref, g_ref, sc_ref, sh_ref, inv_ref)

    h = h_ref[...]
    gate = _dot(h, wg_ref[...])
    val = _dot(h[halo:], wv_ref[...])
    seq_start = (pl.program_id(0) * tm) % seq == 0
    row = lax.broadcasted_iota(jnp.int32, (tm, 1), 0)
    taps = cw_ref.shape[0]
    conv = cb_ref[...]
    for j in range(taps):
        back = taps - 1 - j
        gj = gate[halo - back:halo - back + tm]
        if back:
            gj = jnp.where(row >= jnp.where(seq_start, back, 0), gj, 0.0)
        conv = conv + gj * cw_ref[j:j + 1, :]
    o_ref[...] = (jax.nn.gelu(conv) * val).astype(o_ref.dtype)


def _ffn_up(x, g, sc, sh, w_up, layer, conv_w, conv_b, seq, tm_target=1024, tn_target=768):
    m, d = x.shape
    d_ff = w_up.shape[-1] // 2
    tm = _tile(seq, tm_target, BF16_ROWS)
    tn = _tile(d_ff, tn_target)
    nj = d_ff // tn
    halo = BF16_ROWS
    taps = conv_w.shape[0]
    bidx = lambda i, j: ((i * tm) // seq, 0, 0)
    return pl.pallas_call(
        functools.partial(_ffn_up_kernel, seq=seq),
        out_shape=jax.ShapeDtypeStruct((m, d_ff), BF16),
        grid=(m // tm, nj),
        in_specs=[
            pl.BlockSpec((tm, d), lambda i, j: (i, 0)),
            pl.BlockSpec((halo, d), lambda i, j: (jnp.maximum(i * (tm // halo) - 1, 0), 0)),
            pl.BlockSpec((1, d), lambda i, j: (0, 0)),
            pl.BlockSpec((None, 1, d), bidx),
            pl.BlockSpec((None, 1, d), bidx),
            pl.BlockSpec((None, d, tn), lambda i, j: (layer, 0, j)),
            pl.BlockSpec((None, d, tn), lambda i, j: (layer, 0, nj + j)),
            pl.BlockSpec((taps, tn), lambda i, j: (0, j)),
            pl.BlockSpec((1, tn), lambda i, j: (0, j)),
        ],
        out_specs=pl.BlockSpec((tm, tn), lambda i, j: (i, j)),
        scratch_shapes=[pltpu.VMEM((halo + tm, d), BF16), pltpu.VMEM((tm, LANES), F32)],
        compiler_params=_params("parallel", "arbitrary"),
        name="ffn_up",
    )(x, x, g, sc, sh, w_up, w_up, conv_w, conv_b)


def _final_norm_kernel(x_ref, g_ref, o_ref):
    x = x_ref[...]
    inv = lax.rsqrt(jnp.mean(x * x, axis=-1, keepdims=True) + NORM_EPS)
    o_ref[...] = (x * inv) * g_ref[...]


def _final_norm(x, g, tm_target=512):
    m, d = x.shape
    tm = _tile(m, tm_target, SUBLANES)
    return pl.pallas_call(
        _final_norm_kernel,
        out_shape=jax.ShapeDtypeStruct((m, d), F32),
        grid=(m // tm,),
        in_specs=[pl.BlockSpec((tm, d), lambda i: (i, 0)), pl.BlockSpec((1, d), lambda i: (0, 0))],
        out_specs=pl.BlockSpec((tm, d), lambda i: (i, 0)),
        compiler_params=_params("parallel"),
        name="final_norm",
    )(x, g)


def _compress_weights(pos, w1, w2):
    half = CMP_LEN // 2
    return (pos.reshape(2, half * HEAD_DIM),
            w1.reshape(2, half * HEAD_DIM, HEAD_DIM).astype(BF16),
            w2.astype(BF16))


def kernel(x, c, ada_w, ada_b, norm1_g, w_in, cmp_pos_k, cmp_w1_k, cmp_w2_k, cmp_pos_v, cmp_w1_v, cmp_w2_v,
           lru_conv_w, lru_conv_b, lru_wa, lru_ba, lru_wi, lru_bi, lru_lambda, proj_a, proj_b, w_out, norm2_g,
           ffn_up, ffn_conv_w, ffn_conv_b, ffn_down, final_g):
    batch, seq, d = x.shape
    depth = ada_w.shape[0]
    m = batch * seq
    assert seq % SLC_LEN == 0 and seq // SLC_LEN == LANES // 4 and CMP_STRIDE * LANES == seq
    n_attn = NSA_WIDTH + 6 * KV_WIDTH
    n_rest = 2 * LRU_WIDTH + 2 * d
    n_z = -(-(n_attn + n_rest + N_GATES) // (4 * LANES)) * (4 * LANES)
    z_cols = dict(lx=n_attn, ly=n_attn + LRU_WIDTH, ga=n_attn + 2 * LRU_WIDTH, gb=n_attn + 2 * LRU_WIDTH + d,
                  gates=n_attn + n_rest)

    mod = _ada(c, ada_w, ada_b)
    xf = x.reshape(m, d)
    q_scale = jnp.concatenate([jnp.full((1, NSA_WIDTH), HEAD_DIM ** -0.5 * LOG2_E, F32), jnp.ones((1, n_z - NSA_WIDTH), F32)],
                              axis=1)
    w_all = jnp.concatenate([w_in[:, :, :n_attn], w_in[:, :, n_attn + N_GATES:],
                             jnp.pad(w_in[:, :, n_attn:n_attn + N_GATES],
                                     ((0, 0), (0, 0), (0, n_z - n_attn - n_rest - N_GATES)))], axis=2).astype(BF16)
    proj_a, proj_b, w_out, ffn_up, ffn_down, lru_wa, lru_wi = [
        w.astype(BF16) for w in (proj_a, proj_b, w_out, ffn_up, ffn_down, lru_wa, lru_wi)]

    for l in range(depth):
        sh1, sc1, g1, sh2, sc2, g2 = [mod[l, :, k * d:(k + 1) * d].reshape(batch, 1, d) for k in range(6)]
        z = _in_proj(xf, norm1_g[l].reshape(1, d), sc1, sh1, w_all, l, q_scale, BF16, seq)

        pk, w1k, w2k = _compress_weights(cmp_pos_k[l], cmp_w1_k[l], cmp_w2_k[l])
        pv, w1v, w2v = _compress_weights(cmp_pos_v[l], cmp_w1_v[l], cmp_w2_v[l])
        kv_cmp = _compress(z, batch, seq, jnp.stack([pk, pv]), jnp.stack([w1k, w1v]), jnp.stack([w2k, w2v]))

        o_a = _nsa(z, kv_cmp, z_cols["gates"], batch, seq)
        o_b = _lru(z, z_cols["lx"], z_cols["ly"], lru_conv_w[l], lru_conv_b[l].reshape(1, -1), lru_wa[l],
                   lru_ba[l].reshape(1, -1), lru_wi[l], lru_bi[l].reshape(1, -1),
                   lru_lambda[l].reshape(1, -1), batch, seq)
        merged = _merge(o_a, o_b, proj_a, proj_b, l, z, z_cols["ga"], z_cols["gb"], d)
        xf = _matmul_residual(merged, w_out, l, xf, g1, seq, "out_proj", tm_target=512, tn_target=2048)

        act = _ffn_up(xf, norm2_g[l].reshape(1, d), sc2, sh2, ffn_up, l, ffn_conv_w[l],
                      ffn_conv_b[l].reshape(1, -1), seq)
        xf = _matmul_residual(act, ffn_down, l, xf, g2, seq, "ffn_down", tm_target=512, tn_target=1024)

    return _final_norm(xf, final_g.reshape(1, d)).reshape(batch, seq, d)
```

```python
import functools

import jax
import jax.numpy as jnp
from jax import lax
from jax.experimental import pallas as pl
from jax.experimental.pallas import tpu as pltpu

F32 = jnp.float32
BF16 = jnp.bfloat16

NSA_HEADS = 16
NSA_KV_GROUPS = 4
HEADS_PER_GROUP = NSA_HEADS // NSA_KV_GROUPS
HEAD_DIM = 128
NSA_WIDTH = NSA_HEADS * HEAD_DIM
KV_WIDTH = NSA_KV_GROUPS * HEAD_DIM
CMP_LEN = 32
CMP_STRIDE = 16
SLC_LEN = 64
SLC_TOP_N = 16
SLC_LOCAL = 2
FORCE_BONUS = 1e4
WINDOW = 512
LRU_WIDTH = 2048
LRU_BLOCK_DIM = 128
LRU_C = 8.0
NORM_EPS = 1e-6
LRU_SEG_PAD = 8
LRU_SCAN_UNROLL = 8
N_GATES = 3 * NSA_HEADS
GATES_PER_GROUP = 3 * HEADS_PER_GROUP

LANES = 128
SUBLANES = 8
BF16_ROWS = 16
VMEM_LIMIT_BYTES = 56 * 1024 * 1024

MASK_VALUE = -1e30
LOG2_E = 1.4426950408889634

_NT = (((1,), (1,)), ((), ()))


def _params(*sem):
    return pltpu.CompilerParams(dimension_semantics=sem, vmem_limit_bytes=VMEM_LIMIT_BYTES)


def _tile(n, target, quantum=LANES):
    best = None
    t = quantum
    while t <= min(n, target):
        if n % t == 0:
            best = t
        t += quantum
    assert best is not None, (n, target, quantum)
    return best


def _dot(a, b):
    return jnp.dot(a, b, preferred_element_type=F32)


def _store_norm_mod(h_ref, row0, x_ref, g_ref, sc_ref, sh_ref, inv_ref):
    rows, d = x_ref.shape
    x = x_ref[...]
    inv = lax.rsqrt(jnp.mean(x * x, axis=-1, keepdims=True) + NORM_EPS)
    inv_ref[0:rows, :] = jnp.broadcast_to(inv, (rows, LANES))
    gain = g_ref[...]
    scale = 1.0 + sc_ref[...]
    shift = sh_ref[...]

    def slab(i, carry):
        r0 = pl.multiple_of(i * BF16_ROWS, BF16_ROWS)
        inv_s = jnp.concatenate([inv_ref[pl.ds(r0, BF16_ROWS), :]] * (d // LANES), axis=1)
        h = ((x_ref[pl.ds(r0, BF16_ROWS), :] * inv_s) * gain) * scale + shift
        h_ref[pl.ds(row0 + r0, BF16_ROWS), :] = h.astype(BF16)
        return carry

    lax.fori_loop(0, rows // BF16_ROWS, slab, 0, unroll=4 if rows >= 4 * BF16_ROWS else 1)


def _ada_kernel(c_ref, w_ref, b_ref, o_ref):
    c = c_ref[...]
    act = (c * jax.nn.sigmoid(c)).astype(BF16)
    o_ref[...] = _dot(act, w_ref[...].astype(BF16)) + b_ref[...]


def _ada(c, ada_w, ada_b):
    depth, d, n = ada_w.shape
    b = c.shape[0]
    tn = _tile(n, 1024)
    return pl.pallas_call(
        _ada_kernel,
        out_shape=jax.ShapeDtypeStruct((depth, b, n), F32),
        grid=(depth, n // tn),
        in_specs=[
            pl.BlockSpec((b, d), lambda l, j: (0, 0)),
            pl.BlockSpec((None, d, tn), lambda l, j: (l, 0, j)),
            pl.BlockSpec((None, 1, tn), lambda l, j: (l, 0, j)),
        ],
        out_specs=pl.BlockSpec((None, b, tn), lambda l, j: (l, 0, j)),
        compiler_params=_params("parallel", "parallel"),
        name="ada",
    )(c, ada_w, ada_b.reshape(depth, 1, n))


def _in_proj_kernel(x_ref, g_ref, sc_ref, sh_ref, w_ref, cs_ref, o_ref, h_ref, inv_ref):
    @pl.when(pl.program_id(1) == 0)
    def _():
        _store_norm_mod(h_ref, 0, x_ref, g_ref, sc_ref, sh_ref, inv_ref)

    o_ref[...] = (_dot(h_ref[...], w_ref[...]) * cs_ref[...]).astype(o_ref.dtype)


def _in_proj(x, g, sc, sh, w, layer, colscale, out_dtype, seq, tm_target=1024, tn_target=1536):
    m, d = x.shape
    n = w.shape[-1]
    tm = _tile(seq, tm_target, BF16_ROWS)
    tn = _tile(n, tn_target)
    bidx = lambda i, j: ((i * tm) // seq, 0, 0)
    return pl.pallas_call(
        _in_proj_kernel,
        out_shape=jax.ShapeDtypeStruct((m, n), out_dtype),
        grid=(m // tm, n // tn),
        in_specs=[
            pl.BlockSpec((tm, d), lambda i, j: (i, 0)),
            pl.BlockSpec((1, d), lambda i, j: (0, 0)),
            pl.BlockSpec((None, 1, d), bidx),
            pl.BlockSpec((None, 1, d), bidx),
            pl.BlockSpec((None, d, tn), lambda i, j: (layer, 0, j)),
            pl.BlockSpec((1, tn), lambda i, j: (0, j)),
        ],
        out_specs=pl.BlockSpec((tm, tn), lambda i, j: (i, j)),
        scratch_shapes=[pltpu.VMEM((tm, d), BF16), pltpu.VMEM((tm, LANES), F32)],
        compiler_params=_params("parallel", "arbitrary"),
        name="in_proj",
    )(x, g, sc, sh, w, colscale)


def _compress_kernel(kv_ref, pos_ref, w1_ref, w2_ref, o_ref, tok_sc, c_sc):
    hd = kv_ref.shape[1]
    n = c_sc.shape[0]
    tok_sc[...] = kv_ref[...].astype(F32)
    for j in range(CMP_STRIDE):
        c_sc[:, j * hd:(j + 1) * hd] = tok_sc[pl.ds(j, n, stride=CMP_STRIDE), :]
    c = c_sc[...]
    lo = (c + pos_ref[0:1, :]).astype(BF16)
    hi = (c + pos_ref[1:2, :]).astype(BF16)
    first = _dot(lo, w1_ref[0])
    second = _dot(hi, w1_ref[1])
    hid = first + pltpu.roll(second, n - 1, axis=0)
    o_ref[...] = _dot(jax.nn.gelu(hid).astype(BF16), w2_ref[...]).astype(o_ref.dtype)


def _compress(z, batch, seq, pos, w1, w2):
    g = NSA_KV_GROUPS
    hd = HEAD_DIM
    n = seq // CMP_STRIDE
    cw = CMP_STRIDE * hd
    qcols = NSA_WIDTH // hd
    return pl.pallas_call(
        _compress_kernel,
        out_shape=jax.ShapeDtypeStruct((batch, 2, g, n, hd), BF16),
        grid=(batch, 2, g),
        in_specs=[
            pl.BlockSpec((seq, hd), lambda bi, s, gi: (bi, qcols + s * g + gi)),
            pl.BlockSpec((None, 2, cw), lambda bi, s, gi: (s, 0, 0)),
            pl.BlockSpec((None, 2, cw, hd), lambda bi, s, gi: (s, 0, 0, 0)),
            pl.BlockSpec((None, hd, hd), lambda bi, s, gi: (s, 0, 0)),
        ],
        out_specs=pl.BlockSpec((None, None, None, n, hd), lambda bi, s, gi: (bi, s, gi, 0, 0)),
        scratch_shapes=[pltpu.VMEM((seq, hd), F32), pltpu.VMEM((n, cw), F32)],
        compiler_params=_params("parallel", "parallel", "parallel"),
        name="compress",
    )(z, pos, w1, w2)


def _nsa_kernel(q_ref, ks_ref, vs_ref, kw_ref, vw_ref, kc_ref, vc_ref, zg_ref, o_ref,
                kaug_sc, vsaug_sc, vwaug_sc, vcaug_sc, ov_sc, bias_sc, qa_sc, gate_sc, s_sc, sw_sc, m_sc, acc_sc, out_sc,
                *, tq):
    hd = HEAD_DIM
    hpg = HEADS_PER_GROUP
    tk = tq
    n_blk = LANES // 4
    seq = ks_ref.shape[0]
    n_cmp = kc_ref.shape[0]
    grp = pl.program_id(1)
    qi = pl.program_id(2)
    t0 = qi * tq

    @pl.when(qi == 0)
    def _():
        kaug_sc[:, :hd] = ks_ref[...]
        key_blk = lax.shift_right_logical(lax.broadcasted_iota(jnp.int32, (seq, LANES), 0), 6)
        lane_blk = lax.broadcasted_iota(jnp.int32, (seq, LANES), 1) & (n_blk - 1)
        kaug_sc[:, hd:] = jnp.where(key_blk == lane_blk, 1.0, 0.0).astype(BF16)
        ones = jnp.ones((seq, LANES), BF16)
        vsaug_sc[:, :hd] = vs_ref[...]
        vsaug_sc[:, hd:] = ones
        vwaug_sc[:, :hd] = vw_ref[...]
        vwaug_sc[:, hd:] = ones
        vcaug_sc[:, :hd] = vc_ref[...]
        vcaug_sc[:, hd:] = jnp.ones((n_cmp, LANES), BF16)
        cstart = (lax.broadcasted_iota(jnp.int32, (n_blk, hpg * n_cmp), 1) & (n_cmp - 1)) * CMP_STRIDE
        bstart = lax.broadcasted_iota(jnp.int32, (n_blk, hpg * n_cmp), 0) * SLC_LEN
        ov_sc[...] = jnp.where((cstart < bstart + SLC_LEN) & (cstart + CMP_LEN > bstart), 1.0, 0.0).astype(BF16)
        r = lax.broadcasted_iota(jnp.int32, (tq, tk), 0)
        cidx = lax.broadcasted_iota(jnp.int32, (tq, tk), 1)
        bias_sc[0] = jnp.where(cidx <= r, 0.0, MASK_VALUE)
        bias_sc[1] = jnp.where(cidx > r, 0.0, MASK_VALUE)

    q = q_ref[...]
    qs = jnp.concatenate([q[:, h * hd:(h + 1) * hd] for h in range(hpg)], axis=0)
    qpos = t0 + (lax.broadcasted_iota(jnp.int32, (hpg * tq, 1), 0) & (tq - 1))

    def put_scores(dst, slot, k_ref, start, lhs):
        k_t = k_ref[pl.ds(pl.multiple_of(start, tq), tq), :]
        for h in range(hpg):
            dst[slot, h] = lax.dot_general(lhs(h), k_t, _NT, preferred_element_type=F32)

    def reset():
        m_sc[...] = jnp.full(m_sc.shape, MASK_VALUE, F32)
        acc_sc[...] = jnp.zeros(acc_sc.shape, F32)

    def fold_head(src, slot, h, v_t, bias):
        sc = src[slot, h]
        if bias is not None:
            sc = sc + bias
        m_prev = m_sc[h]
        m_new = jnp.maximum(m_prev, jnp.max(sc, axis=-1, keepdims=True))
        alpha = jnp.exp2(m_prev - m_new)
        pe = jnp.exp2(sc - jnp.concatenate([m_new] * (tq // LANES), axis=1))
        acc_sc[h] = jnp.concatenate([alpha, alpha], axis=1) * acc_sc[h] + _dot(pe.astype(BF16), v_t)
        m_sc[h] = m_new

    def fold(src, slot, v_ref, start, mask):
        v_t = v_ref[pl.ds(pl.multiple_of(start, tq), tq), :]
        bias = None if mask is None else bias_sc[mask]
        for h in range(hpg):
            fold_head(src, slot, h, v_t, bias)

    def finish(h):
        acc = acc_sc[h]
        return acc[:, :hd] / acc[:, hd:]

    q_head = lambda h: q[:, h * hd:(h + 1) * hd]
    s = lax.dot_general(qs, kc_ref[...], _NT, preferred_element_type=F32)
    st = lax.dot_general(kc_ref[...], qs, _NT, preferred_element_type=F32)
    put_scores(sw_sc, 0, kw_ref, t0, q_head)

    gt = jax.nn.sigmoid(pltpu.roll(zg_ref[...].astype(F32), (LANES - GATES_PER_GROUP * grp) & (LANES - 1), axis=1))

    for c in range(GATES_PER_GROUP):
        gate_sc[:, c * LANES:(c + 1) * LANES] = jnp.broadcast_to(gt[:, c:c + 1], (tq, LANES))

    def gate(h, branch):
        c = 3 * h + branch
        return gate_sc[:, c * LANES:(c + 1) * LANES]

    cmp_end = lax.broadcasted_iota(jnp.int32, (1, n_cmp), 1) * CMP_STRIDE + (CMP_LEN - 1)
    s = jnp.where(cmp_end <= qpos, s, -jnp.inf)
    mx = jnp.max(s, axis=-1, keepdims=True)
    mx = jnp.where(jnp.isfinite(mx), mx, 0.0)
    pe_cmp = jnp.exp2(s - mx).astype(BF16)

    end_t = lax.broadcasted_iota(jnp.int32, (n_cmp, 1), 0) * CMP_STRIDE + (CMP_LEN - 1)
    qpos_t = t0 + (lax.broadcasted_iota(jnp.int32, (1, hpg * tq), 1) & (tq - 1))
    st = jnp.where(end_t <= qpos_t, st, -jnp.inf)
    mt = jnp.max(st, axis=0, keepdims=True)
    mt = jnp.where(jnp.isfinite(mt), mt, 0.0)
    pt = jnp.exp2(st - mt)
    pt = pt / jnp.maximum(jnp.sum(pt, axis=0, keepdims=True), 1e-30)
    p_heads = jnp.concatenate([pt[:, h * tq:(h + 1) * tq] for h in range(hpg)], axis=0).astype(BF16)
    imp = _dot(ov_sc[...], p_heads)
    acc = _dot(pe_cmp, vcaug_sc[...])
    put_scores(sw_sc, 1, kw_ref, jnp.maximum(t0 - tq, 0), q_head)

    blk = lax.broadcasted_iota(jnp.int32, (n_blk, tq), 0)
    cur = lax.shift_right_logical(t0 + lax.broadcasted_iota(jnp.int32, (n_blk, tq), 1), 6)
    valid = blk <= cur
    forced = (blk == 0) | (valid & (blk > cur - SLC_LOCAL))
    score = jnp.where(forced, imp + FORCE_BONUS, jnp.where(valid, imp, -1.0))
    key = lax.bitcast_convert_type(score, jnp.int32)
    rank = jnp.zeros((n_blk, tq), jnp.int32)
    for r in range(1, n_blk):
        other = pltpu.roll(key, r, axis=0)
        rank = rank + jnp.where(other + jnp.where(blk >= r, 1, 0) > key, 1, 0)
    sel_bias = jnp.where((rank < SLC_TOP_N) & valid, 0.0, MASK_VALUE)
    sel_rows = jnp.concatenate([sel_bias, jnp.zeros((LANES - n_blk, tq), F32)], axis=0).T.astype(BF16)
    for h in range(hpg):
        qa_sc[h] = jnp.concatenate([q_head(h), sel_rows], axis=1)
    o_cmp = acc[:, :hd] / jnp.maximum(acc[:, hd:], 1e-30)
    for h in range(hpg):
        out_sc[:, h * hd:(h + 1) * hd] = gate(h, 0) * o_cmp[h * tq:(h + 1) * tq]

    assert WINDOW == tq
    reset()
    k_first = kaug_sc[0:tq, :]
    v_diag = vwaug_sc[pl.ds(pl.multiple_of(t0, tq), tq), :]
    for h in range(hpg):
        s_sc[0, h] = lax.dot_general(qa_sc[h], k_first, _NT, preferred_element_type=F32)
        fold_head(sw_sc, 0, h, v_diag, bias_sc[0])

    @pl.when(qi >= 1)
    def _():
        fold(sw_sc, 1, vwaug_sc, t0 - tq, 1)

    for h in range(hpg):
        out_sc[:, h * hd:(h + 1) * hd] += gate(h, 2) * finish(h)

    reset()

    def score_and_fold(put_slot, put_start, fold_slot, fold_start, mask):
        k_t = kaug_sc[pl.ds(pl.multiple_of(put_start, tq), tq), :]
        v_t = vsaug_sc[pl.ds(pl.multiple_of(fold_start, tq), tq), :]
        bias = None if mask is None else bias_sc[mask]
        for h in range(hpg):
            s_sc[put_slot, h] = lax.dot_general(qa_sc[h], k_t, _NT, preferred_element_type=F32)
            fold_head(s_sc, fold_slot, h, v_t, bias)

    def slc_pair(j, carry):
        e = 2 * j * tq
        score_and_fold(1, e + tq, 0, e, None)
        score_and_fold(0, e + 2 * tq, 1, e + tq, None)
        return carry

    lax.fori_loop(0, qi // 2, slc_pair, 0)

    @pl.when(qi % 2 == 1)
    def _():
        score_and_fold(1, t0, 0, t0 - tq, None)
        fold(s_sc, 1, vsaug_sc, t0, 0)

    @pl.when(qi % 2 == 0)
    def _():
        fold(s_sc, 0, vsaug_sc, t0, 0)

    for h in range(hpg):
        o_h = out_sc[:, h * hd:(h + 1) * hd] + gate(h, 1) * finish(h)
        o_ref[:, h * hd:(h + 1) * hd] = o_h.astype(o_ref.dtype)


def _nsa(z, kv_cmp, gates_col, batch, seq, tq=WINDOW):
    m = z.shape[0]
    assert gates_col % LANES == 0
    g = NSA_KV_GROUPS
    hd = HEAD_DIM
    hpg = HEADS_PER_GROUP
    n_cmp = kv_cmp.shape[3]
    nq = seq // tq
    qcols = NSA_WIDTH // hd
    assert (tq // 4) % SLC_LEN == 0 and n_cmp == LANES

    def kv_spec(slot):
        return pl.BlockSpec((seq, hd), lambda b, gi, i: (b, qcols + slot * g + gi))

    def cmp_spec(slot):
        return pl.BlockSpec((None, None, None, n_cmp, hd), lambda b, gi, i: (b, slot, gi, 0, 0))

    return pl.pallas_call(
        functools.partial(_nsa_kernel, tq=tq),
        out_shape=jax.ShapeDtypeStruct((m, NSA_WIDTH), BF16),
        grid=(batch, g, nq),
        in_specs=[
            pl.BlockSpec((tq, hpg * hd), lambda b, gi, i: (b * nq + i, gi)),
            kv_spec(2), kv_spec(3), kv_spec(4), kv_spec(5),
            cmp_spec(0), cmp_spec(1),
            pl.BlockSpec((tq, LANES), lambda b, gi, i: (b * nq + i, gates_col // LANES)),
        ],
        out_specs=pl.BlockSpec((tq, hpg * hd), lambda b, gi, i: (b * nq + i, gi)),
        scratch_shapes=[
            pltpu.VMEM((seq, 2 * hd), BF16),
            pltpu.VMEM((seq, 2 * hd), BF16),
            pltpu.VMEM((seq, 2 * hd), BF16),
            pltpu.VMEM((n_cmp, 2 * hd), BF16),
            pltpu.VMEM((LANES // 4, hpg * n_cmp), BF16),
            pltpu.VMEM((2, tq, tq), F32),
            pltpu.VMEM((hpg, tq, 2 * hd), BF16),
            pltpu.VMEM((tq, GATES_PER_GROUP * LANES), F32),
            pltpu.VMEM((2, hpg, tq, tq), F32),
            pltpu.VMEM((2, hpg, tq, tq), F32),
            pltpu.VMEM((hpg, tq, LANES), F32),
            pltpu.VMEM((hpg, tq, 2 * hd), F32),
            pltpu.VMEM((tq, hpg * hd), F32),
        ],
        compiler_params=_params("parallel", "parallel", "arbitrary"),
        name="nsa",
    )(z, z, z, z, z, kv_cmp, kv_cmp, z)


def _lru_kernel(lx_ref, ly_ref, cw_ref, cb_ref, wa_ref, ba_ref, wi_ref, bi_ref, lam_ref, o_ref, x_sc, a_sc, h_sc):
    seq, wt = lx_ref.shape
    x_sc[0:SUBLANES, :] = jnp.zeros((SUBLANES, wt), F32)
    x_sc[SUBLANES:, :] = lx_ref[...].astype(F32)
    taps = cw_ref.shape[0]
    u = cb_ref[...]
    for j in range(taps):
        back = taps - 1 - j
        u = u + x_sc[SUBLANES - back:SUBLANES - back + seq, :] * cw_ref[j:j + 1, :]
    ub = u.astype(BF16)
    r = jax.nn.sigmoid(_dot(ub, wa_ref[0]) + ba_ref[...])
    gate_i = jax.nn.sigmoid(_dot(ub, wi_ref[0]) + bi_ref[...])
    log_a = (LRU_C * r) * jax.nn.log_sigmoid(lam_ref[...])
    a = jnp.exp(log_a)
    y = 1.0 - a * a
    b = jnp.where(y > 0.0, y * lax.rsqrt(y), 0.0) * (gate_i * u)

    n_seg = SUBLANES
    seg = seq // n_seg
    stride = a_sc.shape[0] // n_seg
    for i in range(n_seg):
        a_sc[stride * i:stride * i + seg, :] = a[seg * i:seg * (i + 1), :]
        h_sc[stride * i:stride * i + seg, :] = b[seg * i:seg * (i + 1), :]

    def steps(jj, carry):
        h, p = carry
        for k in range(LRU_SCAN_UNROLL):
            rows = pl.ds(jj * LRU_SCAN_UNROLL + k, n_seg, stride=stride)
            a_j = a_sc[rows, :]
            h = a_j * h + h_sc[rows, :]
            p = a_j * p
            h_sc[rows, :] = h
            a_sc[rows, :] = p
        return h, p

    h_end, p_end = lax.fori_loop(0, seg // LRU_SCAN_UNROLL, steps,
                                 (jnp.zeros((n_seg, wt), F32), jnp.ones((n_seg, wt), F32)))
    enter = jnp.zeros((1, wt), F32)
    for i in range(n_seg):
        h_i = h_sc[stride * i:stride * i + seg, :] + a_sc[stride * i:stride * i + seg, :] * enter
        gated = jax.nn.gelu(ly_ref[seg * i:seg * (i + 1), :].astype(F32)) * h_i
        o_ref[seg * i:seg * (i + 1), :] = gated.astype(o_ref.dtype)
        enter = h_end[i:i + 1, :] + p_end[i:i + 1, :] * enter


def _lru(z, lx_col, ly_col, conv_w, conv_b, wa, ba, wi, bi, lam, batch, seq):
    m = z.shape[0]
    wt = LRU_BLOCK_DIM
    nw = LRU_WIDTH // wt
    assert lx_col % wt == 0 and ly_col % wt == 0
    taps = conv_w.shape[0]
    vec = pl.BlockSpec((1, wt), lambda b, j: (0, j))
    blk = pl.BlockSpec((1, wt, wt), lambda b, j: (j, 0, 0))
    return pl.pallas_call(
        _lru_kernel,
        out_shape=jax.ShapeDtypeStruct((m, LRU_WIDTH), BF16),
        grid=(batch, nw),
        in_specs=[
            pl.BlockSpec((seq, wt), lambda b, j: (b, lx_col // wt + j)),
            pl.BlockSpec((seq, wt), lambda b, j: (b, ly_col // wt + j)),
            pl.BlockSpec((taps, wt), lambda b, j: (0, j)),
            vec, blk, vec, blk, vec, vec,
        ],
        out_specs=pl.BlockSpec((seq, wt), lambda b, j: (b, j)),
        scratch_shapes=[pltpu.VMEM((SUBLANES + seq, wt), F32),
                        pltpu.VMEM((seq + SUBLANES * LRU_SEG_PAD, wt), F32),
                        pltpu.VMEM((seq + SUBLANES * LRU_SEG_PAD, wt), F32)],
        compiler_params=_params("parallel", "parallel"),
        name="lru",
    )(z, z, conv_w, conv_b, wa, ba, wi, bi, lam)


def _merge_kernel(oa_ref, ob_ref, pa_ref, pb_ref, ga_ref, gb_ref, o_ref):
    a = _dot(oa_ref[...], pa_ref[...])
    b = _dot(ob_ref[...], pb_ref[...])
    merged = jax.nn.sigmoid(ga_ref[...].astype(F32)) * a + jax.nn.sigmoid(gb_ref[...].astype(F32)) * b
    o_ref[...] = merged.astype(o_ref.dtype)


def _merge(o_a, o_b, proj_a, proj_b, layer, z, ga_col, gb_col, d_model, tm_target=1024, tn_target=512):
    m, ka = o_a.shape
    kb = o_b.shape[1]
    tm = _tile(m, tm_target, BF16_ROWS)
    tn = _tile(d_model, tn_target)
    ga0 = ga_col // tn
    gb0 = gb_col // tn
    assert ga_col % tn == 0 and gb_col % tn == 0
    return pl.pallas_call(
        _merge_kernel,
        out_shape=jax.ShapeDtypeStruct((m, d_model), BF16),
        grid=(m // tm, d_model // tn),
        in_specs=[
            pl.BlockSpec((tm, ka), lambda i, j: (i, 0)),
            pl.BlockSpec((tm, kb), lambda i, j: (i, 0)),
            pl.BlockSpec((None, ka, tn), lambda i, j: (layer, 0, j)),
            pl.BlockSpec((None, kb, tn), lambda i, j: (layer, 0, j)),
            pl.BlockSpec((tm, tn), lambda i, j: (i, ga0 + j)),
            pl.BlockSpec((tm, tn), lambda i, j: (i, gb0 + j)),
        ],
        out_specs=pl.BlockSpec((tm, tn), lambda i, j: (i, j)),
        compiler_params=_params("parallel", "parallel"),
        name="merge",
    )(o_a, o_b, proj_a, proj_b, z, z)


def _matmul_residual_kernel(a_ref, w_ref, x_ref, g_ref, o_ref):
    o_ref[...] = x_ref[...] + g_ref[...] * _dot(a_ref[...], w_ref[...])


def _matmul_residual(a, w, layer, x, gate, seq, name, tm_target, tn_target):
    m, k = a.shape
    n = w.shape[-1]
    tm = _tile(seq, tm_target, BF16_ROWS)
    tn = _tile(n, tn_target)
    return pl.pallas_call(
        _matmul_residual_kernel,
        out_shape=jax.ShapeDtypeStruct((m, n), F32),
        grid=(n // tn, m // tm),
        in_specs=[
            pl.BlockSpec((tm, k), lambda j, i: (i, 0)),
            pl.BlockSpec((None, k, tn), lambda j, i: (layer, 0, j)),
            pl.BlockSpec((tm, tn), lambda j, i: (i, j)),
            pl.BlockSpec((None, 1, tn), lambda j, i: ((i * tm) // seq, 0, j)),
        ],
        out_specs=pl.BlockSpec((tm, tn), lambda j, i: (i, j)),
        compiler_params=_params("parallel", "parallel"),
        name=name,
    )(a, w, x, gate)


def _ffn_up_kernel(x_ref, xh_ref, g_ref, sc_ref, sh_ref, wg_ref, wv_ref, cw_ref, cb_ref, o_ref, h_ref, inv_ref, *, seq):
    halo = BF16_ROWS
    tm = o_ref.shape[0]

    @pl.when(pl.program_id(1) == 0)
    def _():
        _store_norm_mod(h_ref, 0, xh_ref, g_ref, sc_ref, sh_ref, inv_ref)
        _store_norm_mod(h_ref, halo, x_ref, g_ref, sc_ref, sh_ref, inv_ref)

    h = h_ref[...]
    gate = _dot(h, wg_ref[...])
    val = _dot(h[halo:], wv_ref[...])
    seq_start = (pl.program_id(0) * tm) % seq == 0
    row = lax.broadcasted_iota(jnp.int32, (tm, 1), 0)
    taps = cw_ref.shape[0]
    conv = cb_ref[...]
    for j in range(taps):
        back = taps - 1 - j
        gj = gate[halo - back:halo - back + tm]
        if back:
            gj = jnp.where(row >= jnp.where(seq_start, back, 0), gj, 0.0)
        conv = conv + gj * cw_ref[j:j + 1, :]
    o_ref[...] = (jax.nn.gelu(conv) * val).astype(o_ref.dtype)


def _ffn_up(x, g, sc, sh, w_up, layer, conv_w, conv_b, seq, tm_target=1024, tn_target=768):
    m, d = x.shape
    d_ff = w_up.shape[-1] // 2
    tm = _tile(seq, tm_target, BF16_ROWS)
    tn = _tile(d_ff, tn_target)
    nj = d_ff // tn
    halo = BF16_ROWS
    taps = conv_w.shape[0]
    bidx = lambda i, j: ((i * tm) // seq, 0, 0)
    return pl.pallas_call(
        functools.partial(_ffn_up_kernel, seq=seq),
        out_shape=jax.ShapeDtypeStruct((m, d_ff), BF16),
        grid=(m // tm, nj),
        in_specs=[
            pl.BlockSpec((tm, d), lambda i, j: (i, 0)),
            pl.BlockSpec((halo, d), lambda i, j: (jnp.maximum(i * (tm // halo) - 1, 0), 0)),
            pl.BlockSpec((1, d), lambda i, j: (0, 0)),
            pl.BlockSpec((None, 1, d), bidx),
            pl.BlockSpec((None, 1, d), bidx),
            pl.BlockSpec((None, d, tn), lambda i, j: (layer, 0, j)),
            pl.BlockSpec((None, d, tn), lambda i, j: (layer, 0, nj + j)),
            pl.BlockSpec((taps, tn), lambda i, j: (0, j)),
            pl.BlockSpec((1, tn), lambda i, j: (0, j)),
        ],
        out_specs=pl.BlockSpec((tm, tn), lambda i, j: (i, j)),
        scratch_shapes=[pltpu.VMEM((halo + tm, d), BF16), pltpu.VMEM((tm, LANES), F32)],
        compiler_params=_params("parallel", "arbitrary"),
        name="ffn_up",
    )(x, x, g, sc, sh, w_up, w_up, conv_w, conv_b)


def _final_norm_kernel(x_ref, g_ref, o_ref):
    x = x_ref[...]
    inv = lax.rsqrt(jnp.mean(x * x, axis=-1, keepdims=True) + NORM_EPS)
    o_ref[...] = (x * inv) * g_ref[...]


def _final_norm(x, g, tm_target=512):
    m, d = x.shape
    tm = _tile(m, tm_target, SUBLANES)
    return pl.pallas_call(
        _final_norm_kernel,
        out_shape=jax.ShapeDtypeStruct((m, d), F32),
        grid=(m // tm,),
        in_specs=[pl.BlockSpec((tm, d), lambda i: (i, 0)), pl.BlockSpec((1, d), lambda i: (0, 0))],
        out_specs=pl.BlockSpec((tm, d), lambda i: (i, 0)),
        compiler_params=_params("parallel"),
        name="final_norm",
    )(x, g)


def _compress_weights(pos, w1, w2):
    half = CMP_LEN // 2
    return (pos.reshape(2, half * HEAD_DIM),
            w1.reshape(2, half * HEAD_DIM, HEAD_DIM).astype(BF16),
            w2.astype(BF16))


def kernel(x, c, ada_w, ada_b, norm1_g, w_in, cmp_pos_k, cmp_w1_k, cmp_w2_k, cmp_pos_v, cmp_w1_v, cmp_w2_v,
           lru_conv_w, lru_conv_b, lru_wa, lru_ba, lru_wi, lru_bi, lru_lambda, proj_a, proj_b, w_out, norm2_g,
           ffn_up, ffn_conv_w, ffn_conv_b, ffn_down, final_g):
    batch, seq, d = x.shape
    depth = ada_w.shape[0]
    m = batch * seq
    assert seq % SLC_LEN == 0 and seq // SLC_LEN == LANES // 4 and CMP_STRIDE * LANES == seq
    n_attn = NSA_WIDTH + 6 * KV_WIDTH
    n_rest = 2 * LRU_WIDTH + 2 * d
    n_z = -(-(n_attn + n_rest + N_GATES) // (4 * LANES)) * (4 * LANES)
    z_cols = dict(lx=n_attn, ly=n_attn + LRU_WIDTH, ga=n_attn + 2 * LRU_WIDTH, gb=n_attn + 2 * LRU_WIDTH + d,
                  gates=n_attn + n_rest)

    mod = _ada(c, ada_w, ada_b)
    xf = x.reshape(m, d)
    q_scale = jnp.concatenate([jnp.full((1, NSA_WIDTH), HEAD_DIM ** -0.5 * LOG2_E, F32), jnp.ones((1, n_z - NSA_WIDTH), F32)],
                              axis=1)
    w_all = jnp.concatenate([w_in[:, :, :n_attn], w_in[:, :, n_attn + N_GATES:],
                             jnp.pad(w_in[:, :, n_attn:n_attn + N_GATES],
                                     ((0, 0), (0, 0), (0, n_z - n_attn - n_rest - N_GATES)))], axis=2).astype(BF16)
    proj_a, proj_b, w_out, ffn_up, ffn_down, lru_wa, lru_wi = [
        w.astype(BF16) for w in (proj_a, proj_b, w_out, ffn_up, ffn_down, lru_wa, lru_wi)]

    for l in range(depth):
        sh1, sc1, g1, sh2, sc2, g2 = [mod[l, :, k * d:(k + 1) * d].reshape(batch, 1, d) for k in range(6)]
        z = _in_proj(xf, norm1_g[l].reshape(1, d), sc1, sh1, w_all, l, q_scale, BF16, seq)

        pk, w1k, w2k = _compress_weights(cmp_pos_k[l], cmp_w1_k[l], cmp_w2_k[l])
        pv, w1v, w2v = _compress_weights(cmp_pos_v[l], cmp_w1_v[l], cmp_w2_v[l])
        kv_cmp = _compress(z, batch, seq, jnp.stack([pk, pv]), jnp.stack([w1k, w1v]), jnp.stack([w2k, w2v]))

        o_a = _nsa(z, kv_cmp, z_cols["gates"], batch, seq)
        o_b = _lru(z, z_cols["lx"], z_cols["ly"], lru_conv_w[l], lru_conv_b[l].reshape(1, -1), lru_wa[l],
                   lru_ba[l].reshape(1, -1), lru_wi[l], lru_bi[l].reshape(1, -1),
                   lru_lambda[l].reshape(1, -1), batch, seq)
        merged = _merge(o_a, o_b, proj_a, proj_b, l, z, z_cols["ga"], z_cols["gb"], d)
        xf = _matmul_residual(merged, w_out, l, xf, g1, seq, "out_proj", tm_target=512, tn_target=2048)

        act = _ffn_up(xf, norm2_g[l].reshape(1, d), sc2, sh2, ffn_up, l, ffn_conv_w[l],
                      ffn_conv_b[l].reshape(1, -1), seq)
        xf = _matmul_residual(act, ffn_down, l, xf, g2, seq, "ffn_down", tm_target=512, tn_target=1024)

    return _final_norm(xf, final_g.reshape(1, d)).reshape(batch, seq, d)
```

```python
import functools

import jax
import jax.numpy as jnp
from jax import lax
from jax.experimental import pallas as pl
from jax.experimental.pallas import tpu as pltpu

F32 = jnp.float32
BF16 = jnp.bfloat16

NSA_HEADS = 16
NSA_KV_GROUPS = 4
HEADS_PER_GROUP = NSA_HEADS // NSA_KV_GROUPS
HEAD_DIM = 128
NSA_WIDTH = NSA_HEADS * HEAD_DIM
KV_WIDTH = NSA_KV_GROUPS * HEAD_DIM
CMP_LEN = 32
CMP_STRIDE = 16
SLC_LEN = 64
SLC_TOP_N = 16
SLC_LOCAL = 2
FORCE_BONUS = 1e4
WINDOW = 512
LRU_WIDTH = 2048
LRU_BLOCK_DIM = 128
LRU_C = 8.0
NORM_EPS = 1e-6
LRU_SEG_PAD = 8
LRU_SCAN_UNROLL = 8
N_GATES = 3 * NSA_HEADS
GATES_PER_GROUP = 3 * HEADS_PER_GROUP

LANES = 128
SUBLANES = 8
BF16_ROWS = 16
VMEM_LIMIT_BYTES = 56 * 1024 * 1024

MASK_VALUE = -1e30
LOG2_E = 1.4426950408889634

_NT = (((1,), (1,)), ((), ()))


def _params(*sem):
    return pltpu.CompilerParams(dimension_semantics=sem, vmem_limit_bytes=VMEM_LIMIT_BYTES)


def _tile(n, target, quantum=LANES):
    best = None
    t = quantum
    while t <= min(n, target):
        if n % t == 0:
            best = t
        t += quantum
    assert best is not None, (n, target, quantum)
    return best


def _dot(a, b):
    return jnp.dot(a, b, preferred_element_type=F32)


def _store_norm_mod(h_ref, row0, x_ref, g_ref, sc_ref, sh_ref, inv_ref):
    rows, d = x_ref.shape
    x = x_ref[...]
    inv = lax.rsqrt(jnp.mean(x * x, axis=-1, keepdims=True) + NORM_EPS)
    inv_ref[0:rows, :] = jnp.broadcast_to(inv, (rows, LANES))
    gain = g_ref[...]
    scale = 1.0 + sc_ref[...]
    shift = sh_ref[...]

    def slab(i, carry):
        r0 = pl.multiple_of(i * BF16_ROWS, BF16_ROWS)
        inv_s = jnp.concatenate([inv_ref[pl.ds(r0, BF16_ROWS), :]] * (d // LANES), axis=1)
        h = ((x_ref[pl.ds(r0, BF16_ROWS), :] * inv_s) * gain) * scale + shift
        h_ref[pl.ds(row0 + r0, BF16_ROWS), :] = h.astype(BF16)
        return carry

    lax.fori_loop(0, rows // BF16_ROWS, slab, 0, unroll=4 if rows >= 4 * BF16_ROWS else 1)


def _ada_kernel(c_ref, w_ref, b_ref, o_ref):
    c = c_ref[...]
    act = (c * jax.nn.sigmoid(c)).astype(BF16)
    o_ref[...] = _dot(act, w_ref[...].astype(BF16)) + b_ref[...]


def _ada(c, ada_w, ada_b):
    depth, d, n = ada_w.shape
    b = c.shape[0]
    tn = _tile(n, 1024)
    return pl.pallas_call(
        _ada_kernel,
        out_shape=jax.ShapeDtypeStruct((depth, b, n), F32),
        grid=(depth, n // tn),
        in_specs=[
            pl.BlockSpec((b, d), lambda l, j: (0, 0)),
            pl.BlockSpec((None, d, tn), lambda l, j: (l, 0, j)),
            pl.BlockSpec((None, 1, tn), lambda l, j: (l, 0, j)),
        ],
        out_specs=pl.BlockSpec((None, b, tn), lambda l, j: (l, 0, j)),
        compiler_params=_params("parallel", "parallel"),
        name="ada",
    )(c, ada_w, ada_b.reshape(depth, 1, n))


def _in_proj_kernel(x_ref, g_ref, sc_ref, sh_ref, w_ref, cs_ref, o_ref, h_ref, inv_ref):
    @pl.when(pl.program_id(1) == 0)
    def _():
        _store_norm_mod(h_ref, 0, x_ref, g_ref, sc_ref, sh_ref, inv_ref)

    o_ref[...] = (_dot(h_ref[...], w_ref[...]) * cs_ref[...]).astype(o_ref.dtype)


def _in_proj(x, g, sc, sh, w, layer, colscale, out_dtype, seq, tm_target=1024, tn_target=1536):
    m, d = x.shape
    n = w.shape[-1]
    tm = _tile(seq, tm_target, BF16_ROWS)
    tn = _tile(n, tn_target)
    bidx = lambda i, j: ((i * tm) // seq, 0, 0)
    return pl.pallas_call(
        _in_proj_kernel,
        out_shape=jax.ShapeDtypeStruct((m, n), out_dtype),
        grid=(m // tm, n // tn),
        in_specs=[
            pl.BlockSpec((tm, d), lambda i, j: (i, 0)),
            pl.BlockSpec((1, d), lambda i, j: (0, 0)),
            pl.BlockSpec((None, 1, d), bidx),
            pl.BlockSpec((None, 1, d), bidx),
            pl.BlockSpec((None, d, tn), lambda i, j: (layer, 0, j)),
            pl.BlockSpec((1, tn), lambda i, j: (0, j)),
        ],
        out_specs=pl.BlockSpec((tm, tn), lambda i, j: (i, j)),
        scratch_shapes=[pltpu.VMEM((tm, d), BF16), pltpu.VMEM((tm, LANES), F32)],
        compiler_params=_params("parallel", "arbitrary"),
        name="in_proj",
    )(x, g, sc, sh, w, colscale)


def _compress_kernel(kv_ref, pos_ref, w1_ref, w2_ref, o_ref, tok_sc, c_sc):
    g, n, hd = o_ref.shape
    for gi in range(g):
        tok_sc[gi] = kv_ref[:, gi * hd:(gi + 1) * hd].astype(F32)
        for j in range(CMP_STRIDE):
            c_sc[gi * n:(gi + 1) * n, j * hd:(j + 1) * hd] = tok_sc[gi, pl.ds(j, n, stride=CMP_STRIDE), :]
    c = c_sc[...]
    lo = (c + pos_ref[0:1, :]).astype(BF16)
    hi = (c + pos_ref[1:2, :]).astype(BF16)
    first = _dot(lo, w1_ref[0])
    second = _dot(hi, w1_ref[1])
    hid = jnp.concatenate([first[gi * n:(gi + 1) * n] + pltpu.roll(second[gi * n:(gi + 1) * n], n - 1, axis=0)
                           for gi in range(g)], axis=0)
    out = _dot(jax.nn.gelu(hid).astype(BF16), w2_ref[...]).astype(o_ref.dtype)
    for gi in range(g):
        o_ref[gi] = out[gi * n:(gi + 1) * n]


def _compress(z, batch, seq, pos, w1, w2):
    g = NSA_KV_GROUPS
    hd = HEAD_DIM
    n = seq // CMP_STRIDE
    cw = CMP_STRIDE * hd
    kv0 = NSA_WIDTH // KV_WIDTH
    return pl.pallas_call(
        _compress_kernel,
        out_shape=jax.ShapeDtypeStruct((batch, 2, g, n, hd), BF16),
        grid=(batch, 2),
        in_specs=[
            pl.BlockSpec((seq, KV_WIDTH), lambda bi, s: (bi, kv0 + s)),
            pl.BlockSpec((None, 2, cw), lambda bi, s: (s, 0, 0)),
            pl.BlockSpec((None, 2, cw, hd), lambda bi, s: (s, 0, 0, 0)),
            pl.BlockSpec((None, hd, hd), lambda bi, s: (s, 0, 0)),
        ],
        out_specs=pl.BlockSpec((None, None, g, n, hd), lambda bi, s: (bi, s, 0, 0, 0)),
        scratch_shapes=[pltpu.VMEM((g, seq, hd), F32), pltpu.VMEM((g * n, cw), F32)],
        compiler_params=_params("parallel", "parallel"),
        name="compress",
    )(z, pos, w1, w2)


def _nsa_kernel(q_ref, ks_ref, vs_ref, kw_ref, vw_ref, kc_ref, vc_ref, zg_ref, o_ref,
                kaug_sc, vsaug_sc, vwaug_sc, vcaug_sc, ov_sc, bias_sc, qa_sc, gate_sc, s_sc, sw_sc, m_sc, acc_sc, out_sc,
                *, tq):
    hd = HEAD_DIM
    hpg = HEADS_PER_GROUP
    tk = tq
    n_blk = LANES // 4
    seq = ks_ref.shape[0]
    n_cmp = kc_ref.shape[0]
    grp = pl.program_id(1)
    qi = pl.program_id(2)
    t0 = qi * tq

    @pl.when(qi == 0)
    def _():
        kaug_sc[:, :hd] = ks_ref[...]
        key_blk = lax.shift_right_logical(lax.broadcasted_iota(jnp.int32, (seq, LANES), 0), 6)
        lane_blk = lax.broadcasted_iota(jnp.int32, (seq, LANES), 1) & (n_blk - 1)
        kaug_sc[:, hd:] = jnp.where(key_blk == lane_blk, 1.0, 0.0).astype(BF16)
        ones = jnp.ones((seq, LANES), BF16)
        vsaug_sc[:, :hd] = vs_ref[...]
        vsaug_sc[:, hd:] = ones
        vwaug_sc[:, :hd] = vw_ref[...]
        vwaug_sc[:, hd:] = ones
        vcaug_sc[:, :hd] = vc_ref[...]
        vcaug_sc[:, hd:] = jnp.ones((n_cmp, LANES), BF16)
        cstart = (lax.broadcasted_iota(jnp.int32, (n_blk, hpg * n_cmp), 1) & (n_cmp - 1)) * CMP_STRIDE
        bstart = lax.broadcasted_iota(jnp.int32, (n_blk, hpg * n_cmp), 0) * SLC_LEN
        ov_sc[...] = jnp.where((cstart < bstart + SLC_LEN) & (cstart + CMP_LEN > bstart), 1.0, 0.0).astype(BF16)
        r = lax.broadcasted_iota(jnp.int32, (tq, tk), 0)
        cidx = lax.broadcasted_iota(jnp.int32, (tq, tk), 1)
        bias_sc[0] = jnp.where(cidx <= r, 0.0, MASK_VALUE)
        bias_sc[1] = jnp.where(cidx > r, 0.0, MASK_VALUE)

    q = q_ref[...]
    qs = jnp.concatenate([q[:, h * hd:(h + 1) * hd] for h in range(hpg)], axis=0)
    qpos = t0 + (lax.broadcasted_iota(jnp.int32, (hpg * tq, 1), 0) & (tq - 1))

    def put_scores(dst, slot, k_ref, start, lhs):
        k_t = k_ref[pl.ds(pl.multiple_of(start, tq), tq), :]
        for h in range(hpg):
            dst[slot, h] = lax.dot_general(lhs(h), k_t, _NT, preferred_element_type=F32)

    def reset():
        m_sc[...] = jnp.full(m_sc.shape, MASK_VALUE, F32)
        acc_sc[...] = jnp.zeros(acc_sc.shape, F32)

    def fold_head(src, slot, h, v_t, bias):
        sc = src[slot, h]
        if bias is not None:
            sc = sc + bias
        m_prev = m_sc[h]
        m_new = jnp.maximum(m_prev, jnp.max(sc, axis=-1, keepdims=True))
        alpha = jnp.exp2(m_prev - m_new)
        pe = jnp.exp2(sc - jnp.concatenate([m_new] * (tq // LANES), axis=1))
        acc_sc[h] = jnp.concatenate([alpha, alpha], axis=1) * acc_sc[h] + _dot(pe.astype(BF16), v_t)
        m_sc[h] = m_new

    def fold(src, slot, v_ref, start, mask):
        v_t = v_ref[pl.ds(pl.multiple_of(start, tq), tq), :]
        bias = None if mask is None else bias_sc[mask]
        for h in range(hpg):
            fold_head(src, slot, h, v_t, bias)

    def finish(h):
        acc = acc_sc[h]
        return acc[:, :hd] / acc[:, hd:]

    q_head = lambda h: q[:, h * hd:(h + 1) * hd]
    s = lax.dot_general(qs, kc_ref[...], _NT, preferred_element_type=F32)
    st = lax.dot_general(kc_ref[...], qs, _NT, preferred_element_type=F32)
    put_scores(sw_sc, 0, kw_ref, t0, q_head)

    gt = jax.nn.sigmoid(pltpu.roll(zg_ref[...].astype(F32), (LANES - GATES_PER_GROUP * grp) & (LANES - 1), axis=1))

    for c in range(GATES_PER_GROUP):
        gate_sc[:, c * LANES:(c + 1) * LANES] = jnp.broadcast_to(gt[:, c:c + 1], (tq, LANES))

    def gate(h, branch):
        c = 3 * h + branch
        return gate_sc[:, c * LANES:(c + 1) * LANES]

    cmp_end = lax.broadcasted_iota(jnp.int32, (1, n_cmp), 1) * CMP_STRIDE + (CMP_LEN - 1)
    s = jnp.where(cmp_end <= qpos, s, -jnp.inf)
    mx = jnp.max(s, axis=-1, keepdims=True)
    mx = jnp.where(jnp.isfinite(mx), mx, 0.0)
    pe_cmp = jnp.exp2(s - mx).astype(BF16)

    end_t = lax.broadcasted_iota(jnp.int32, (n_cmp, 1), 0) * CMP_STRIDE + (CMP_LEN - 1)
    qpos_t = t0 + (lax.broadcasted_iota(jnp.int32, (1, hpg * tq), 1) & (tq - 1))
    st = jnp.where(end_t <= qpos_t, st, -jnp.inf)
    mt = jnp.max(st, axis=0, keepdims=True)
    mt = jnp.where(jnp.isfinite(mt), mt, 0.0)
    pt = jnp.exp2(st - mt)
    pt = pt / jnp.maximum(jnp.sum(pt, axis=0, keepdims=True), 1e-30)
    p_heads = jnp.concatenate([pt[:, h * tq:(h + 1) * tq] for h in range(hpg)], axis=0).astype(BF16)
    imp = _dot(ov_sc[...], p_heads)
    acc = _dot(pe_cmp, vcaug_sc[...])
    put_scores(sw_sc, 1, kw_ref, jnp.maximum(t0 - tq, 0), q_head)

    blk = lax.broadcasted_iota(jnp.int32, (n_blk, tq), 0)
    cur = lax.shift_right_logical(t0 + lax.broadcasted_iota(jnp.int32, (n_blk, tq), 1), 6)
    valid = blk <= cur
    forced = (blk == 0) | (valid & (blk > cur - SLC_LOCAL))
    score = jnp.where(forced, imp + FORCE_BONUS, jnp.where(valid, imp, -1.0))
    key = lax.bitcast_convert_type(score, jnp.int32)
    rank = jnp.zeros((n_blk, tq), jnp.int32)
    for r in range(1, n_blk):
        other = pltpu.roll(key, r, axis=0)
        rank = rank + jnp.where(other + jnp.where(blk >= r, 1, 0) > key, 1, 0)
    sel_bias = jnp.where((rank < SLC_TOP_N) & valid, 0.0, MASK_VALUE)
    sel_rows = jnp.concatenate([sel_bias, jnp.zeros((LANES - n_blk, tq), F32)], axis=0).T.astype(BF16)
    for h in range(hpg):
        qa_sc[h] = jnp.concatenate([q_head(h), sel_rows], axis=1)
    o_cmp = acc[:, :hd] / jnp.maximum(acc[:, hd:], 1e-30)
    for h in range(hpg):
        out_sc[:, h * hd:(h + 1) * hd] = gate(h, 0) * o_cmp[h * tq:(h + 1) * tq]

    assert WINDOW == tq
    reset()
    k_first = kaug_sc[0:tq, :]
    v_diag = vwaug_sc[pl.ds(pl.multiple_of(t0, tq), tq), :]
    for h in range(hpg):
        s_sc[0, h] = lax.dot_general(qa_sc[h], k_first, _NT, preferred_element_type=F32)
        fold_head(sw_sc, 0, h, v_diag, bias_sc[0])

    @pl.when(qi >= 1)
    def _():
        fold(sw_sc, 1, vwaug_sc, t0 - tq, 1)

    for h in range(hpg):
        out_sc[:, h * hd:(h + 1) * hd] += gate(h, 2) * finish(h)

    reset()

    def score_and_fold(put_slot, put_start, fold_slot, fold_start, mask):
        k_t = kaug_sc[pl.ds(pl.multiple_of(put_start, tq), tq), :]
        v_t = vsaug_sc[pl.ds(pl.multiple_of(fold_start, tq), tq), :]
        bias = None if mask is None else bias_sc[mask]
        for h in range(hpg):
            s_sc[put_slot, h] = lax.dot_general(qa_sc[h], k_t, _NT, preferred_element_type=F32)
            fold_head(s_sc, fold_slot, h, v_t, bias)

    def slc_pair(j, carry):
        e = 2 * j * tq
        score_and_fold(1, e + tq, 0, e, None)
        score_and_fold(0, e + 2 * tq, 1, e + tq, None)
        return carry

    lax.fori_loop(0, qi // 2, slc_pair, 0)

    @pl.when(qi % 2 == 1)
    def _():
        score_and_fold(1, t0, 0, t0 - tq, None)
        fold(s_sc, 1, vsaug_sc, t0, 0)

    @pl.when(qi % 2 == 0)
    def _():
        fold(s_sc, 0, vsaug_sc, t0, 0)

    for h in range(hpg):
        o_h = out_sc[:, h * hd:(h + 1) * hd] + gate(h, 1) * finish(h)
        o_ref[:, h * hd:(h + 1) * hd] = o_h.astype(o_ref.dtype)


def _nsa(z, kv_cmp, gates_col, batch, seq, tq=WINDOW):
    m = z.shape[0]
    assert gates_col % LANES == 0
    g = NSA_KV_GROUPS
    hd = HEAD_DIM
    hpg = HEADS_PER_GROUP
    n_cmp = kv_cmp.shape[3]
    nq = seq // tq
    qcols = NSA_WIDTH // hd
    assert (tq // 4) % SLC_LEN == 0 and n_cmp == LANES

    def kv_spec(slot):
        return pl.BlockSpec((seq, hd), lambda b, gi, i: (b, qcols + slot * g + gi))

    def cmp_spec(slot):
        return pl.BlockSpec((None, None, None, n_cmp, hd), lambda b, gi, i: (b, slot, gi, 0, 0))

    return pl.pallas_call(
        functools.partial(_nsa_kernel, tq=tq),
        out_shape=jax.ShapeDtypeStruct((m, NSA_WIDTH), BF16),
        grid=(batch, g, nq),
        in_specs=[
            pl.BlockSpec((tq, hpg * hd), lambda b, gi, i: (b * nq + i, gi)),
            kv_spec(2), kv_spec(3), kv_spec(4), kv_spec(5),
            cmp_spec(0), cmp_spec(1),
            pl.BlockSpec((tq, LANES), lambda b, gi, i: (b * nq + i, gates_col // LANES)),
        ],
        out_specs=pl.BlockSpec((tq, hpg * hd), lambda b, gi, i: (b * nq + i, gi)),
        scratch_shapes=[
            pltpu.VMEM((seq, 2 * hd), BF16),
            pltpu.VMEM((seq, 2 * hd), BF16),
            pltpu.VMEM((seq, 2 * hd), BF16),
            pltpu.VMEM((n_cmp, 2 * hd), BF16),
            pltpu.VMEM((LANES // 4, hpg * n_cmp), BF16),
            pltpu.VMEM((2, tq, tq), F32),
            pltpu.VMEM((hpg, tq, 2 * hd), BF16),
            pltpu.VMEM((tq, GATES_PER_GROUP * LANES), F32),
            pltpu.VMEM((2, hpg, tq, tq), F32),
            pltpu.VMEM((2, hpg, tq, tq), F32),
            pltpu.VMEM((hpg, tq, LANES), F32),
            pltpu.VMEM((hpg, tq, 2 * hd), F32),
            pltpu.VMEM((tq, hpg * hd), F32),
        ],
        compiler_params=_params("parallel", "parallel", "arbitrary"),
        name="nsa",
    )(z, z, z, z, z, kv_cmp, kv_cmp, z)


def _lru_kernel(lx_ref, ly_ref, cw_ref, cb_ref, wa_ref, ba_ref, wi_ref, bi_ref, lam_ref, o_ref, x_sc, a_sc, h_sc):
    seq, wt = lx_ref.shape
    x_sc[0:SUBLANES, :] = jnp.zeros((SUBLANES, wt), F32)
    x_sc[SUBLANES:, :] = lx_ref[...].astype(F32)
    taps = cw_ref.shape[0]
    u = cb_ref[...]
    for j in range(taps):
        back = taps - 1 - j
        u = u + x_sc[SUBLANES - back:SUBLANES - back + seq, :] * cw_ref[j:j + 1, :]
    ub = u.astype(BF16)
    r = jax.nn.sigmoid(_dot(ub, wa_ref[0]) + ba_ref[...])
    gate_i = jax.nn.sigmoid(_dot(ub, wi_ref[0]) + bi_ref[...])
    log_a = (LRU_C * r) * jax.nn.log_sigmoid(lam_ref[...])
    a = jnp.exp(log_a)
    y = 1.0 - a * a
    b = jnp.where(y > 0.0, y * lax.rsqrt(y), 0.0) * (gate_i * u)

    n_seg = SUBLANES
    seg = seq // n_seg
    stride = a_sc.shape[0] // n_seg
    for i in range(n_seg):
        a_sc[stride * i:stride * i + seg, :] = a[seg * i:seg * (i + 1), :]
        h_sc[stride * i:stride * i + seg, :] = b[seg * i:seg * (i + 1), :]

    def steps(jj, carry):
        h, p = carry
        for k in range(LRU_SCAN_UNROLL):
            rows = pl.ds(jj * LRU_SCAN_UNROLL + k, n_seg, stride=stride)
            a_j = a_sc[rows, :]
            h = a_j * h + h_sc[rows, :]
            p = a_j * p
            h_sc[rows, :] = h
            a_sc[rows, :] = p
        return h, p

    h_end, p_end = lax.fori_loop(0, seg // LRU_SCAN_UNROLL, steps,
                                 (jnp.zeros((n_seg, wt), F32), jnp.ones((n_seg, wt), F32)))
    enter = jnp.zeros((1, wt), F32)
    for i in range(n_seg):
        h_i = h_sc[stride * i:stride * i + seg, :] + a_sc[stride * i:stride * i + seg, :] * enter
        gated = jax.nn.gelu(ly_ref[seg * i:seg * (i + 1), :].astype(F32)) * h_i
        o_ref[seg * i:seg * (i + 1), :] = gated.astype(o_ref.dtype)
        enter = h_end[i:i + 1, :] + p_end[i:i + 1, :] * enter


def _lru(z, lx_col, ly_col, conv_w, conv_b, wa, ba, wi, bi, lam, batch, seq):
    m = z.shape[0]
    wt = LRU_BLOCK_DIM
    nw = LRU_WIDTH // wt
    assert lx_col % wt == 0 and ly_col % wt == 0
    taps = conv_w.shape[0]
    vec = pl.BlockSpec((1, wt), lambda b, j: (0, j))
    blk = pl.BlockSpec((1, wt, wt), lambda b, j: (j, 0, 0))
    return pl.pallas_call(
        _lru_kernel,
        out_shape=jax.ShapeDtypeStruct((m, LRU_WIDTH), BF16),
        grid=(batch, nw),
        in_specs=[
            pl.BlockSpec((seq, wt), lambda b, j: (b, lx_col // wt + j)),
            pl.BlockSpec((seq, wt), lambda b, j: (b, ly_col // wt + j)),
            pl.BlockSpec((taps, wt), lambda b, j: (0, j)),
            vec, blk, vec, blk, vec, vec,
        ],
        out_specs=pl.BlockSpec((seq, wt), lambda b, j: (b, j)),
        scratch_shapes=[pltpu.VMEM((SUBLANES + seq, wt), F32),
                        pltpu.VMEM((seq + SUBLANES * LRU_SEG_PAD, wt), F32),
                        pltpu.VMEM((seq + SUBLANES * LRU_SEG_PAD, wt), F32)],
        compiler_params=_params("parallel", "parallel"),
        name="lru",
    )(z, z, conv_w, conv_b, wa, ba, wi, bi, lam)


def _merge_kernel(oa_ref, ob_ref, pa_ref, pb_ref, ga_ref, gb_ref, o_ref):
    a = _dot(oa_ref[...], pa_ref[...])
    b = _dot(ob_ref[...], pb_ref[...])
    merged = jax.nn.sigmoid(ga_ref[...].astype(F32)) * a + jax.nn.sigmoid(gb_ref[...].astype(F32)) * b
    o_ref[...] = merged.astype(o_ref.dtype)


def _merge(o_a, o_b, proj_a, proj_b, layer, z, ga_col, gb_col, d_model, tm_target=1024, tn_target=512):
    m, ka = o_a.shape
    kb = o_b.shape[1]
    tm = _tile(m, tm_target, BF16_ROWS)
    tn = _tile(d_model, tn_target)
    ga0 = ga_col // tn
    gb0 = gb_col // tn
    assert ga_col % tn == 0 and gb_col % tn == 0
    return pl.pallas_call(
        _merge_kernel,
        out_shape=jax.ShapeDtypeStruct((m, d_model), BF16),
        grid=(m // tm, d_model // tn),
        in_specs=[
            pl.BlockSpec((tm, ka), lambda i, j: (i, 0)),
            pl.BlockSpec((tm, kb), lambda i, j: (i, 0)),
            pl.BlockSpec((None, ka, tn), lambda i, j: (layer, 0, j)),
            pl.BlockSpec((None, kb, tn), lambda i, j: (layer, 0, j)),
            pl.BlockSpec((tm, tn), lambda i, j: (i, ga0 + j)),
            pl.BlockSpec((tm, tn), lambda i, j: (i, gb0 + j)),
        ],
        out_specs=pl.BlockSpec((tm, tn), lambda i, j: (i, j)),
        compiler_params=_params("parallel", "parallel"),
        name="merge",
    )(o_a, o_b, proj_a, proj_b, z, z)


def _matmul_residual_kernel(a_ref, w_ref, x_ref, g_ref, o_ref):
    o_ref[...] = x_ref[...] + g_ref[...] * _dot(a_ref[...], w_ref[...])


def _matmul_residual(a, w, layer, x, gate, seq, name, tm_target, tn_target):
    m, k = a.shape
    n = w.shape[-1]
    tm = _tile(seq, tm_target, BF16_ROWS)
    tn = _tile(n, tn_target)
    return pl.pallas_call(
        _matmul_residual_kernel,
        out_shape=jax.ShapeDtypeStruct((m, n), F32),
        grid=(n // tn, m // tm),
        in_specs=[
            pl.BlockSpec((tm, k), lambda j, i: (i, 0)),
            pl.BlockSpec((None, k, tn), lambda j, i: (layer, 0, j)),
            pl.BlockSpec((tm, tn), lambda j, i: (i, j)),
            pl.BlockSpec((None, 1, tn), lambda j, i: ((i * tm) // seq, 0, j)),
        ],
        out_specs=pl.BlockSpec((tm, tn), lambda j, i: (i, j)),
        compiler_params=_params("parallel", "parallel"),
        name=name,
    )(a, w, x, gate)


def _ffn_up_kernel(x_ref, xh_ref, g_ref, sc_ref, sh_ref, wg_ref, wv_ref, cw_ref, cb_ref, o_ref, h_ref, inv_ref, *, seq):
    halo = BF16_ROWS
    tm = o_ref.shape[0]

    @pl.when(pl.program_id(1) == 0)
    def _():
        _store_norm_mod(h_ref, 0, xh_ref, g_ref, sc_ref, sh_ref, inv_ref)
        _store_norm_mod(h_ref, halo, x_ref, g_ref, sc_ref, sh_ref, inv_ref)

    h = h_ref[...]
    gate = _dot(h, wg_ref[...])
    val = _dot(h[halo:], wv_ref[...])
    seq_start = (pl.program_id(0) * tm) % seq == 0
    row = lax.broadcasted_iota(jnp.int32, (tm, 1), 0)
    taps = cw_ref.shape[0]
    conv = cb_ref[...]
    for j in range(taps):
        back = taps - 1 - j
        gj = gate[halo - back:halo - back + tm]
        if back:
            gj = jnp.where(row >= jnp.where(seq_start, back, 0), gj, 0.0)
        conv = conv + gj * cw_ref[j:j + 1, :]
    o_ref[...] = (jax.nn.gelu(conv) * val).astype(o_ref.dtype)


def _ffn_up(x, g, sc, sh, w_up, layer, conv_w, conv_b, seq, tm_target=1024, tn_target=768):
    m, d = x.shape
    d_ff = w_up.shape[-1] // 2
    tm = _tile(seq, tm_target, BF16_ROWS)
    tn = _tile(d_ff, tn_target)
    nj = d_ff // tn
    halo = BF16_ROWS
    taps = conv_w.shape[0]
    bidx = lambda i, j: ((i * tm) // seq, 0, 0)
    return pl.pallas_call(
        functools.partial(_ffn_up_kernel, seq=seq),
        out_shape=jax.ShapeDtypeStruct((m, d_ff), BF16),
        grid=(m // tm, nj),
        in_specs=[
            pl.BlockSpec((tm, d), lambda i, j: (i, 0)),
            pl.BlockSpec((halo, d), lambda i, j: (jnp.maximum(i * (tm // halo) - 1, 0), 0)),
            pl.BlockSpec((1, d), lambda i, j: (0, 0)),
            pl.BlockSpec((None, 1, d), bidx),
            pl.BlockSpec((None, 1, d), bidx),
            pl.BlockSpec((None, d, tn), lambda i, j: (layer, 0, j)),
            pl.BlockSpec((None, d, tn), lambda i, j: (layer, 0, nj + j)),
            pl.BlockSpec((taps, tn), lambda i, j: (0, j)),
            pl.BlockSpec((1, tn), lambda i, j: (0, j)),
        ],
        out_specs=pl.BlockSpec((tm, tn), lambda i, j: (i, j)),
        scratch_shapes=[pltpu.VMEM((halo + tm, d), BF16), pltpu.VMEM((tm, LANES), F32)],
        compiler_params=_params("parallel", "arbitrary"),
        name="ffn_up",
    )(x, x, g, sc, sh, w_up, w_up, conv_w, conv_b)


def _final_norm_kernel(x_ref, g_ref, o_ref):
    x = x_ref[...]
    inv = lax.rsqrt(jnp.mean(x * x, axis=-1, keepdims=True) + NORM_EPS)
    o_ref[...] = (x * inv) * g_ref[...]


def _final_norm(x, g, tm_target=512):
    m, d = x.shape
    tm = _tile(m, tm_target, SUBLANES)
    return pl.pallas_call(
        _final_norm_kernel,
        out_shape=jax.ShapeDtypeStruct((m, d), F32),
        grid=(m // tm,),
        in_specs=[pl.BlockSpec((tm, d), lambda i: (i, 0)), pl.BlockSpec((1, d), lambda i: (0, 0))],
        out_specs=pl.BlockSpec((tm, d), lambda i: (i, 0)),
        compiler_params=_params("parallel"),
        name="final_norm",
    )(x, g)


def _regroup_kernel(w_ref, o_ref, *, n_attn, n_gates):
    n_in = w_ref.shape[1]
    n_rest = n_in - n_attn - n_gates
    o_ref[:, 0:n_attn] = w_ref[:, 0:n_attn].astype(BF16)
    o_ref[:, n_attn:n_attn + n_rest] = w_ref[:, n_attn + n_gates:n_in].astype(BF16)
    o_ref[:, n_attn + n_rest:n_in] = w_ref[:, n_attn:n_attn + n_gates].astype(BF16)
    o_ref[:, n_in:] = jnp.zeros((o_ref.shape[0], o_ref.shape[1] - n_in), BF16)


def _regroup_in_proj(w_in, n_attn, n_gates, n_out, tr_target=256):
    depth, d, n_in = w_in.shape
    tr = _tile(d, tr_target, BF16_ROWS)
    return pl.pallas_call(
        functools.partial(_regroup_kernel, n_attn=n_attn, n_gates=n_gates),
        out_shape=jax.ShapeDtypeStruct((depth, d, n_out), BF16),
        grid=(depth, d // tr),
        in_specs=[pl.BlockSpec((None, tr, n_in), lambda l, i: (l, i, 0))],
        out_specs=pl.BlockSpec((None, tr, n_out), lambda l, i: (l, i, 0)),
        compiler_params=_params("parallel", "parallel"),
        name="regroup_w_in",
    )(w_in)


def _compress_weights(pos, w1, w2):
    half = CMP_LEN // 2
    return (pos.reshape(2, half * HEAD_DIM),
            w1.reshape(2, half * HEAD_DIM, HEAD_DIM).astype(BF16),
            w2.astype(BF16))


def kernel(x, c, ada_w, ada_b, norm1_g, w_in, cmp_pos_k, cmp_w1_k, cmp_w2_k, cmp_pos_v, cmp_w1_v, cmp_w2_v,
           lru_conv_w, lru_conv_b, lru_wa, lru_ba, lru_wi, lru_bi, lru_lambda, proj_a, proj_b, w_out, norm2_g,
           ffn_up, ffn_conv_w, ffn_conv_b, ffn_down, final_g):
    batch, seq, d = x.shape
    depth = ada_w.shape[0]
    m = batch * seq
    assert seq % SLC_LEN == 0 and seq // SLC_LEN == LANES // 4 and CMP_STRIDE * LANES == seq
    n_attn = NSA_WIDTH + 6 * KV_WIDTH
    n_rest = 2 * LRU_WIDTH + 2 * d
    n_z = -(-(n_attn + n_rest + N_GATES) // (4 * LANES)) * (4 * LANES)
    z_cols = dict(lx=n_attn, ly=n_attn + LRU_WIDTH, ga=n_attn + 2 * LRU_WIDTH, gb=n_attn + 2 * LRU_WIDTH + d,
                  gates=n_attn + n_rest)

    mod = _ada(c, ada_w, ada_b)
    xf = x.reshape(m, d)
    q_scale = jnp.concatenate([jnp.full((1, NSA_WIDTH), HEAD_DIM ** -0.5 * LOG2_E, F32), jnp.ones((1, n_z - NSA_WIDTH), F32)],
                              axis=1)
    w_all = _regroup_in_proj(w_in, n_attn, N_GATES, n_z)
    proj_a, proj_b, w_out, ffn_up, ffn_down, lru_wa, lru_wi = [
        w.astype(BF16) for w in (proj_a, proj_b, w_out, ffn_up, ffn_down, lru_wa, lru_wi)]

    for l in range(depth):
        sh1, sc1, g1, sh2, sc2, g2 = [mod[l, :, k * d:(k + 1) * d].reshape(batch, 1, d) for k in range(6)]
        z = _in_proj(xf, norm1_g[l].reshape(1, d), sc1, sh1, w_all, l, q_scale, BF16, seq)

        pk, w1k, w2k = _compress_weights(cmp_pos_k[l], cmp_w1_k[l], cmp_w2_k[l])
        pv, w1v, w2v = _compress_weights(cmp_pos_v[l], cmp_w1_v[l], cmp_w2_v[l])
        kv_cmp = _compress(z, batch, seq, jnp.stack([pk, pv]), jnp.stack([w1k, w1v]), jnp.stack([w2k, w2v]))

        o_a = _nsa(z, kv_cmp, z_cols["gates"], batch, seq)
        o_b = _lru(z, z_cols["lx"], z_cols["ly"], lru_conv_w[l], lru_conv_b[l].reshape(1, -1), lru_wa[l],
                   lru_ba[l].reshape(1, -1), lru_wi[l], lru_bi[l].reshape(1, -1),
                   lru_lambda[l].reshape(1, -1), batch, seq)
        merged = _merge(o_a, o_b, proj_a, proj_b, l, z, z_cols["ga"], z_cols["gb"], d)
        xf = _matmul_residual(merged, w_out, l, xf, g1, seq, "out_proj", tm_target=512, tn_target=2048)

        act = _ffn_up(xf, norm2_g[l].reshape(1, d), sc2, sh2, ffn_up, l, ffn_conv_w[l],
                      ffn_conv_b[l].reshape(1, -1), seq)
        xf = _matmul_residual(act, ffn_down, l, xf, g2, seq, "ffn_down", tm_target=512, tn_target=1024)

    return _final_norm(xf, final_g.reshape(1, d)).reshape(batch, seq, d)
```

```python
import functools

import jax
import jax.numpy as jnp
from jax import lax
from jax.experimental import pallas as pl
from jax.experimental.pallas import tpu as pltpu

F32 = jnp.float32
BF16 = jnp.bfloat16

NSA_HEADS = 16
NSA_KV_GROUPS = 4
HEADS_PER_GROUP = NSA_HEADS // NSA_KV_GROUPS
HEAD_DIM = 128
NSA_WIDTH = NSA_HEADS * HEAD_DIM
KV_WIDTH = NSA_KV_GROUPS * HEAD_DIM
CMP_LEN = 32
CMP_STRIDE = 16
SLC_LEN = 64
SLC_TOP_N = 16
SLC_LOCAL = 2
FORCE_BONUS = 1e4
WINDOW = 512
LRU_WIDTH = 2048
LRU_BLOCK_DIM = 128
LRU_C = 8.0
NORM_EPS = 1e-6
LRU_SEG_PAD = 8
LRU_SCAN_UNROLL = 8
N_GATES = 3 * NSA_HEADS
GATES_PER_GROUP = 3 * HEADS_PER_GROUP

LANES = 128
SUBLANES = 8
BF16_ROWS = 16
VMEM_LIMIT_BYTES = 56 * 1024 * 1024

MASK_VALUE = -1e30
LOG2_E = 1.4426950408889634

_NT = (((1,), (1,)), ((), ()))


def _params(*sem):
    return pltpu.CompilerParams(dimension_semantics=sem, vmem_limit_bytes=VMEM_LIMIT_BYTES)


def _tile(n, target, quantum=LANES):
    best = None
    t = quantum
    while t <= min(n, target):
        if n % t == 0:
            best = t
        t += quantum
    assert best is not None, (n, target, quantum)
    return best


def _dot(a, b):
    return jnp.dot(a, b, preferred_element_type=F32)


def _store_norm_mod(h_ref, row0, x_ref, g_ref, sc_ref, sh_ref, inv_ref):
    rows, d = x_ref.shape
    x = x_ref[...]
    inv = lax.rsqrt(jnp.mean(x * x, axis=-1, keepdims=True) + NORM_EPS)
    inv_ref[0:rows, :] = jnp.broadcast_to(inv, (rows, LANES))
    gain = g_ref[...]
    scale = 1.0 + sc_ref[...]
    shift = sh_ref[...]

    def slab(i, carry):
        r0 = pl.multiple_of(i * BF16_ROWS, BF16_ROWS)
        inv_s = jnp.concatenate([inv_ref[pl.ds(r0, BF16_ROWS), :]] * (d // LANES), axis=1)
        h = ((x_ref[pl.ds(r0, BF16_ROWS), :] * inv_s) * gain) * scale + shift
        h_ref[pl.ds(row0 + r0, BF16_ROWS), :] = h.astype(BF16)
        return carry

    lax.fori_loop(0, rows // BF16_ROWS, slab, 0, unroll=4 if rows >= 4 * BF16_ROWS else 1)


def _norm_mod_rows(x, g_ref, sc_ref, sh_ref):
    inv = lax.rsqrt(jnp.mean(x * x, axis=-1, keepdims=True) + NORM_EPS)
    return (((x * inv) * g_ref[...]) * (1.0 + sc_ref[...]) + sh_ref[...]).astype(BF16)


def _ahead_rows(tm, n_steps):
    return min(tm, -(-tm // ((n_steps - 1) * BF16_ROWS)) * BF16_ROWS)


def _ahead_tile(i, j, n_tiles):
    return jnp.where((i == 0) & (j == 0), 0, jnp.minimum(i + 1, n_tiles - 1))


def _ada_kernel(c_ref, w_ref, b_ref, o_ref):
    c = c_ref[...]
    act = (c * jax.nn.sigmoid(c)).astype(BF16)
    o_ref[...] = _dot(act, w_ref[...].astype(BF16)) + b_ref[...]


def _ada(c, ada_w, ada_b):
    depth, d, n = ada_w.shape
    b = c.shape[0]
    tn = _tile(n, 1024)
    return pl.pallas_call(
        _ada_kernel,
        out_shape=jax.ShapeDtypeStruct((depth, b, n), F32),
        grid=(depth, n // tn),
        in_specs=[
            pl.BlockSpec((b, d), lambda l, j: (0, 0)),
            pl.BlockSpec((None, d, tn), lambda l, j: (l, 0, j)),
            pl.BlockSpec((None, 1, tn), lambda l, j: (l, 0, j)),
        ],
        out_specs=pl.BlockSpec((None, b, tn), lambda l, j: (l, 0, j)),
        compiler_params=_params("parallel", "parallel"),
        name="ada",
    )(c, ada_w, ada_b.reshape(depth, 1, n))


def _in_proj_kernel(xn_ref, g_ref, scn_ref, shn_ref, w_ref, cs_ref, o_ref, h_ref, inv_ref, *, ahead):
    i = pl.program_id(0)
    j = pl.program_id(1)
    tm = o_ref.shape[0]
    slot = i % 2

    @pl.when((i == 0) & (j == 0))
    def _():
        _store_norm_mod(h_ref.at[0], 0, xn_ref, g_ref, scn_ref, shn_ref, inv_ref)

    o_ref[...] = (_dot(h_ref[slot], w_ref[...]) * cs_ref[...]).astype(o_ref.dtype)
    r0 = pl.multiple_of(jnp.clip((j - 1) * ahead, 0, tm - ahead), BF16_ROWS)
    h_ref[1 - slot, pl.ds(r0, ahead), :] = _norm_mod_rows(xn_ref[pl.ds(r0, ahead), :], g_ref, scn_ref, shn_ref)


def _in_proj(x, g, sc, sh, w, layer, colscale, out_dtype, seq, tm_target=1024, tn_target=1536):
    m, d = x.shape
    n = w.shape[-1]
    tm = _tile(seq, tm_target, BF16_ROWS)
    tn = _tile(n, min(tn_target, n // 2))
    n_i, n_j = m // tm, n // tn
    ahead_x = lambda i, j: (_ahead_tile(i, j, n_i), 0)
    ahead_b = lambda i, j: ((_ahead_tile(i, j, n_i) * tm) // seq, 0, 0)
    return pl.pallas_call(
        functools.partial(_in_proj_kernel, ahead=_ahead_rows(tm, n_j)),
        out_shape=jax.ShapeDtypeStruct((m, n), out_dtype),
        grid=(n_i, n_j),
        in_specs=[
            pl.BlockSpec((tm, d), ahead_x),
            pl.BlockSpec((1, d), lambda i, j: (0, 0)),
            pl.BlockSpec((None, 1, d), ahead_b),
            pl.BlockSpec((None, 1, d), ahead_b),
            pl.BlockSpec((None, d, tn), lambda i, j: (layer, 0, j)),
            pl.BlockSpec((1, tn), lambda i, j: (0, j)),
        ],
        out_specs=pl.BlockSpec((tm, tn), lambda i, j: (i, j)),
        scratch_shapes=[pltpu.VMEM((2, tm, d), BF16), pltpu.VMEM((tm, LANES), F32)],
        compiler_params=_params("arbitrary", "arbitrary"),
        name="in_proj",
    )(x, g, sc, sh, w, colscale)


def _compress_kernel(kv_ref, pos_ref, w1_ref, w2_ref, o_ref, tok_sc, c_sc):
    g, n, hd = o_ref.shape
    for gi in range(g):
        tok_sc[gi] = kv_ref[:, gi * hd:(gi + 1) * hd].astype(F32)
        for j in range(CMP_STRIDE):
            c_sc[gi * n:(gi + 1) * n, j * hd:(j + 1) * hd] = tok_sc[gi, pl.ds(j, n, stride=CMP_STRIDE), :]
    c = c_sc[...]
    lo = (c + pos_ref[0:1, :]).astype(BF16)
    hi = (c + pos_ref[1:2, :]).astype(BF16)
    first = _dot(lo, w1_ref[0])
    second = _dot(hi, w1_ref[1])
    hid = jnp.concatenate([first[gi * n:(gi + 1) * n] + pltpu.roll(second[gi * n:(gi + 1) * n], n - 1, axis=0)
                           for gi in range(g)], axis=0)
    out = _dot(jax.nn.gelu(hid).astype(BF16), w2_ref[...]).astype(o_ref.dtype)
    for gi in range(g):
        o_ref[gi] = out[gi * n:(gi + 1) * n]


def _compress(z, batch, seq, pos, w1, w2):
    g = NSA_KV_GROUPS
    hd = HEAD_DIM
    n = seq // CMP_STRIDE
    cw = CMP_STRIDE * hd
    kv0 = NSA_WIDTH // KV_WIDTH
    return pl.pallas_call(
        _compress_kernel,
        out_shape=jax.ShapeDtypeStruct((batch, 2, g, n, hd), BF16),
        grid=(batch, 2),
        in_specs=[
            pl.BlockSpec((seq, KV_WIDTH), lambda bi, s: (bi, kv0 + s)),
            pl.BlockSpec((None, 2, cw), lambda bi, s: (s, 0, 0)),
            pl.BlockSpec((None, 2, cw, hd), lambda bi, s: (s, 0, 0, 0)),
            pl.BlockSpec((None, hd, hd), lambda bi, s: (s, 0, 0)),
        ],
        out_specs=pl.BlockSpec((None, None, g, n, hd), lambda bi, s: (bi, s, 0, 0, 0)),
        scratch_shapes=[pltpu.VMEM((g, seq, hd), F32), pltpu.VMEM((g * n, cw), F32)],
        compiler_params=_params("parallel", "parallel"),
        name="compress",
    )(z, pos, w1, w2)


def _nsa_kernel(q_ref, ks_ref, vs_ref, kw_ref, vw_ref, kc_ref, vc_ref, zg_ref, o_ref,
                kaug_sc, vsaug_sc, vwaug_sc, vcaug_sc, ov_sc, bias_sc, qa_sc, gate_sc, s_sc, sw_sc, m_sc, acc_sc, out_sc,
                *, tq):
    hd = HEAD_DIM
    hpg = HEADS_PER_GROUP
    tk = tq
    n_blk = LANES // 4
    seq = ks_ref.shape[0]
    n_cmp = kc_ref.shape[0]
    grp = pl.program_id(1)
    qi = pl.program_id(2)
    t0 = qi * tq

    @pl.when(qi == 0)
    def _():
        kaug_sc[:, :hd] = ks_ref[...]
        key_blk = lax.shift_right_logical(lax.broadcasted_iota(jnp.int32, (seq, LANES), 0), 6)
        lane_blk = lax.broadcasted_iota(jnp.int32, (seq, LANES), 1) & (n_blk - 1)
        kaug_sc[:, hd:] = jnp.where(key_blk == lane_blk, 1.0, 0.0).astype(BF16)
        ones = jnp.ones((seq, LANES), BF16)
        vsaug_sc[:, :hd] = vs_ref[...]
        vsaug_sc[:, hd:] = ones
        vwaug_sc[:, :hd] = vw_ref[...]
        vwaug_sc[:, hd:] = ones
        vcaug_sc[:, :hd] = vc_ref[...]
        vcaug_sc[:, hd:] = jnp.ones((n_cmp, LANES), BF16)
        cstart = (lax.broadcasted_iota(jnp.int32, (n_blk, hpg * n_cmp), 1) & (n_cmp - 1)) * CMP_STRIDE
        bstart = lax.broadcasted_iota(jnp.int32, (n_blk, hpg * n_cmp), 0) * SLC_LEN
        ov_sc[...] = jnp.where((cstart < bstart + SLC_LEN) & (cstart + CMP_LEN > bstart), 1.0, 0.0).astype(BF16)
        r = lax.broadcasted_iota(jnp.int32, (tq, tk), 0)
        cidx = lax.broadcasted_iota(jnp.int32, (tq, tk), 1)
        bias_sc[0] = jnp.where(cidx <= r, 0.0, MASK_VALUE)
        bias_sc[1] = jnp.where(cidx > r, 0.0, MASK_VALUE)

    q = q_ref[...]
    qs = jnp.concatenate([q[:, h * hd:(h + 1) * hd] for h in range(hpg)], axis=0)
    qpos = t0 + (lax.broadcasted_iota(jnp.int32, (hpg * tq, 1), 0) & (tq - 1))

    def put_scores(dst, slot, k_ref, start, lhs):
        k_t = k_ref[pl.ds(pl.multiple_of(start, tq), tq), :]
        for h in range(hpg):
            dst[slot, h] = lax.dot_general(lhs(h), k_t, _NT, preferred_element_type=F32)

    def reset():
        m_sc[...] = jnp.full(m_sc.shape, MASK_VALUE, F32)
        acc_sc[...] = jnp.zeros(acc_sc.shape, F32)

    def fold_head(src, slot, h, v_t, bias):
        sc = src[slot, h]
        if bias is not None:
            sc = sc + bias
        m_prev = m_sc[h]
        m_new = jnp.maximum(m_prev, jnp.max(sc, axis=-1, keepdims=True))
        alpha = jnp.exp2(m_prev - m_new)
        pe = jnp.exp2(sc - jnp.concatenate([m_new] * (tq // LANES), axis=1))
        acc_sc[h] = jnp.concatenate([alpha, alpha], axis=1) * acc_sc[h] + _dot(pe.astype(BF16), v_t)
        m_sc[h] = m_new

    def fold(src, slot, v_ref, start, mask):
        v_t = v_ref[pl.ds(pl.multiple_of(start, tq), tq), :]
        bias = None if mask is None else bias_sc[mask]
        for h in range(hpg):
            fold_head(src, slot, h, v_t, bias)

    def finish(h):
        acc = acc_sc[h]
        return acc[:, :hd] / acc[:, hd:]

    q_head = lambda h: q[:, h * hd:(h + 1) * hd]
    s = lax.dot_general(qs, kc_ref[...], _NT, preferred_element_type=F32)
    st = lax.dot_general(kc_ref[...], qs, _NT, preferred_element_type=F32)
    put_scores(sw_sc, 0, kw_ref, t0, q_head)

    gt = jax.nn.sigmoid(pltpu.roll(zg_ref[...].astype(F32), (LANES - GATES_PER_GROUP * grp) & (LANES - 1), axis=1))

    for c in range(GATES_PER_GROUP):
        gate_sc[:, c * LANES:(c + 1) * LANES] = jnp.broadcast_to(gt[:, c:c + 1], (tq, LANES))

    def gate(h, branch):
        c = 3 * h + branch
        return gate_sc[:, c * LANES:(c + 1) * LANES]

    cmp_end = lax.broadcasted_iota(jnp.int32, (1, n_cmp), 1) * CMP_STRIDE + (CMP_LEN - 1)
    s = jnp.where(cmp_end <= qpos, s, -jnp.inf)
    mx = jnp.max(s, axis=-1, keepdims=True)
    mx = jnp.where(jnp.isfinite(mx), mx, 0.0)
    pe_cmp = jnp.exp2(s - mx).astype(BF16)

    end_t = lax.broadcasted_iota(jnp.int32, (n_cmp, 1), 0) * CMP_STRIDE + (CMP_LEN - 1)
    qpos_t = t0 + (lax.broadcasted_iota(jnp.int32, (1, hpg * tq), 1) & (tq - 1))
    st = jnp.where(end_t <= qpos_t, st, -jnp.inf)
    mt = jnp.max(st, axis=0, keepdims=True)
    mt = jnp.where(jnp.isfinite(mt), mt, 0.0)
    pt = jnp.exp2(st - mt)
    pt = pt / jnp.maximum(jnp.sum(pt, axis=0, keepdims=True), 1e-30)
    p_heads = jnp.concatenate([pt[:, h * tq:(h + 1) * tq] for h in range(hpg)], axis=0).astype(BF16)
    imp = _dot(ov_sc[...], p_heads)
    acc = _dot(pe_cmp, vcaug_sc[...])
    put_scores(sw_sc, 1, kw_ref, jnp.maximum(t0 - tq, 0), q_head)

    blk = lax.broadcasted_iota(jnp.int32, (n_blk, tq), 0)
    cur = lax.shift_right_logical(t0 + lax.broadcasted_iota(jnp.int32, (n_blk, tq), 1), 6)
    valid = blk <= cur
    forced = (blk == 0) | (valid & (blk > cur - SLC_LOCAL))
    score = jnp.where(forced, imp + FORCE_BONUS, jnp.where(valid, imp, -1.0))
    key = lax.bitcast_convert_type(score, jnp.int32)
    rank = jnp.zeros((n_blk, tq), jnp.int32)
    for r in range(1, n_blk):
        other = pltpu.roll(key, r, axis=0)
        rank = rank + jnp.where(other + jnp.where(blk >= r, 1, 0) > key, 1, 0)
    sel_bias = jnp.where((rank < SLC_TOP_N) & valid, 0.0, MASK_VALUE)
    sel_rows = jnp.concatenate([sel_bias, jnp.zeros((LANES - n_blk, tq), F32)], axis=0).T.astype(BF16)
    for h in range(hpg):
        qa_sc[h] = jnp.concatenate([q_head(h), sel_rows], axis=1)
    o_cmp = acc[:, :hd] / jnp.maximum(acc[:, hd:], 1e-30)
    for h in range(hpg):
        out_sc[:, h * hd:(h + 1) * hd] = gate(h, 0) * o_cmp[h * tq:(h + 1) * tq]

    assert WINDOW == tq
    reset()
    k_first = kaug_sc[0:tq, :]
    v_diag = vwaug_sc[pl.ds(pl.multiple_of(t0, tq), tq), :]
    for h in range(hpg):
        s_sc[0, h] = lax.dot_general(qa_sc[h], k_first, _NT, preferred_element_type=F32)
        fold_head(sw_sc, 0, h, v_diag, bias_sc[0])

    @pl.when(qi >= 1)
    def _():
        fold(sw_sc, 1, vwaug_sc, t0 - tq, 1)

    for h in range(hpg):
        out_sc[:, h * hd:(h + 1) * hd] += gate(h, 2) * finish(h)

    reset()

    def score_and_fold(put_slot, put_start, fold_slot, fold_start, mask):
        k_t = kaug_sc[pl.ds(pl.multiple_of(put_start, tq), tq), :]
        v_t = vsaug_sc[pl.ds(pl.multiple_of(fold_start, tq), tq), :]
        bias = None if mask is None else bias_sc[mask]
        for h in range(hpg):
            s_sc[put_slot, h] = lax.dot_general(qa_sc[h], k_t, _NT, preferred_element_type=F32)
            fold_head(s_sc, fold_slot, h, v_t, bias)

    def slc_pair(j, carry):
        e = 2 * j * tq
        score_and_fold(1, e + tq, 0, e, None)
        score_and_fold(0, e + 2 * tq, 1, e + tq, None)
        return carry

    lax.fori_loop(0, qi // 2, slc_pair, 0)

    @pl.when(qi % 2 == 1)
    def _():
        score_and_fold(1, t0, 0, t0 - tq, None)
        fold(s_sc, 1, vsaug_sc, t0, 0)

    @pl.when(qi % 2 == 0)
    def _():
        fold(s_sc, 0, vsaug_sc, t0, 0)

    for h in range(hpg):
        o_h = out_sc[:, h * hd:(h + 1) * hd] + gate(h, 1) * finish(h)
        o_ref[:, h * hd:(h + 1) * hd] = o_h.astype(o_ref.dtype)


def _nsa(z, kv_cmp, gates_col, batch, seq, tq=WINDOW):
    m = z.shape[0]
    assert gates_col % LANES == 0
    g = NSA_KV_GROUPS
    hd = HEAD_DIM
    hpg = HEADS_PER_GROUP
    n_cmp = kv_cmp.shape[3]
    nq = seq // tq
    qcols = NSA_WIDTH // hd
    assert (tq // 4) % SLC_LEN == 0 and n_cmp == LANES

    def kv_spec(slot):
        return pl.BlockSpec((seq, hd), lambda b, gi, i: (b, qcols + slot * g + gi))

    def cmp_spec(slot):
        return pl.BlockSpec((None, None, None, n_cmp, hd), lambda b, gi, i: (b, slot, gi, 0, 0))

    return pl.pallas_call(
        functools.partial(_nsa_kernel, tq=tq),
        out_shape=jax.ShapeDtypeStruct((m, NSA_WIDTH), BF16),
        grid=(batch, g, nq),
        in_specs=[
            pl.BlockSpec((tq, hpg * hd), lambda b, gi, i: (b * nq + i, gi)),
            kv_spec(2), kv_spec(3), kv_spec(4), kv_spec(5),
            cmp_spec(0), cmp_spec(1),
            pl.BlockSpec((tq, LANES), lambda b, gi, i: (b * nq + i, gates_col // LANES)),
        ],
        out_specs=pl.BlockSpec((tq, hpg * hd), lambda b, gi, i: (b * nq + i, gi)),
        scratch_shapes=[
            pltpu.VMEM((seq, 2 * hd), BF16),
            pltpu.VMEM((seq, 2 * hd), BF16),
            pltpu.VMEM((seq, 2 * hd), BF16),
            pltpu.VMEM((n_cmp, 2 * hd), BF16),
            pltpu.VMEM((LANES // 4, hpg * n_cmp), BF16),
            pltpu.VMEM((2, tq, tq), F32),
            pltpu.VMEM((hpg, tq, 2 * hd), BF16),
            pltpu.VMEM((tq, GATES_PER_GROUP * LANES), F32),
            pltpu.VMEM((2, hpg, tq, tq), F32),
            pltpu.VMEM((2, hpg, tq, tq), F32),
            pltpu.VMEM((hpg, tq, LANES), F32),
            pltpu.VMEM((hpg, tq, 2 * hd), F32),
            pltpu.VMEM((tq, hpg * hd), F32),
        ],
        compiler_params=_params("parallel", "parallel", "arbitrary"),
        name="nsa",
    )(z, z, z, z, z, kv_cmp, kv_cmp, z)


def _lru_kernel(lx_ref, ly_ref, cw_ref, cb_ref, wa_ref, ba_ref, wi_ref, bi_ref, lam_ref, o_ref, x_sc, a_sc, h_sc):
    seq, wt = lx_ref.shape
    x_sc[0:SUBLANES, :] = jnp.zeros((SUBLANES, wt), F32)
    x_sc[SUBLANES:, :] = lx_ref[...].astype(F32)
    taps = cw_ref.shape[0]
    u = cb_ref[...]
    for j in range(taps):
        back = taps - 1 - j
        u = u + x_sc[SUBLANES - back:SUBLANES - back + seq, :] * cw_ref[j:j + 1, :]
    ub = u.astype(BF16)
    r = jax.nn.sigmoid(_dot(ub, wa_ref[0]) + ba_ref[...])
    gate_i = jax.nn.sigmoid(_dot(ub, wi_ref[0]) + bi_ref[...])
    log_a = (LRU_C * r) * jax.nn.log_sigmoid(lam_ref[...])
    a = jnp.exp(log_a)
    y = 1.0 - a * a
    b = jnp.where(y > 0.0, y * lax.rsqrt(y), 0.0) * (gate_i * u)

    n_seg = SUBLANES
    seg = seq // n_seg
    stride = a_sc.shape[0] // n_seg
    for i in range(n_seg):
        a_sc[stride * i:stride * i + seg, :] = a[seg * i:seg * (i + 1), :]
        h_sc[stride * i:stride * i + seg, :] = b[seg * i:seg * (i + 1), :]

    def steps(jj, carry):
        h, p = carry
        for k in range(LRU_SCAN_UNROLL):
            rows = pl.ds(jj * LRU_SCAN_UNROLL + k, n_seg, stride=stride)
            a_j = a_sc[rows, :]
            h = a_j * h + h_sc[rows, :]
            p = a_j * p
            h_sc[rows, :] = h
            a_sc[rows, :] = p
        return h, p

    h_end, p_end = lax.fori_loop(0, seg // LRU_SCAN_UNROLL, steps,
                                 (jnp.zeros((n_seg, wt), F32), jnp.ones((n_seg, wt), F32)))
    enter = jnp.zeros((1, wt), F32)
    for i in range(n_seg):
        h_i = h_sc[stride * i:stride * i + seg, :] + a_sc[stride * i:stride * i + seg, :] * enter
        gated = jax.nn.gelu(ly_ref[seg * i:seg * (i + 1), :].astype(F32)) * h_i
        o_ref[seg * i:seg * (i + 1), :] = gated.astype(o_ref.dtype)
        enter = h_end[i:i + 1, :] + p_end[i:i + 1, :] * enter


def _lru(z, lx_col, ly_col, conv_w, conv_b, wa, ba, wi, bi, lam, batch, seq):
    m = z.shape[0]
    wt = LRU_BLOCK_DIM
    nw = LRU_WIDTH // wt
    assert lx_col % wt == 0 and ly_col % wt == 0
    taps = conv_w.shape[0]
    vec = pl.BlockSpec((1, wt), lambda b, j: (0, j))
    blk = pl.BlockSpec((1, wt, wt), lambda b, j: (j, 0, 0))
    return pl.pallas_call(
        _lru_kernel,
        out_shape=jax.ShapeDtypeStruct((m, LRU_WIDTH), BF16),
        grid=(batch, nw),
        in_specs=[
            pl.BlockSpec((seq, wt), lambda b, j: (b, lx_col // wt + j)),
            pl.BlockSpec((seq, wt), lambda b, j: (b, ly_col // wt + j)),
            pl.BlockSpec((taps, wt), lambda b, j: (0, j)),
            vec, blk, vec, blk, vec, vec,
        ],
        out_specs=pl.BlockSpec((seq, wt), lambda b, j: (b, j)),
        scratch_shapes=[pltpu.VMEM((SUBLANES + seq, wt), F32),
                        pltpu.VMEM((seq + SUBLANES * LRU_SEG_PAD, wt), F32),
                        pltpu.VMEM((seq + SUBLANES * LRU_SEG_PAD, wt), F32)],
        compiler_params=_params("parallel", "parallel"),
        name="lru",
    )(z, z, conv_w, conv_b, wa, ba, wi, bi, lam)


def _merge_kernel(oa_ref, ob_ref, pa_ref, pb_ref, ga_ref, gb_ref, o_ref):
    a = _dot(oa_ref[...], pa_ref[...])
    b = _dot(ob_ref[...], pb_ref[...])
    merged = jax.nn.sigmoid(ga_ref[...].astype(F32)) * a + jax.nn.sigmoid(gb_ref[...].astype(F32)) * b
    o_ref[...] = merged.astype(o_ref.dtype)


def _merge(o_a, o_b, proj_a, proj_b, layer, z, ga_col, gb_col, d_model, tm_target=1024, tn_target=512):
    m, ka = o_a.shape
    kb = o_b.shape[1]
    tm = _tile(m, tm_target, BF16_ROWS)
    tn = _tile(d_model, tn_target)
    ga0 = ga_col // tn
    gb0 = gb_col // tn
    assert ga_col % tn == 0 and gb_col % tn == 0
    return pl.pallas_call(
        _merge_kernel,
        out_shape=jax.ShapeDtypeStruct((m, d_model), BF16),
        grid=(m // tm, d_model // tn),
        in_specs=[
            pl.BlockSpec((tm, ka), lambda i, j: (i, 0)),
            pl.BlockSpec((tm, kb), lambda i, j: (i, 0)),
            pl.BlockSpec((None, ka, tn), lambda i, j: (layer, 0, j)),
            pl.BlockSpec((None, kb, tn), lambda i, j: (layer, 0, j)),
            pl.BlockSpec((tm, tn), lambda i, j: (i, ga0 + j)),
            pl.BlockSpec((tm, tn), lambda i, j: (i, gb0 + j)),
        ],
        out_specs=pl.BlockSpec((tm, tn), lambda i, j: (i, j)),
        compiler_params=_params("parallel", "parallel"),
        name="merge",
    )(o_a, o_b, proj_a, proj_b, z, z)


def _matmul_residual_kernel(a_ref, w_ref, x_ref, g_ref, o_ref):
    o_ref[...] = x_ref[...] + g_ref[...] * _dot(a_ref[...], w_ref[...])


def _matmul_residual(a, w, layer, x, gate, seq, name, tm_target, tn_target):
    m, k = a.shape
    n = w.shape[-1]
    tm = _tile(seq, tm_target, BF16_ROWS)
    tn = _tile(n, tn_target)
    return pl.pallas_call(
        _matmul_residual_kernel,
        out_shape=jax.ShapeDtypeStruct((m, n), F32),
        grid=(n // tn, m // tm),
        in_specs=[
            pl.BlockSpec((tm, k), lambda j, i: (i, 0)),
            pl.BlockSpec((None, k, tn), lambda j, i: (layer, 0, j)),
            pl.BlockSpec((tm, tn), lambda j, i: (i, j)),
            pl.BlockSpec((None, 1, tn), lambda j, i: ((i * tm) // seq, 0, j)),
        ],
        out_specs=pl.BlockSpec((tm, tn), lambda j, i: (i, j)),
        compiler_params=_params("parallel", "parallel"),
        name=name,
    )(a, w, x, gate)


def _ffn_up_kernel(xn_ref, g_ref, scn_ref, shn_ref, wg_ref, wv_ref, cw_ref, cb_ref, o_ref, h_ref, inv_ref, *, seq, ahead):
    halo = BF16_ROWS
    tm = o_ref.shape[0]
    i = pl.program_id(0)
    j = pl.program_id(1)
    slot = i % 2

    @pl.when((i == 0) & (j == 0))
    def _():
        h_ref[0, 0:halo, :] = jnp.zeros((halo, h_ref.shape[2]), BF16)
        _store_norm_mod(h_ref.at[0], halo, xn_ref, g_ref, scn_ref, shn_ref, inv_ref)

    h = h_ref[slot]
    gate = _dot(h, wg_ref[...])
    val = _dot(h[halo:], wv_ref[...])
    seq_start = (i * tm) % seq == 0
    row = lax.broadcasted_iota(jnp.int32, (tm, 1), 0)
    taps = cw_ref.shape[0]
    conv = cb_ref[...]
    for k in range(taps):
        back = taps - 1 - k
        gk = gate[halo - back:halo - back + tm]
        if back:
            gk = jnp.where(row >= jnp.where(seq_start, back, 0), gk, 0.0)
        conv = conv + gk * cw_ref[k:k + 1, :]
    o_ref[...] = (jax.nn.gelu(conv) * val).astype(o_ref.dtype)

    r0 = pl.multiple_of(jnp.clip((j - 1) * ahead, 0, tm - ahead), BF16_ROWS)
    h_ref[1 - slot, pl.ds(halo + r0, ahead), :] = _norm_mod_rows(xn_ref[pl.ds(r0, ahead), :], g_ref, scn_ref, shn_ref)
    h_ref[1 - slot, 0:halo, :] = h_ref[slot, tm:tm + halo, :]


def _ffn_up(x, g, sc, sh, w_up, layer, conv_w, conv_b, seq, tm_target=1024, tn_target=768):
    m, d = x.shape
    d_ff = w_up.shape[-1] // 2
    tm = _tile(seq, tm_target, BF16_ROWS)
    tn = _tile(d_ff, min(tn_target, d_ff // 2))
    n_i, n_j = m // tm, d_ff // tn
    halo = BF16_ROWS
    taps = conv_w.shape[0]
    ahead_x = lambda i, j: (_ahead_tile(i, j, n_i), 0)
    ahead_b = lambda i, j: ((_ahead_tile(i, j, n_i) * tm) // seq, 0, 0)
    return pl.pallas_call(
        functools.partial(_ffn_up_kernel, seq=seq, ahead=_ahead_rows(tm, n_j)),
        out_shape=jax.ShapeDtypeStruct((m, d_ff), BF16),
        grid=(n_i, n_j),
        in_specs=[
            pl.BlockSpec((tm, d), ahead_x),
            pl.BlockSpec((1, d), lambda i, j: (0, 0)),
            pl.BlockSpec((None, 1, d), ahead_b),
            pl.BlockSpec((None, 1, d), ahead_b),
            pl.BlockSpec((None, d, tn), lambda i, j: (layer, 0, j)),
            pl.BlockSpec((None, d, tn), lambda i, j: (layer, 0, n_j + j)),
            pl.BlockSpec((taps, tn), lambda i, j: (0, j)),
            pl.BlockSpec((1, tn), lambda i, j: (0, j)),
        ],
        out_specs=pl.BlockSpec((tm, tn), lambda i, j: (i, j)),
        scratch_shapes=[pltpu.VMEM((2, halo + tm, d), BF16), pltpu.VMEM((tm, LANES), F32)],
        compiler_params=_params("arbitrary", "arbitrary"),
        name="ffn_up",
    )(x, g, sc, sh, w_up, w_up, conv_w, conv_b)


def _final_norm_kernel(x_ref, g_ref, o_ref):
    x = x_ref[...]
    inv = lax.rsqrt(jnp.mean(x * x, axis=-1, keepdims=True) + NORM_EPS)
    o_ref[...] = (x * inv) * g_ref[...]


def _final_norm(x, g, tm_target=512):
    m, d = x.shape
    tm = _tile(m, tm_target, SUBLANES)
    return pl.pallas_call(
        _final_norm_kernel,
        out_shape=jax.ShapeDtypeStruct((m, d), F32),
        grid=(m // tm,),
        in_specs=[pl.BlockSpec((tm, d), lambda i: (i, 0)), pl.BlockSpec((1, d), lambda i: (0, 0))],
        out_specs=pl.BlockSpec((tm, d), lambda i: (i, 0)),
        compiler_params=_params("parallel"),
        name="final_norm",
    )(x, g)


def _compress_weights(pos, w1, w2):
    half = CMP_LEN // 2
    return (pos.reshape(2, half * HEAD_DIM),
            w1.reshape(2, half * HEAD_DIM, HEAD_DIM).astype(BF16),
            w2.astype(BF16))


def kernel(x, c, ada_w, ada_b, norm1_g, w_in, cmp_pos_k, cmp_w1_k, cmp_w2_k, cmp_pos_v, cmp_w1_v, cmp_w2_v,
           lru_conv_w, lru_conv_b, lru_wa, lru_ba, lru_wi, lru_bi, lru_lambda, proj_a, proj_b, w_out, norm2_g,
           ffn_up, ffn_conv_w, ffn_conv_b, ffn_down, final_g):
    batch, seq, d = x.shape
    depth = ada_w.shape[0]
    m = batch * seq
    assert seq % SLC_LEN == 0 and seq // SLC_LEN == LANES // 4 and CMP_STRIDE * LANES == seq
    n_attn = NSA_WIDTH + 6 * KV_WIDTH
    n_rest = 2 * LRU_WIDTH + 2 * d
    n_z = -(-(n_attn + n_rest + N_GATES) // (4 * LANES)) * (4 * LANES)
    z_cols = dict(lx=n_attn, ly=n_attn + LRU_WIDTH, ga=n_attn + 2 * LRU_WIDTH, gb=n_attn + 2 * LRU_WIDTH + d,
                  gates=n_attn + n_rest)

    mod = _ada(c, ada_w, ada_b)
    xf = x.reshape(m, d)
    q_scale = jnp.concatenate([jnp.full((1, NSA_WIDTH), HEAD_DIM ** -0.5 * LOG2_E, F32), jnp.ones((1, n_z - NSA_WIDTH), F32)],
                              axis=1)
    w_all = jnp.concatenate([w_in[:, :, :n_attn], w_in[:, :, n_attn + N_GATES:],
                             jnp.pad(w_in[:, :, n_attn:n_attn + N_GATES],
                                     ((0, 0), (0, 0), (0, n_z - n_attn - n_rest - N_GATES)))], axis=2).astype(BF16)
    proj_a, proj_b, w_out, ffn_up, ffn_down, lru_wa, lru_wi = [
        w.astype(BF16) for w in (proj_a, proj_b, w_out, ffn_up, ffn_down, lru_wa, lru_wi)]

    for l in range(depth):
        sh1, sc1, g1, sh2, sc2, g2 = [mod[l, :, k * d:(k + 1) * d].reshape(batch, 1, d) for k in range(6)]
        z = _in_proj(xf, norm1_g[l].reshape(1, d), sc1, sh1, w_all, l, q_scale, BF16, seq)

        pk, w1k, w2k = _compress_weights(cmp_pos_k[l], cmp_w1_k[l], cmp_w2_k[l])
        pv, w1v, w2v = _compress_weights(cmp_pos_v[l], cmp_w1_v[l], cmp_w2_v[l])
        kv_cmp = _compress(z, batch, seq, jnp.stack([pk, pv]), jnp.stack([w1k, w1v]), jnp.stack([w2k, w2v]))

        o_a = _nsa(z, kv_cmp, z_cols["gates"], batch, seq)
        o_b = _lru(z, z_cols["lx"], z_cols["ly"], lru_conv_w[l], lru_conv_b[l].reshape(1, -1), lru_wa[l],
                   lru_ba[l].reshape(1, -1), lru_wi[l], lru_bi[l].reshape(1, -1),
                   lru_lambda[l].reshape(1, -1), batch, seq)
        merged = _merge(o_a, o_b, proj_a, proj_b, l, z, z_cols["ga"], z_cols["gb"], d)
        xf = _matmul_residual(merged, w_out, l, xf, g1, seq, "out_proj", tm_target=512, tn_target=2048)

        act = _ffn_up(xf, norm2_g[l].reshape(1, d), sc2, sh2, ffn_up, l, ffn_conv_w[l],
                      ffn_conv_b[l].reshape(1, -1), seq)
        xf = _matmul_residual(act, ffn_down, l, xf, g2, seq, "ffn_down", tm_target=512, tn_target=1024)

    return _final_norm(xf, final_g.reshape(1, d)).reshape(batch, seq, d)
```

```python
import functools

import jax
import jax.numpy as jnp
from jax import lax
from jax.experimental import pallas as pl
from jax.experimental.pallas import tpu as pltpu

F32 = jnp.float32
BF16 = jnp.bfloat16

NSA_HEADS = 16
NSA_KV_GROUPS = 4
HEADS_PER_GROUP = NSA_HEADS // NSA_KV_GROUPS
HEAD_DIM = 128
NSA_WIDTH = NSA_HEADS * HEAD_DIM
KV_WIDTH = NSA_KV_GROUPS * HEAD_DIM
CMP_LEN = 32
CMP_STRIDE = 16
SLC_LEN = 64
SLC_LEN_LOG2 = SLC_LEN.bit_length() - 1
SLC_TOP_N = 16
SLC_LOCAL = 2
FORCE_BONUS = 1e4
WINDOW = 512
LRU_WIDTH = 2048
LRU_BLOCK_DIM = 128
LRU_C = 8.0
NORM_EPS = 1e-6
LRU_SEG_PAD = 8
LRU_SCAN_UNROLL = 8
N_GATES = 3 * NSA_HEADS
GATES_PER_GROUP = 3 * HEADS_PER_GROUP

LANES = 128
SUBLANES = 8
BF16_ROWS = 16
VMEM_LIMIT_BYTES = 56 * 1024 * 1024

DENOM_FLOOR = 1e-30
MASK_VALUE = -1e30
LOG2_E = 1.4426950408889634

_NT = (((1,), (1,)), ((), ()))


def _params(*sem):
    return pltpu.CompilerParams(dimension_semantics=sem, vmem_limit_bytes=VMEM_LIMIT_BYTES)


def _tile(n, target, quantum=LANES):
    best = None
    t = quantum
    while t <= min(n, target):
        if n % t == 0:
            best = t
        t += quantum
    assert best is not None, (n, target, quantum)
    return best


def _dot(a, b):
    return jnp.dot(a, b, preferred_element_type=F32)


def _store_norm_mod(h_ref, row0, x_ref, g_ref, sc_ref, sh_ref, inv_ref):
    rows, d = x_ref.shape
    x = x_ref[...]
    inv = lax.rsqrt(jnp.mean(x * x, axis=-1, keepdims=True) + NORM_EPS)
    inv_ref[0:rows, :] = jnp.broadcast_to(inv, (rows, LANES))
    gain = g_ref[...]
    scale = 1.0 + sc_ref[...]
    shift = sh_ref[...]

    def slab(i, carry):
        r0 = pl.multiple_of(i * BF16_ROWS, BF16_ROWS)
        inv_s = jnp.concatenate([inv_ref[pl.ds(r0, BF16_ROWS), :]] * (d // LANES), axis=1)
        h = ((x_ref[pl.ds(r0, BF16_ROWS), :] * inv_s) * gain) * scale + shift
        h_ref[pl.ds(row0 + r0, BF16_ROWS), :] = h.astype(BF16)
        return carry

    lax.fori_loop(0, rows // BF16_ROWS, slab, 0, unroll=4 if rows >= 4 * BF16_ROWS else 1)


def _norm_mod_rows(x, g_ref, sc_ref, sh_ref):
    inv = lax.rsqrt(jnp.mean(x * x, axis=-1, keepdims=True) + NORM_EPS)
    return (((x * inv) * g_ref[...]) * (1.0 + sc_ref[...]) + sh_ref[...]).astype(BF16)


def _ahead_rows(tm, n_steps):
    return min(tm, -(-tm // ((n_steps - 1) * BF16_ROWS)) * BF16_ROWS)


def _ahead_tile(i, j, n_tiles):
    return jnp.where((i == 0) & (j == 0), 0, jnp.minimum(i + 1, n_tiles - 1))


def _ada_kernel(c_ref, w_ref, b_ref, o_ref):
    c = c_ref[...]
    act = (c * jax.nn.sigmoid(c)).astype(BF16)
    o_ref[...] = _dot(act, w_ref[...].astype(BF16)) + b_ref[...]


def _ada(c, ada_w, ada_b):
    depth, d, n = ada_w.shape
    b = c.shape[0]
    tn = _tile(n, 1024)
    return pl.pallas_call(
        _ada_kernel,
        out_shape=jax.ShapeDtypeStruct((depth, b, n), F32),
        grid=(depth, n // tn),
        in_specs=[
            pl.BlockSpec((b, d), lambda l, j: (0, 0)),
            pl.BlockSpec((None, d, tn), lambda l, j: (l, 0, j)),
            pl.BlockSpec((None, 1, tn), lambda l, j: (l, 0, j)),
        ],
        out_specs=pl.BlockSpec((None, b, tn), lambda l, j: (l, 0, j)),
        compiler_params=_params("parallel", "parallel"),
        name="ada",
    )(c, ada_w, ada_b.reshape(depth, 1, n))


def _in_proj_kernel(xn_ref, g_ref, scn_ref, shn_ref, w_ref, cs_ref, o_ref, h_ref, inv_ref, *, ahead):
    i = pl.program_id(0)
    j = pl.program_id(1)
    tm = o_ref.shape[0]
    slot = i % 2

    @pl.when((i == 0) & (j == 0))
    def _():
        _store_norm_mod(h_ref.at[0], 0, xn_ref, g_ref, scn_ref, shn_ref, inv_ref)

    o_ref[...] = (_dot(h_ref[slot], w_ref[...]) * cs_ref[...]).astype(o_ref.dtype)
    r0 = pl.multiple_of(jnp.clip((j - 1) * ahead, 0, tm - ahead), BF16_ROWS)
    h_ref[1 - slot, pl.ds(r0, ahead), :] = _norm_mod_rows(xn_ref[pl.ds(r0, ahead), :], g_ref, scn_ref, shn_ref)


def _in_proj(x, g, sc, sh, w, layer, colscale, out_dtype, seq, tm_target=1024, tn_target=1536):
    m, d = x.shape
    n = w.shape[-1]
    tm = _tile(seq, tm_target, BF16_ROWS)
    tn = _tile(n, min(tn_target, n // 2))
    n_i, n_j = m // tm, n // tn
    ahead_x = lambda i, j: (_ahead_tile(i, j, n_i), 0)
    ahead_b = lambda i, j: ((_ahead_tile(i, j, n_i) * tm) // seq, 0, 0)
    return pl.pallas_call(
        functools.partial(_in_proj_kernel, ahead=_ahead_rows(tm, n_j)),
        out_shape=jax.ShapeDtypeStruct((m, n), out_dtype),
        grid=(n_i, n_j),
        in_specs=[
            pl.BlockSpec((tm, d), ahead_x),
            pl.BlockSpec((1, d), lambda i, j: (0, 0)),
            pl.BlockSpec((None, 1, d), ahead_b),
            pl.BlockSpec((None, 1, d), ahead_b),
            pl.BlockSpec((None, d, tn), lambda i, j: (layer, 0, j)),
            pl.BlockSpec((1, tn), lambda i, j: (0, j)),
        ],
        out_specs=pl.BlockSpec((tm, tn), lambda i, j: (i, j)),
        scratch_shapes=[pltpu.VMEM((2, tm, d), BF16), pltpu.VMEM((tm, LANES), F32)],
        compiler_params=_params("arbitrary", "arbitrary"),
        name="in_proj",
    )(x, g, sc, sh, w, colscale)


def _compress_kernel(kv_ref, pos_ref, w1_ref, w2_ref, o_ref, tok_sc, c_sc):
    g, n, hd = o_ref.shape
    for gi in range(g):
        tok_sc[gi] = kv_ref[:, gi * hd:(gi + 1) * hd].astype(F32)
        for j in range(CMP_STRIDE):
            c_sc[gi * n:(gi + 1) * n, j * hd:(j + 1) * hd] = tok_sc[gi, pl.ds(j, n, stride=CMP_STRIDE), :]
    c = c_sc[...]
    lo = (c + pos_ref[0:1, :]).astype(BF16)
    hi = (c + pos_ref[1:2, :]).astype(BF16)
    first = _dot(lo, w1_ref[0])
    second = _dot(hi, w1_ref[1])
    hid = jnp.concatenate([first[gi * n:(gi + 1) * n] + pltpu.roll(second[gi * n:(gi + 1) * n], n - 1, axis=0)
                           for gi in range(g)], axis=0)
    out = _dot(jax.nn.gelu(hid).astype(BF16), w2_ref[...]).astype(o_ref.dtype)
    for gi in range(g):
        o_ref[gi] = out[gi * n:(gi + 1) * n]


def _compress(z, batch, seq, pos, w1, w2):
    g = NSA_KV_GROUPS
    hd = HEAD_DIM
    n = seq // CMP_STRIDE
    cw = CMP_STRIDE * hd
    kv0 = NSA_WIDTH // KV_WIDTH
    return pl.pallas_call(
        _compress_kernel,
        out_shape=jax.ShapeDtypeStruct((batch, 2, g, n, hd), BF16),
        grid=(batch, 2),
        in_specs=[
            pl.BlockSpec((seq, KV_WIDTH), lambda bi, s: (bi, kv0 + s)),
            pl.BlockSpec((None, 2, cw), lambda bi, s: (s, 0, 0)),
            pl.BlockSpec((None, 2, cw, hd), lambda bi, s: (s, 0, 0, 0)),
            pl.BlockSpec((None, hd, hd), lambda bi, s: (s, 0, 0)),
        ],
        out_specs=pl.BlockSpec((None, None, g, n, hd), lambda bi, s: (bi, s, 0, 0, 0)),
        scratch_shapes=[pltpu.VMEM((g, seq, hd), F32), pltpu.VMEM((g * n, cw), F32)],
        compiler_params=_params("parallel", "parallel"),
        name="compress",
    )(z, pos, w1, w2)


def _nsa_kernel(q_ref, ks_ref, vs_ref, kw_ref, vw_ref, kc_ref, vc_ref, zg_ref, o_ref,
                kaug_sc, vsaug_sc, vwaug_sc, vcaug_sc, ov_sc, bias_sc, qa_sc, gate_sc, s_sc, sw_sc, m_sc, acc_sc, out_sc,
                *, tq):
    hd = HEAD_DIM
    hpg = HEADS_PER_GROUP
    tk = tq
    n_blk = LANES // 4
    seq = ks_ref.shape[0]
    n_cmp = kc_ref.shape[0]
    grp = pl.program_id(1)
    qi = pl.program_id(2)
    t0 = qi * tq

    @pl.when(qi == 0)
    def _():
        kaug_sc[:, :hd] = ks_ref[...]
        key_blk = lax.shift_right_logical(lax.broadcasted_iota(jnp.int32, (seq, LANES), 0), SLC_LEN_LOG2)
        lane_blk = lax.broadcasted_iota(jnp.int32, (seq, LANES), 1) & (n_blk - 1)
        kaug_sc[:, hd:] = jnp.where(key_blk == lane_blk, 1.0, 0.0).astype(BF16)
        ones = jnp.ones((seq, LANES), BF16)
        vsaug_sc[:, :hd] = vs_ref[...]
        vsaug_sc[:, hd:] = ones
        vwaug_sc[:, :hd] = vw_ref[...]
        vwaug_sc[:, hd:] = ones
        vcaug_sc[:, :hd] = vc_ref[...]
        vcaug_sc[:, hd:] = jnp.ones((n_cmp, LANES), BF16)
        cstart = (lax.broadcasted_iota(jnp.int32, (n_blk, hpg * n_cmp), 1) & (n_cmp - 1)) * CMP_STRIDE
        bstart = lax.broadcasted_iota(jnp.int32, (n_blk, hpg * n_cmp), 0) * SLC_LEN
        ov_sc[...] = jnp.where((cstart < bstart + SLC_LEN) & (cstart + CMP_LEN > bstart), 1.0, 0.0).astype(BF16)
        r = lax.broadcasted_iota(jnp.int32, (tq, tk), 0)
        cidx = lax.broadcasted_iota(jnp.int32, (tq, tk), 1)
        bias_sc[0] = jnp.where(cidx <= r, 0.0, MASK_VALUE)
        bias_sc[1] = jnp.where(cidx > r, 0.0, MASK_VALUE)

    q = q_ref[...]
    qs = jnp.concatenate([q[:, h * hd:(h + 1) * hd] for h in range(hpg)], axis=0)
    qpos = t0 + (lax.broadcasted_iota(jnp.int32, (hpg * tq, 1), 0) & (tq - 1))

    def put_scores(dst, slot, k_ref, start, lhs):
        k_t = k_ref[pl.ds(pl.multiple_of(start, tq), tq), :]
        for h in range(hpg):
            dst[slot, h] = lax.dot_general(lhs(h), k_t, _NT, preferred_element_type=F32)

    def reset():
        m_sc[...] = jnp.full(m_sc.shape, MASK_VALUE, F32)
        acc_sc[...] = jnp.zeros(acc_sc.shape, F32)

    def fold_head(src, slot, h, v_t, bias):
        sc = src[slot, h]
        if bias is not None:
            sc = sc + bias
        m_prev = m_sc[h]
        m_new = jnp.maximum(m_prev, jnp.max(sc, axis=-1, keepdims=True))
        alpha = jnp.exp2(m_prev - m_new)
        pe = jnp.exp2(sc - jnp.concatenate([m_new] * (tq // LANES), axis=1))
        acc_sc[h] = jnp.concatenate([alpha, alpha], axis=1) * acc_sc[h] + _dot(pe.astype(BF16), v_t)
        m_sc[h] = m_new

    def fold(src, slot, v_ref, start, mask):
        v_t = v_ref[pl.ds(pl.multiple_of(start, tq), tq), :]
        bias = None if mask is None else bias_sc[mask]
        for h in range(hpg):
            fold_head(src, slot, h, v_t, bias)

    def finish(h):
        acc = acc_sc[h]
        return acc[:, :hd] / acc[:, hd:]

    q_head = lambda h: q[:, h * hd:(h + 1) * hd]
    s = lax.dot_general(qs, kc_ref[...], _NT, preferred_element_type=F32)
    st = lax.dot_general(kc_ref[...], qs, _NT, preferred_element_type=F32)
    put_scores(sw_sc, 0, kw_ref, t0, q_head)

    gt = jax.nn.sigmoid(pltpu.roll(zg_ref[...].astype(F32), (LANES - GATES_PER_GROUP * grp) & (LANES - 1), axis=1))

    for c in range(GATES_PER_GROUP):
        gate_sc[:, c * LANES:(c + 1) * LANES] = jnp.broadcast_to(gt[:, c:c + 1], (tq, LANES))

    def gate(h, branch):
        c = 3 * h + branch
        return gate_sc[:, c * LANES:(c + 1) * LANES]

    cmp_end = lax.broadcasted_iota(jnp.int32, (1, n_cmp), 1) * CMP_STRIDE + (CMP_LEN - 1)
    s = jnp.where(cmp_end <= qpos, s, -jnp.inf)
    mx = jnp.max(s, axis=-1, keepdims=True)
    mx = jnp.where(jnp.isfinite(mx), mx, 0.0)
    pe_cmp = jnp.exp2(s - mx).astype(BF16)

    end_t = lax.broadcasted_iota(jnp.int32, (n_cmp, 1), 0) * CMP_STRIDE + (CMP_LEN - 1)
    qpos_t = t0 + (lax.broadcasted_iota(jnp.int32, (1, hpg * tq), 1) & (tq - 1))
    st = jnp.where(end_t <= qpos_t, st, -jnp.inf)
    mt = jnp.max(st, axis=0, keepdims=True)
    mt = jnp.where(jnp.isfinite(mt), mt, 0.0)
    pt = jnp.exp2(st - mt)
    pt = pt / jnp.maximum(jnp.sum(pt, axis=0, keepdims=True), DENOM_FLOOR)
    p_heads = jnp.concatenate([pt[:, h * tq:(h + 1) * tq] for h in range(hpg)], axis=0).astype(BF16)
    imp = _dot(ov_sc[...], p_heads)
    acc = _dot(pe_cmp, vcaug_sc[...])
    put_scores(sw_sc, 1, kw_ref, jnp.maximum(t0 - tq, 0), q_head)

    blk = lax.broadcasted_iota(jnp.int32, (n_blk, tq), 0)
    cur = lax.shift_right_logical(t0 + lax.broadcasted_iota(jnp.int32, (n_blk, tq), 1), SLC_LEN_LOG2)
    valid = blk <= cur
    forced = (blk == 0) | (valid & (blk > cur - SLC_LOCAL))
    score = jnp.where(forced, imp + FORCE_BONUS, jnp.where(valid, imp, -1.0))
    key = lax.bitcast_convert_type(score, jnp.int32)
    rank = jnp.zeros((n_blk, tq), jnp.int32)
    for r in range(1, n_blk):
        other = pltpu.roll(key, r, axis=0)
        rank = rank + jnp.where(other + jnp.where(blk >= r, 1, 0) > key, 1, 0)
    sel_bias = jnp.where((rank < SLC_TOP_N) & valid, 0.0, MASK_VALUE)
    sel_rows = jnp.concatenate([sel_bias, jnp.zeros((LANES - n_blk, tq), F32)], axis=0).T.astype(BF16)
    for h in range(hpg):
        qa_sc[h] = jnp.concatenate([q_head(h), sel_rows], axis=1)
    o_cmp = acc[:, :hd] / jnp.maximum(acc[:, hd:], DENOM_FLOOR)
    for h in range(hpg):
        out_sc[:, h * hd:(h + 1) * hd] = gate(h, 0) * o_cmp[h * tq:(h + 1) * tq]

    assert WINDOW == tq
    reset()
    k_first = kaug_sc[0:tq, :]
    v_diag = vwaug_sc[pl.ds(pl.multiple_of(t0, tq), tq), :]
    for h in range(hpg):
        s_sc[0, h] = lax.dot_general(qa_sc[h], k_first, _NT, preferred_element_type=F32)
        fold_head(sw_sc, 0, h, v_diag, bias_sc[0])

    @pl.when(qi >= 1)
    def _():
        fold(sw_sc, 1, vwaug_sc, t0 - tq, 1)

    for h in range(hpg):
        out_sc[:, h * hd:(h + 1) * hd] += gate(h, 2) * finish(h)

    reset()

    def score_and_fold(put_slot, put_start, fold_slot, fold_start, mask):
        k_t = kaug_sc[pl.ds(pl.multiple_of(put_start, tq), tq), :]
        v_t = vsaug_sc[pl.ds(pl.multiple_of(fold_start, tq), tq), :]
        bias = None if mask is None else bias_sc[mask]
        for h in range(hpg):
            s_sc[put_slot, h] = lax.dot_general(qa_sc[h], k_t, _NT, preferred_element_type=F32)
            fold_head(s_sc, fold_slot, h, v_t, bias)

    def slc_pair(j, carry):
        e = 2 * j * tq
        score_and_fold(1, e + tq, 0, e, None)
        score_and_fold(0, e + 2 * tq, 1, e + tq, None)
        return carry

    lax.fori_loop(0, qi // 2, slc_pair, 0)

    @pl.when(qi % 2 == 1)
    def _():
        score_and_fold(1, t0, 0, t0 - tq, None)
        fold(s_sc, 1, vsaug_sc, t0, 0)

    @pl.when(qi % 2 == 0)
    def _():
        fold(s_sc, 0, vsaug_sc, t0, 0)

    for h in range(hpg):
        o_h = out_sc[:, h * hd:(h + 1) * hd] + gate(h, 1) * finish(h)
        o_ref[:, h * hd:(h + 1) * hd] = o_h.astype(o_ref.dtype)


def _nsa(z, kv_cmp, gates_col, batch, seq, tq=WINDOW):
    m = z.shape[0]
    assert gates_col % LANES == 0
    g = NSA_KV_GROUPS
    hd = HEAD_DIM
    hpg = HEADS_PER_GROUP
    n_cmp = kv_cmp.shape[3]
    nq = seq // tq
    qcols = NSA_WIDTH // hd
    assert (tq // 4) % SLC_LEN == 0 and n_cmp == LANES

    def kv_spec(slot):
        return pl.BlockSpec((seq, hd), lambda b, gi, i: (b, qcols + slot * g + gi))

    def cmp_spec(slot):
        return pl.BlockSpec((None, None, None, n_cmp, hd), lambda b, gi, i: (b, slot, gi, 0, 0))

    return pl.pallas_call(
        functools.partial(_nsa_kernel, tq=tq),
        out_shape=jax.ShapeDtypeStruct((m, NSA_WIDTH), BF16),
        grid=(batch, g, nq),
        in_specs=[
            pl.BlockSpec((tq, hpg * hd), lambda b, gi, i: (b * nq + i, gi)),
            kv_spec(2), kv_spec(3), kv_spec(4), kv_spec(5),
            cmp_spec(0), cmp_spec(1),
            pl.BlockSpec((tq, LANES), lambda b, gi, i: (b * nq + i, gates_col // LANES)),
        ],
        out_specs=pl.BlockSpec((tq, hpg * hd), lambda b, gi, i: (b * nq + i, gi)),
        scratch_shapes=[
            pltpu.VMEM((seq, 2 * hd), BF16),
            pltpu.VMEM((seq, 2 * hd), BF16),
            pltpu.VMEM((seq, 2 * hd), BF16),
            pltpu.VMEM((n_cmp, 2 * hd), BF16),
            pltpu.VMEM((LANES // 4, hpg * n_cmp), BF16),
            pltpu.VMEM((2, tq, tq), F32),
            pltpu.VMEM((hpg, tq, 2 * hd), BF16),
            pltpu.VMEM((tq, GATES_PER_GROUP * LANES), F32),
            pltpu.VMEM((2, hpg, tq, tq), F32),
            pltpu.VMEM((2, hpg, tq, tq), F32),
            pltpu.VMEM((hpg, tq, LANES), F32),
            pltpu.VMEM((hpg, tq, 2 * hd), F32),
            pltpu.VMEM((tq, hpg * hd), F32),
        ],
        compiler_params=_params("parallel", "parallel", "arbitrary"),
        name="nsa",
    )(z, z, z, z, z, kv_cmp, kv_cmp, z)


def _lru_kernel(lx_ref, ly_ref, cw_ref, cb_ref, wa_ref, ba_ref, wi_ref, bi_ref, lam_ref, o_ref, x_sc, a_sc, h_sc):
    seq, wt = lx_ref.shape
    x_sc[0:SUBLANES, :] = jnp.zeros((SUBLANES, wt), F32)
    x_sc[SUBLANES:, :] = lx_ref[...].astype(F32)
    taps = cw_ref.shape[0]
    u = cb_ref[...]
    for j in range(taps):
        back = taps - 1 - j
        u = u + x_sc[SUBLANES - back:SUBLANES - back + seq, :] * cw_ref[j:j + 1, :]
    ub = u.astype(BF16)
    r = jax.nn.sigmoid(_dot(ub, wa_ref[0]) + ba_ref[...])
    gate_i = jax.nn.sigmoid(_dot(ub, wi_ref[0]) + bi_ref[...])
    log_a = r * (LRU_C * jax.nn.log_sigmoid(lam_ref[...]))
    a = jnp.exp(log_a)
    y = 1.0 - a * a
    b = jnp.where(y > 0.0, y * lax.rsqrt(y), 0.0) * (gate_i * u)

    n_seg = SUBLANES
    seg = seq // n_seg
    stride = a_sc.shape[0] // n_seg
    for i in range(n_seg):
        a_sc[stride * i:stride * i + seg, :] = a[seg * i:seg * (i + 1), :]
        h_sc[stride * i:stride * i + seg, :] = b[seg * i:seg * (i + 1), :]

    def steps(jj, carry):
        h, p = carry
        for k in range(LRU_SCAN_UNROLL):
            rows = pl.ds(jj * LRU_SCAN_UNROLL + k, n_seg, stride=stride)
            a_j = a_sc[rows, :]
            h = a_j * h + h_sc[rows, :]
            p = a_j * p
            h_sc[rows, :] = h
            a_sc[rows, :] = p
        return h, p

    h_end, p_end = lax.fori_loop(0, seg // LRU_SCAN_UNROLL, steps,
                                 (jnp.zeros((n_seg, wt), F32), jnp.ones((n_seg, wt), F32)))
    enter = jnp.zeros((1, wt), F32)
    for i in range(n_seg):
        h_i = h_sc[stride * i:stride * i + seg, :] + a_sc[stride * i:stride * i + seg, :] * enter
        gated = jax.nn.gelu(ly_ref[seg * i:seg * (i + 1), :].astype(F32)) * h_i
        o_ref[seg * i:seg * (i + 1), :] = gated.astype(o_ref.dtype)
        enter = h_end[i:i + 1, :] + p_end[i:i + 1, :] * enter


def _lru(z, lx_col, ly_col, conv_w, conv_b, wa, ba, wi, bi, lam, batch, seq):
    m = z.shape[0]
    wt = LRU_BLOCK_DIM
    nw = LRU_WIDTH // wt
    assert lx_col % wt == 0 and ly_col % wt == 0
    taps = conv_w.shape[0]
    vec = pl.BlockSpec((1, wt), lambda b, j: (0, j))
    blk = pl.BlockSpec((1, wt, wt), lambda b, j: (j, 0, 0))
    return pl.pallas_call(
        _lru_kernel,
        out_shape=jax.ShapeDtypeStruct((m, LRU_WIDTH), BF16),
        grid=(batch, nw),
        in_specs=[
            pl.BlockSpec((seq, wt), lambda b, j: (b, lx_col // wt + j)),
            pl.BlockSpec((seq, wt), lambda b, j: (b, ly_col // wt + j)),
            pl.BlockSpec((taps, wt), lambda b, j: (0, j)),
            vec, blk, vec, blk, vec, vec,
        ],
        out_specs=pl.BlockSpec((seq, wt), lambda b, j: (b, j)),
        scratch_shapes=[pltpu.VMEM((SUBLANES + seq, wt), F32),
                        pltpu.VMEM((seq + SUBLANES * LRU_SEG_PAD, wt), F32),
                        pltpu.VMEM((seq + SUBLANES * LRU_SEG_PAD, wt), F32)],
        compiler_params=_params("parallel", "parallel"),
        name="lru",
    )(z, z, conv_w, conv_b, wa, ba, wi, bi, lam)


def _merge_kernel(oa_ref, ob_ref, pa_ref, pb_ref, ga_ref, gb_ref, o_ref):
    a = _dot(oa_ref[...], pa_ref[...])
    b = _dot(ob_ref[...], pb_ref[...])
    merged = jax.nn.sigmoid(ga_ref[...].astype(F32)) * a + jax.nn.sigmoid(gb_ref[...].astype(F32)) * b
    o_ref[...] = merged.astype(o_ref.dtype)


def _merge(o_a, o_b, proj_a, proj_b, layer, z, ga_col, gb_col, d_model, tm_target=1024, tn_target=1024):
    m, ka = o_a.shape
    kb = o_b.shape[1]
    tm = _tile(m, tm_target, BF16_ROWS)
    tn = _tile(d_model, tn_target)
    ga0 = ga_col // tn
    gb0 = gb_col // tn
    assert ga_col % tn == 0 and gb_col % tn == 0
    return pl.pallas_call(
        _merge_kernel,
        out_shape=jax.ShapeDtypeStruct((m, d_model), BF16),
        grid=(m // tm, d_model // tn),
        in_specs=[
            pl.BlockSpec((tm, ka), lambda i, j: (i, 0)),
            pl.BlockSpec((tm, kb), lambda i, j: (i, 0)),
            pl.BlockSpec((None, ka, tn), lambda i, j: (layer, 0, j)),
            pl.BlockSpec((None, kb, tn), lambda i, j: (layer, 0, j)),
            pl.BlockSpec((tm, tn), lambda i, j: (i, ga0 + j)),
            pl.BlockSpec((tm, tn), lambda i, j: (i, gb0 + j)),
        ],
        out_specs=pl.BlockSpec((tm, tn), lambda i, j: (i, j)),
        compiler_params=_params("parallel", "parallel"),
        name="merge",
    )(o_a, o_b, proj_a, proj_b, z, z)


def _matmul_residual_kernel(a_ref, w_ref, x_ref, g_ref, o_ref):
    o_ref[...] = x_ref[...] + g_ref[...] * _dot(a_ref[...], w_ref[...])


def _matmul_residual(a, w, layer, x, gate, seq, name, tm_target, tn_target):
    m, k = a.shape
    n = w.shape[-1]
    tm = _tile(seq, tm_target, BF16_ROWS)
    tn = _tile(n, tn_target)
    return pl.pallas_call(
        _matmul_residual_kernel,
        out_shape=jax.ShapeDtypeStruct((m, n), F32),
        grid=(n // tn, m // tm),
        in_specs=[
            pl.BlockSpec((tm, k), lambda j, i: (i, 0)),
            pl.BlockSpec((None, k, tn), lambda j, i: (layer, 0, j)),
            pl.BlockSpec((tm, tn), lambda j, i: (i, j)),
            pl.BlockSpec((None, 1, tn), lambda j, i: ((i * tm) // seq, 0, j)),
        ],
        out_specs=pl.BlockSpec((tm, tn), lambda j, i: (i, j)),
        compiler_params=_params("parallel", "parallel"),
        name=name,
    )(a, w, x, gate)


def _ffn_up_kernel(xn_ref, g_ref, scn_ref, shn_ref, wg_ref, wv_ref, cw_ref, cb_ref, o_ref, h_ref, inv_ref, *, seq, ahead):
    halo = BF16_ROWS
    tm = o_ref.shape[0]
    i = pl.program_id(0)
    j = pl.program_id(1)
    slot = i % 2

    @pl.when((i == 0) & (j == 0))
    def _():
        h_ref[0, 0:halo, :] = jnp.zeros((halo, h_ref.shape[2]), BF16)
        _store_norm_mod(h_ref.at[0], halo, xn_ref, g_ref, scn_ref, shn_ref, inv_ref)

    h = h_ref[slot]
    gate = _dot(h, wg_ref[...])
    val = _dot(h[halo:], wv_ref[...])
    seq_start = (i * tm) % seq == 0
    row = lax.broadcasted_iota(jnp.int32, (tm, 1), 0)
    taps = cw_ref.shape[0]
    conv = cb_ref[...]
    for k in range(taps):
        back = taps - 1 - k
        gk = gate[halo - back:halo - back + tm]
        if back:
            gk = jnp.where(row >= jnp.where(seq_start, back, 0), gk, 0.0)
        conv = conv + gk * cw_ref[k:k + 1, :]
    o_ref[...] = (jax.nn.gelu(conv) * val).astype(o_ref.dtype)

    r0 = pl.multiple_of(jnp.clip((j - 1) * ahead, 0, tm - ahead), BF16_ROWS)
    h_ref[1 - slot, pl.ds(halo + r0, ahead), :] = _norm_mod_rows(xn_ref[pl.ds(r0, ahead), :], g_ref, scn_ref, shn_ref)
    h_ref[1 - slot, 0:halo, :] = h_ref[slot, tm:tm + halo, :]


def _ffn_up(x, g, sc, sh, w_up, layer, conv_w, conv_b, seq, tm_target=1024, tn_target=768):
    m, d = x.shape
    d_ff = w_up.shape[-1] // 2
    tm = _tile(seq, tm_target, BF16_ROWS)
    tn = _tile(d_ff, min(tn_target, d_ff // 2))
    n_i, n_j = m // tm, d_ff // tn
    halo = BF16_ROWS
    taps = conv_w.shape[0]
    ahead_x = lambda i, j: (_ahead_tile(i, j, n_i), 0)
    ahead_b = lambda i, j: ((_ahead_tile(i, j, n_i) * tm) // seq, 0, 0)
    return pl.pallas_call(
        functools.partial(_ffn_up_kernel, seq=seq, ahead=_ahead_rows(tm, n_j)),
        out_shape=jax.ShapeDtypeStruct((m, d_ff), BF16),
        grid=(n_i, n_j),
        in_specs=[
            pl.BlockSpec((tm, d), ahead_x),
            pl.BlockSpec((1, d), lambda i, j: (0, 0)),
            pl.BlockSpec((None, 1, d), ahead_b),
            pl.BlockSpec((None, 1, d), ahead_b),
            pl.BlockSpec((None, d, tn), lambda i, j: (layer, 0, j)),
            pl.BlockSpec((None, d, tn), lambda i, j: (layer, 0, n_j + j)),
            pl.BlockSpec((taps, tn), lambda i, j: (0, j)),
            pl.BlockSpec((1, tn), lambda i, j: (0, j)),
        ],
        out_specs=pl.BlockSpec((tm, tn), lambda i, j: (i, j)),
        scratch_shapes=[pltpu.VMEM((2, halo + tm, d), BF16), pltpu.VMEM((tm, LANES), F32)],
        compiler_params=_params("arbitrary", "arbitrary"),
        name="ffn_up",
    )(x, g, sc, sh, w_up, w_up, conv_w, conv_b)


def _final_norm_kernel(x_ref, g_ref, o_ref):
    x = x_ref[...]
    inv = lax.rsqrt(jnp.mean(x * x, axis=-1, keepdims=True) + NORM_EPS)
    o_ref[...] = (x * inv) * g_ref[...]


def _final_norm(x, g, tm_target=512):
    m, d = x.shape
    tm = _tile(m, tm_target, SUBLANES)
    return pl.pallas_call(
        _final_norm_kernel,
        out_shape=jax.ShapeDtypeStruct((m, d), F32),
        grid=(m // tm,),
        in_specs=[pl.BlockSpec((tm, d), lambda i: (i, 0)), pl.BlockSpec((1, d), lambda i: (0, 0))],
        out_specs=pl.BlockSpec((tm, d), lambda i: (i, 0)),
        compiler_params=_params("parallel"),
        name="final_norm",
    )(x, g)


def _compress_weights(pos, w1, w2):
    half = CMP_LEN // 2
    return (pos.reshape(2, half * HEAD_DIM),
            w1.reshape(2, half * HEAD_DIM, HEAD_DIM).astype(BF16),
            w2.astype(BF16))


def kernel(x, c, ada_w, ada_b, norm1_g, w_in, cmp_pos_k, cmp_w1_k, cmp_w2_k, cmp_pos_v, cmp_w1_v, cmp_w2_v,
           lru_conv_w, lru_conv_b, lru_wa, lru_ba, lru_wi, lru_bi, lru_lambda, proj_a, proj_b, w_out, norm2_g,
           ffn_up, ffn_conv_w, ffn_conv_b, ffn_down, final_g):
    batch, seq, d = x.shape
    depth = ada_w.shape[0]
    m = batch * seq
    assert seq % SLC_LEN == 0 and seq // SLC_LEN == LANES // 4 and CMP_STRIDE * LANES == seq
    n_attn = NSA_WIDTH + 6 * KV_WIDTH
    n_rest = 2 * LRU_WIDTH + 2 * d
    n_z = -(-(n_attn + n_rest + N_GATES) // (4 * LANES)) * (4 * LANES)
    z_cols = dict(lx=n_attn, ly=n_attn + LRU_WIDTH, ga=n_attn + 2 * LRU_WIDTH, gb=n_attn + 2 * LRU_WIDTH + d,
                  gates=n_attn + n_rest)

    mod = _ada(c, ada_w, ada_b)
    xf = x.reshape(m, d)
    q_scale = jnp.concatenate([jnp.full((1, NSA_WIDTH), HEAD_DIM ** -0.5 * LOG2_E, F32), jnp.ones((1, n_z - NSA_WIDTH), F32)],
                              axis=1)
    w_all = jnp.concatenate([w_in[:, :, :n_attn], w_in[:, :, n_attn + N_GATES:],
                             jnp.pad(w_in[:, :, n_attn:n_attn + N_GATES],
                                     ((0, 0), (0, 0), (0, n_z - n_attn - n_rest - N_GATES)))], axis=2).astype(BF16)
    proj_a, proj_b, w_out, ffn_up, ffn_down, lru_wa, lru_wi = [
        w.astype(BF16) for w in (proj_a, proj_b, w_out, ffn_up, ffn_down, lru_wa, lru_wi)]

    for l in range(depth):
        sh1, sc1, g1, sh2, sc2, g2 = [mod[l, :, k * d:(k + 1) * d].reshape(batch, 1, d) for k in range(6)]
        z = _in_proj(xf, norm1_g[l].reshape(1, d), sc1, sh1, w_all, l, q_scale, BF16, seq)

        pk, w1k, w2k = _compress_weights(cmp_pos_k[l], cmp_w1_k[l], cmp_w2_k[l])
        pv, w1v, w2v = _compress_weights(cmp_pos_v[l], cmp_w1_v[l], cmp_w2_v[l])
        kv_cmp = _compress(z, batch, seq, jnp.stack([pk, pv]), jnp.stack([w1k, w1v]), jnp.stack([w2k, w2v]))

        o_a = _nsa(z, kv_cmp, z_cols["gates"], batch, seq)
        o_b = _lru(z, z_cols["lx"], z_cols["ly"], lru_conv_w[l], lru_conv_b[l].reshape(1, -1), lru_wa[l],
                   lru_ba[l].reshape(1, -1), lru_wi[l], lru_bi[l].reshape(1, -1),
                   lru_lambda[l].reshape(1, -1), batch, seq)
        merged = _merge(o_a, o_b, proj_a, proj_b, l, z, z_cols["ga"], z_cols["gb"], d)
        xf = _matmul_residual(merged, w_out, l, xf, g1, seq, "out_proj", tm_target=512, tn_target=2048)

        act = _ffn_up(xf, norm2_g[l].reshape(1, d), sc2, sh2, ffn_up, l, ffn_conv_w[l],
                      ffn_conv_b[l].reshape(1, -1), seq)
        xf = _matmul_residual(act, ffn_down, l, xf, g2, seq, "ffn_down", tm_target=512, tn_target=1024)

    return _final_norm(xf, final_g.reshape(1, d)).reshape(batch, seq, d)
```

```python
import functools

import jax
import jax.numpy as jnp
from jax import lax
from jax.experimental import pallas as pl
from jax.experimental.pallas import tpu as pltpu

F32 = jnp.float32
BF16 = jnp.bfloat16

NSA_HEADS = 16
NSA_KV_GROUPS = 4
HEADS_PER_GROUP = NSA_HEADS // NSA_KV_GROUPS
HEAD_DIM = 128
NSA_WIDTH = NSA_HEADS * HEAD_DIM
KV_WIDTH = NSA_KV_GROUPS * HEAD_DIM
CMP_LEN = 32
CMP_STRIDE = 16
SLC_LEN = 64
SLC_LEN_LOG2 = SLC_LEN.bit_length() - 1
SLC_TOP_N = 16
SLC_LOCAL = 2
FORCE_BONUS = 1e4
WINDOW = 512
LRU_WIDTH = 2048
LRU_BLOCK_DIM = 128
LRU_C = 8.0
NORM_EPS = 1e-6
LRU_SEG_PAD = 8
LRU_SCAN_UNROLL = 8
N_GATES = 3 * NSA_HEADS
GATES_PER_GROUP = 3 * HEADS_PER_GROUP

LANES = 128
SUBLANES = 8
BF16_ROWS = 16
VMEM_LIMIT_BYTES = 56 * 1024 * 1024

DENOM_FLOOR = 1e-30
MASK_VALUE = -1e30
LOG2_E = 1.4426950408889634

_NT = (((1,), (1,)), ((), ()))


def _params(*sem):
    return pltpu.CompilerParams(dimension_semantics=sem, vmem_limit_bytes=VMEM_LIMIT_BYTES)


def _tile(n, target, quantum=LANES):
    best = None
    t = quantum
    while t <= min(n, target):
        if n % t == 0:
            best = t
        t += quantum
    assert best is not None, (n, target, quantum)
    return best


def _dot(a, b):
    return jnp.dot(a, b, preferred_element_type=F32)


def _store_norm_mod(h_ref, row0, x_ref, g_ref, sc_ref, sh_ref, inv_ref):
    rows, d = x_ref.shape
    x = x_ref[...]
    inv = lax.rsqrt(jnp.mean(x * x, axis=-1, keepdims=True) + NORM_EPS)
    inv_ref[0:rows, :] = jnp.broadcast_to(inv, (rows, LANES))
    gain = g_ref[...]
    scale = 1.0 + sc_ref[...]
    shift = sh_ref[...]

    def slab(i, carry):
        r0 = pl.multiple_of(i * BF16_ROWS, BF16_ROWS)
        inv_s = jnp.concatenate([inv_ref[pl.ds(r0, BF16_ROWS), :]] * (d // LANES), axis=1)
        h = ((x_ref[pl.ds(r0, BF16_ROWS), :] * inv_s) * gain) * scale + shift
        h_ref[pl.ds(row0 + r0, BF16_ROWS), :] = h.astype(BF16)
        return carry

    lax.fori_loop(0, rows // BF16_ROWS, slab, 0, unroll=4 if rows >= 4 * BF16_ROWS else 1)


def _norm_mod_rows(x, g_ref, sc_ref, sh_ref):
    inv = lax.rsqrt(jnp.mean(x * x, axis=-1, keepdims=True) + NORM_EPS)
    return (((x * inv) * g_ref[...]) * (1.0 + sc_ref[...]) + sh_ref[...]).astype(BF16)


def _ahead_rows(tm, n_steps):
    return min(tm, -(-tm // ((n_steps - 1) * BF16_ROWS)) * BF16_ROWS)


def _ahead_tile(i, j, n_tiles):
    return jnp.where((i == 0) & (j == 0), 0, jnp.minimum(i + 1, n_tiles - 1))


def _ada_kernel(c_ref, w_ref, b_ref, o_ref):
    c = c_ref[...]
    act = (c * jax.nn.sigmoid(c)).astype(BF16)
    o_ref[...] = _dot(act, w_ref[...].astype(BF16)) + b_ref[...]


def _ada(c, ada_w, ada_b):
    depth, d, n = ada_w.shape
    b = c.shape[0]
    tn = _tile(n, 1024)
    return pl.pallas_call(
        _ada_kernel,
        out_shape=jax.ShapeDtypeStruct((depth, b, n), F32),
        grid=(depth, n // tn),
        in_specs=[
            pl.BlockSpec((b, d), lambda l, j: (0, 0)),
            pl.BlockSpec((None, d, tn), lambda l, j: (l, 0, j)),
            pl.BlockSpec((None, 1, tn), lambda l, j: (l, 0, j)),
        ],
        out_specs=pl.BlockSpec((None, b, tn), lambda l, j: (l, 0, j)),
        compiler_params=_params("parallel", "parallel"),
        name="ada",
    )(c, ada_w, ada_b.reshape(depth, 1, n))


def _in_proj_kernel(xn_ref, g_ref, scn_ref, shn_ref, w_ref, cs_ref, o_ref, h_ref, inv_ref, *, ahead):
    i = pl.program_id(0)
    j = pl.program_id(1)
    tm = o_ref.shape[0]
    slot = i % 2

    @pl.when((i == 0) & (j == 0))
    def _():
        _store_norm_mod(h_ref.at[0], 0, xn_ref, g_ref, scn_ref, shn_ref, inv_ref)

    o_ref[...] = (_dot(h_ref[slot], w_ref[...]) * cs_ref[...]).astype(o_ref.dtype)
    r0 = pl.multiple_of(jnp.clip((j - 1) * ahead, 0, tm - ahead), BF16_ROWS)
    h_ref[1 - slot, pl.ds(r0, ahead), :] = _norm_mod_rows(xn_ref[pl.ds(r0, ahead), :], g_ref, scn_ref, shn_ref)


def _in_proj(x, g, sc, sh, w, layer, colscale, out_dtype, seq, tm_target=1024, tn_target=1536):
    m, d = x.shape
    n = w.shape[-1]
    tm = _tile(seq, tm_target, BF16_ROWS)
    tn = _tile(n, min(tn_target, n // 2))
    n_i, n_j = m // tm, n // tn
    ahead_x = lambda i, j: (_ahead_tile(i, j, n_i), 0)
    ahead_b = lambda i, j: ((_ahead_tile(i, j, n_i) * tm) // seq, 0, 0)
    return pl.pallas_call(
        functools.partial(_in_proj_kernel, ahead=_ahead_rows(tm, n_j)),
        out_shape=jax.ShapeDtypeStruct((m, n), out_dtype),
        grid=(n_i, n_j),
        in_specs=[
            pl.BlockSpec((tm, d), ahead_x),
            pl.BlockSpec((1, d), lambda i, j: (0, 0)),
            pl.BlockSpec((None, 1, d), ahead_b),
            pl.BlockSpec((None, 1, d), ahead_b),
            pl.BlockSpec((None, d, tn), lambda i, j: (layer, 0, j)),
            pl.BlockSpec((1, tn), lambda i, j: (0, j)),
        ],
        out_specs=pl.BlockSpec((tm, tn), lambda i, j: (i, j)),
        scratch_shapes=[pltpu.VMEM((2, tm, d), BF16), pltpu.VMEM((tm, LANES), F32)],
        compiler_params=_params("arbitrary", "arbitrary"),
        name="in_proj",
    )(x, g, sc, sh, w, colscale)


def _compress_kernel(kv_ref, pos_ref, w1_ref, w2_ref, o_ref, tok_sc, c_sc):
    g, n, hd = o_ref.shape
    for gi in range(g):
        tok_sc[gi] = kv_ref[:, gi * hd:(gi + 1) * hd].astype(F32)
        for j in range(CMP_STRIDE):
            c_sc[gi * n:(gi + 1) * n, j * hd:(j + 1) * hd] = tok_sc[gi, pl.ds(j, n, stride=CMP_STRIDE), :]
    c = c_sc[...]
    lo = (c + pos_ref[0:1, :]).astype(BF16)
    hi = (c + pos_ref[1:2, :]).astype(BF16)
    first = _dot(lo, w1_ref[0])
    second = _dot(hi, w1_ref[1])
    hid = jnp.concatenate([first[gi * n:(gi + 1) * n] + pltpu.roll(second[gi * n:(gi + 1) * n], n - 1, axis=0)
                           for gi in range(g)], axis=0)
    out = _dot(jax.nn.gelu(hid).astype(BF16), w2_ref[...]).astype(o_ref.dtype)
    for gi in range(g):
        o_ref[gi] = out[gi * n:(gi + 1) * n]


def _compress(z, batch, seq, pos, w1, w2):
    g = NSA_KV_GROUPS
    hd = HEAD_DIM
    n = seq // CMP_STRIDE
    cw = CMP_STRIDE * hd
    kv0 = NSA_WIDTH // KV_WIDTH
    return pl.pallas_call(
        _compress_kernel,
        out_shape=jax.ShapeDtypeStruct((batch, 2, g, n, hd), BF16),
        grid=(batch, 2),
        in_specs=[
            pl.BlockSpec((seq, KV_WIDTH), lambda bi, s: (bi, kv0 + s)),
            pl.BlockSpec((None, 2, cw), lambda bi, s: (s, 0, 0)),
            pl.BlockSpec((None, 2, cw, hd), lambda bi, s: (s, 0, 0, 0)),
            pl.BlockSpec((None, hd, hd), lambda bi, s: (s, 0, 0)),
        ],
        out_specs=pl.BlockSpec((None, None, g, n, hd), lambda bi, s: (bi, s, 0, 0, 0)),
        scratch_shapes=[pltpu.VMEM((g, seq, hd), F32), pltpu.VMEM((g * n, cw), F32)],
        compiler_params=_params("parallel", "parallel"),
        name="compress",
    )(z, pos, w1, w2)


def _nsa_kernel(q_ref, ks_ref, vs_ref, kw_ref, vw_ref, kc_ref, vc_ref, zg_ref, o_ref,
                kaug_sc, vsaug_sc, vwaug_sc, vcaug_sc, ov_sc, bias_sc, qa_sc, gate_sc, s_sc, sw_sc, m_sc, acc_sc, out_sc,
                *, tq):
    hd = HEAD_DIM
    hpg = HEADS_PER_GROUP
    tk = tq
    n_blk = LANES // 4
    seq = ks_ref.shape[0]
    n_cmp = kc_ref.shape[0]
    grp = pl.program_id(1)
    qi = pl.program_id(2)
    t0 = qi * tq

    @pl.when(qi == 0)
    def _():
        kaug_sc[:, :hd] = ks_ref[...]
        key_blk = lax.shift_right_logical(lax.broadcasted_iota(jnp.int32, (seq, LANES), 0), SLC_LEN_LOG2)
        lane_blk = lax.broadcasted_iota(jnp.int32, (seq, LANES), 1) & (n_blk - 1)
        kaug_sc[:, hd:] = jnp.where(key_blk == lane_blk, 1.0, 0.0).astype(BF16)
        ones = jnp.ones((seq, LANES), BF16)
        vsaug_sc[:, :hd] = vs_ref[...]
        vsaug_sc[:, hd:] = ones
        vwaug_sc[:, :hd] = vw_ref[...]
        vwaug_sc[:, hd:] = ones
        vcaug_sc[:, :hd] = vc_ref[...]
        vcaug_sc[:, hd:] = jnp.ones((n_cmp, LANES), BF16)
        cstart = (lax.broadcasted_iota(jnp.int32, (n_blk, hpg * n_cmp), 1) & (n_cmp - 1)) * CMP_STRIDE
        bstart = lax.broadcasted_iota(jnp.int32, (n_blk, hpg * n_cmp), 0) * SLC_LEN
        ov_sc[...] = jnp.where((cstart < bstart + SLC_LEN) & (cstart + CMP_LEN > bstart), 1.0, 0.0).astype(BF16)
        r = lax.broadcasted_iota(jnp.int32, (tq, tk), 0)
        cidx = lax.broadcasted_iota(jnp.int32, (tq, tk), 1)
        bias_sc[0] = jnp.where(cidx <= r, 0.0, MASK_VALUE)
        bias_sc[1] = jnp.where(cidx > r, 0.0, MASK_VALUE)

    q = q_ref[...]
    qs = jnp.concatenate([q[:, h * hd:(h + 1) * hd] for h in range(hpg)], axis=0)
    qpos = t0 + (lax.broadcasted_iota(jnp.int32, (hpg * tq, 1), 0) & (tq - 1))

    def put_scores(dst, slot, k_ref, start, lhs):
        k_t = k_ref[pl.ds(pl.multiple_of(start, tq), tq), :]
        for h in range(hpg):
            dst[slot, h] = lax.dot_general(lhs(h), k_t, _NT, preferred_element_type=F32)

    def reset():
        m_sc[...] = jnp.full(m_sc.shape, MASK_VALUE, F32)
        acc_sc[...] = jnp.zeros(acc_sc.shape, F32)

    def fold_head(src, slot, h, v_t, bias):
        sc = src[slot, h]
        if bias is not None:
            sc = sc + bias
        m_prev = m_sc[h]
        m_new = jnp.maximum(m_prev, jnp.max(sc, axis=-1, keepdims=True))
        alpha = jnp.exp2(m_prev - m_new)
        pe = jnp.exp2(sc - jnp.concatenate([m_new] * (tq // LANES), axis=1))
        acc_sc[h] = jnp.concatenate([alpha, alpha], axis=1) * acc_sc[h] + _dot(pe.astype(BF16), v_t)
        m_sc[h] = m_new

    def fold(src, slot, v_ref, start, mask):
        v_t = v_ref[pl.ds(pl.multiple_of(start, tq), tq), :]
        bias = None if mask is None else bias_sc[mask]
        for h in range(hpg):
            fold_head(src, slot, h, v_t, bias)

    def finish(h):
        acc = acc_sc[h]
        return acc[:, :hd] / acc[:, hd:]

    q_head = lambda h: q[:, h * hd:(h + 1) * hd]
    s = lax.dot_general(qs, kc_ref[...], _NT, preferred_element_type=F32)
    st = lax.dot_general(kc_ref[...], qs, _NT, preferred_element_type=F32)
    put_scores(sw_sc, 0, kw_ref, t0, q_head)

    gt = jax.nn.sigmoid(pltpu.roll(zg_ref[...].astype(F32), (LANES - GATES_PER_GROUP * grp) & (LANES - 1), axis=1))

    for c in range(GATES_PER_GROUP):
        gate_sc[:, c * LANES:(c + 1) * LANES] = jnp.broadcast_to(gt[:, c:c + 1], (tq, LANES))

    def gate(h, branch):
        c = 3 * h + branch
        return gate_sc[:, c * LANES:(c + 1) * LANES]

    cmp_end = lax.broadcasted_iota(jnp.int32, (1, n_cmp), 1) * CMP_STRIDE + (CMP_LEN - 1)
    s = jnp.where(cmp_end <= qpos, s, -jnp.inf)
    mx = jnp.max(s, axis=-1, keepdims=True)
    mx = jnp.where(jnp.isfinite(mx), mx, 0.0)
    pe_cmp = jnp.exp2(s - mx).astype(BF16)

    end_t = lax.broadcasted_iota(jnp.int32, (n_cmp, 1), 0) * CMP_STRIDE + (CMP_LEN - 1)
    qpos_t = t0 + (lax.broadcasted_iota(jnp.int32, (1, hpg * tq), 1) & (tq - 1))
    st = jnp.where(end_t <= qpos_t, st, -jnp.inf)
    mt = jnp.max(st, axis=0, keepdims=True)
    mt = jnp.where(jnp.isfinite(mt), mt, 0.0)
    pt = jnp.exp2(st - mt)
    pt = pt / jnp.maximum(jnp.sum(pt, axis=0, keepdims=True), DENOM_FLOOR)
    p_heads = jnp.concatenate([pt[:, h * tq:(h + 1) * tq] for h in range(hpg)], axis=0).astype(BF16)
    imp = _dot(ov_sc[...], p_heads)
    acc = _dot(pe_cmp, vcaug_sc[...])
    put_scores(sw_sc, 1, kw_ref, jnp.maximum(t0 - tq, 0), q_head)

    blk = lax.broadcasted_iota(jnp.int32, (n_blk, tq), 0)
    cur = lax.shift_right_logical(t0 + lax.broadcasted_iota(jnp.int32, (n_blk, tq), 1), SLC_LEN_LOG2)
    valid = blk <= cur
    forced = (blk == 0) | (valid & (blk > cur - SLC_LOCAL))
    score = jnp.where(forced, imp + FORCE_BONUS, jnp.where(valid, imp, -1.0))
    key = lax.bitcast_convert_type(score, jnp.int32)
    rank = jnp.zeros((n_blk, tq), jnp.int32)
    for r in range(1, n_blk):
        other = pltpu.roll(key, r, axis=0)
        rank = rank + jnp.where(other + jnp.where(blk >= r, 1, 0) > key, 1, 0)
    sel_bias = jnp.where((rank < SLC_TOP_N) & valid, 0.0, MASK_VALUE)
    sel_rows = jnp.concatenate([sel_bias, jnp.zeros((LANES - n_blk, tq), F32)], axis=0).T.astype(BF16)
    for h in range(hpg):
        qa_sc[h] = jnp.concatenate([q_head(h), sel_rows], axis=1)
    o_cmp = acc[:, :hd] / jnp.maximum(acc[:, hd:], DENOM_FLOOR)
    for h in range(hpg):
        out_sc[:, h * hd:(h + 1) * hd] = gate(h, 0) * o_cmp[h * tq:(h + 1) * tq]

    assert WINDOW == tq
    reset()
    k_first = kaug_sc[0:tq, :]
    v_diag = vwaug_sc[pl.ds(pl.multiple_of(t0, tq), tq), :]
    for h in range(hpg):
        s_sc[0, h] = lax.dot_general(qa_sc[h], k_first, _NT, preferred_element_type=F32)
        fold_head(sw_sc, 0, h, v_diag, bias_sc[0])

    @pl.when(qi >= 1)
    def _():
        fold(sw_sc, 1, vwaug_sc, t0 - tq, 1)

    for h in range(hpg):
        out_sc[:, h * hd:(h + 1) * hd] += gate(h, 2) * finish(h)

    reset()

    def score_and_fold(put_slot, put_start, fold_slot, fold_start, mask):
        k_t = kaug_sc[pl.ds(pl.multiple_of(put_start, tq), tq), :]
        v_t = vsaug_sc[pl.ds(pl.multiple_of(fold_start, tq), tq), :]
        bias = None if mask is None else bias_sc[mask]
        for h in range(hpg):
            s_sc[put_slot, h] = lax.dot_general(qa_sc[h], k_t, _NT, preferred_element_type=F32)
            fold_head(s_sc, fold_slot, h, v_t, bias)

    def slc_pair(j, carry):
        e = 2 * j * tq
        score_and_fold(1, e + tq, 0, e, None)
        score_and_fold(0, e + 2 * tq, 1, e + tq, None)
        return carry

    lax.fori_loop(0, qi // 2, slc_pair, 0)

    @pl.when(qi % 2 == 1)
    def _():
        score_and_fold(1, t0, 0, t0 - tq, None)
        fold(s_sc, 1, vsaug_sc, t0, 0)

    @pl.when(qi % 2 == 0)
    def _():
        fold(s_sc, 0, vsaug_sc, t0, 0)

    for h in range(hpg):
        o_h = out_sc[:, h * hd:(h + 1) * hd] + gate(h, 1) * finish(h)
        o_ref[:, h * hd:(h + 1) * hd] = o_h.astype(o_ref.dtype)


def _nsa(z, kv_cmp, gates_col, batch, seq, tq=WINDOW):
    m = z.shape[0]
    assert gates_col % LANES == 0
    g = NSA_KV_GROUPS
    hd = HEAD_DIM
    hpg = HEADS_PER_GROUP
    n_cmp = kv_cmp.shape[3]
    nq = seq // tq
    qcols = NSA_WIDTH // hd
    assert (tq // 4) % SLC_LEN == 0 and n_cmp == LANES

    def kv_spec(slot):
        return pl.BlockSpec((seq, hd), lambda b, gi, i: (b, qcols + slot * g + gi))

    def cmp_spec(slot):
        return pl.BlockSpec((None, None, None, n_cmp, hd), lambda b, gi, i: (b, slot, gi, 0, 0))

    return pl.pallas_call(
        functools.partial(_nsa_kernel, tq=tq),
        out_shape=jax.ShapeDtypeStruct((m, NSA_WIDTH), BF16),
        grid=(batch, g, nq),
        in_specs=[
            pl.BlockSpec((tq, hpg * hd), lambda b, gi, i: (b * nq + i, gi)),
            kv_spec(2), kv_spec(3), kv_spec(4), kv_spec(5),
            cmp_spec(0), cmp_spec(1),
            pl.BlockSpec((tq, LANES), lambda b, gi, i: (b * nq + i, gates_col // LANES)),
        ],
        out_specs=pl.BlockSpec((tq, hpg * hd), lambda b, gi, i: (b * nq + i, gi)),
        scratch_shapes=[
            pltpu.VMEM((seq, 2 * hd), BF16),
            pltpu.VMEM((seq, 2 * hd), BF16),
            pltpu.VMEM((seq, 2 * hd), BF16),
            pltpu.VMEM((n_cmp, 2 * hd), BF16),
            pltpu.VMEM((LANES // 4, hpg * n_cmp), BF16),
            pltpu.VMEM((2, tq, tq), F32),
            pltpu.VMEM((hpg, tq, 2 * hd), BF16),
            pltpu.VMEM((tq, GATES_PER_GROUP * LANES), F32),
            pltpu.VMEM((2, hpg, tq, tq), F32),
            pltpu.VMEM((2, hpg, tq, tq), F32),
            pltpu.VMEM((hpg, tq, LANES), F32),
            pltpu.VMEM((hpg, tq, 2 * hd), F32),
            pltpu.VMEM((tq, hpg * hd), F32),
        ],
        compiler_params=_params("parallel", "parallel", "arbitrary"),
        name="nsa",
    )(z, z, z, z, z, kv_cmp, kv_cmp, z)


def _lru_kernel(lx_ref, ly_ref, cw_ref, cb_ref, wa_ref, ba_ref, wi_ref, bi_ref, lam_ref, o_ref, x_sc, a_sc, h_sc):
    seq, wt = lx_ref.shape
    x_sc[0:SUBLANES, :] = jnp.zeros((SUBLANES, wt), F32)
    x_sc[SUBLANES:, :] = lx_ref[...].astype(F32)
    taps = cw_ref.shape[0]
    u = cb_ref[...]
    for j in range(taps):
        back = taps - 1 - j
        u = u + x_sc[SUBLANES - back:SUBLANES - back + seq, :] * cw_ref[j:j + 1, :]
    ub = u.astype(BF16)
    r = jax.nn.sigmoid(_dot(ub, wa_ref[0]) + ba_ref[...])
    gate_i = jax.nn.sigmoid(_dot(ub, wi_ref[0]) + bi_ref[...])
    log_a = r * (LRU_C * jax.nn.log_sigmoid(lam_ref[...]))
    a = jnp.exp(log_a)
    y = 1.0 - a * a
    b = jnp.where(y > 0.0, y * lax.rsqrt(y), 0.0) * (gate_i * u)

    n_seg = SUBLANES
    seg = seq // n_seg
    stride = a_sc.shape[0] // n_seg
    for i in range(n_seg):
        a_sc[stride * i:stride * i + seg, :] = a[seg * i:seg * (i + 1), :]
        h_sc[stride * i:stride * i + seg, :] = b[seg * i:seg * (i + 1), :]

    def steps(jj, carry):
        h, p = carry
        for k in range(LRU_SCAN_UNROLL):
            rows = pl.ds(jj * LRU_SCAN_UNROLL + k, n_seg, stride=stride)
            a_j = a_sc[rows, :]
            h = a_j * h + h_sc[rows, :]
            p = a_j * p
            h_sc[rows, :] = h
            a_sc[rows, :] = p
        return h, p

    h_end, p_end = lax.fori_loop(0, seg // LRU_SCAN_UNROLL, steps,
                                 (jnp.zeros((n_seg, wt), F32), jnp.ones((n_seg, wt), F32)))
    enter = jnp.zeros((1, wt), F32)
    for i in range(n_seg):
        h_i = h_sc[stride * i:stride * i + seg, :] + a_sc[stride * i:stride * i + seg, :] * enter
        gated = jax.nn.gelu(ly_ref[seg * i:seg * (i + 1), :].astype(F32)) * h_i
        o_ref[seg * i:seg * (i + 1), :] = gated.astype(o_ref.dtype)
        enter = h_end[i:i + 1, :] + p_end[i:i + 1, :] * enter


def _lru(z, lx_col, ly_col, conv_w, conv_b, wa, ba, wi, bi, lam, batch, seq):
    m = z.shape[0]
    wt = LRU_BLOCK_DIM
    nw = LRU_WIDTH // wt
    assert lx_col % wt == 0 and ly_col % wt == 0
    taps = conv_w.shape[0]
    vec = pl.BlockSpec((1, wt), lambda b, j: (0, j))
    blk = pl.BlockSpec((1, wt, wt), lambda b, j: (j, 0, 0))
    return pl.pallas_call(
        _lru_kernel,
        out_shape=jax.ShapeDtypeStruct((m, LRU_WIDTH), BF16),
        grid=(batch, nw),
        in_specs=[
            pl.BlockSpec((seq, wt), lambda b, j: (b, lx_col // wt + j)),
            pl.BlockSpec((seq, wt), lambda b, j: (b, ly_col // wt + j)),
            pl.BlockSpec((taps, wt), lambda b, j: (0, j)),
            vec, blk, vec, blk, vec, vec,
        ],
        out_specs=pl.BlockSpec((seq, wt), lambda b, j: (b, j)),
        scratch_shapes=[pltpu.VMEM((SUBLANES + seq, wt), F32),
                        pltpu.VMEM((seq + SUBLANES * LRU_SEG_PAD, wt), F32),
                        pltpu.VMEM((seq + SUBLANES * LRU_SEG_PAD, wt), F32)],
        compiler_params=_params("parallel", "parallel"),
        name="lru",
    )(z, z, conv_w, conv_b, wa, ba, wi, bi, lam)


def _merge_kernel(oa_ref, ob_ref, pa_ref, pb_ref, ga_ref, gb_ref, o_ref):
    a = _dot(oa_ref[...], pa_ref[...])
    b = _dot(ob_ref[...], pb_ref[...])
    merged = jax.nn.sigmoid(ga_ref[...].astype(F32)) * a + jax.nn.sigmoid(gb_ref[...].astype(F32)) * b
    o_ref[...] = merged.astype(o_ref.dtype)


def _merge(o_a, o_b, proj_a, proj_b, layer, z, ga_col, gb_col, d_model, tm_target=1024, tn_target=1024):
    m, ka = o_a.shape
    kb = o_b.shape[1]
    tm = _tile(m, tm_target, BF16_ROWS)
    tn = _tile(d_model, tn_target)
    ga0 = ga_col // tn
    gb0 = gb_col // tn
    assert ga_col % tn == 0 and gb_col % tn == 0
    return pl.pallas_call(
        _merge_kernel,
        out_shape=jax.ShapeDtypeStruct((m, d_model), BF16),
        grid=(m // tm, d_model // tn),
        in_specs=[
            pl.BlockSpec((tm, ka), lambda i, j: (i, 0)),
            pl.BlockSpec((tm, kb), lambda i, j: (i, 0)),
            pl.BlockSpec((None, ka, tn), lambda i, j: (layer, 0, j)),
            pl.BlockSpec((None, kb, tn), lambda i, j: (layer, 0, j)),
            pl.BlockSpec((tm, tn), lambda i, j: (i, ga0 + j)),
            pl.BlockSpec((tm, tn), lambda i, j: (i, gb0 + j)),
        ],
        out_specs=pl.BlockSpec((tm, tn), lambda i, j: (i, j)),
        compiler_params=_params("parallel", "parallel"),
        name="merge",
    )(o_a, o_b, proj_a, proj_b, z, z)


def _matmul_residual_kernel(a_ref, w_ref, x_ref, g_ref, o_ref):
    o_ref[...] = x_ref[...] + g_ref[...] * _dot(a_ref[...], w_ref[...])


def _matmul_residual(a, w, layer, x, gate, seq, name, tm_target, tn_target):
    m, k = a.shape
    n = w.shape[-1]
    tm = _tile(seq, tm_target, BF16_ROWS)
    tn = _tile(n, tn_target)
    return pl.pallas_call(
        _matmul_residual_kernel,
        out_shape=jax.ShapeDtypeStruct((m, n), F32),
        grid=(n // tn, m // tm),
        in_specs=[
            pl.BlockSpec((tm, k), lambda j, i: (i, 0)),
            pl.BlockSpec((None, k, tn), lambda j, i: (layer, 0, j)),
            pl.BlockSpec((tm, tn), lambda j, i: (i, j)),
            pl.BlockSpec((None, 1, tn), lambda j, i: ((i * tm) // seq, 0, j)),
        ],
        out_specs=pl.BlockSpec((tm, tn), lambda j, i: (i, j)),
        compiler_params=_params("parallel", "parallel"),
        name=name,
    )(a, w, x, gate)


def _ffn_up_kernel(xn_ref, g_ref, scn_ref, shn_ref, wg_ref, wv_ref, cw_ref, cb_ref, o_ref, h_ref, inv_ref, *, seq, ahead):
    halo = BF16_ROWS
    tm = o_ref.shape[0]
    i = pl.program_id(0)
    j = pl.program_id(1)
    slot = i % 2

    @pl.when((i == 0) & (j == 0))
    def _():
        h_ref[0, 0:halo, :] = jnp.zeros((halo, h_ref.shape[2]), BF16)
        _store_norm_mod(h_ref.at[0], halo, xn_ref, g_ref, scn_ref, shn_ref, inv_ref)

    h = h_ref[slot]
    gate = _dot(h, wg_ref[...])
    val = _dot(h[halo:], wv_ref[...])
    seq_start = (i * tm) % seq == 0
    row = lax.broadcasted_iota(jnp.int32, (tm, 1), 0)
    taps = cw_ref.shape[0]
    conv = cb_ref[...]
    for k in range(taps):
        back = taps - 1 - k
        gk = gate[halo - back:halo - back + tm]
        if back:
            gk = jnp.where(row >= jnp.where(seq_start, back, 0), gk, 0.0)
        conv = conv + gk * cw_ref[k:k + 1, :]
    o_ref[...] = (jax.nn.gelu(conv) * val).astype(o_ref.dtype)

    r0 = pl.multiple_of(jnp.clip((j - 1) * ahead, 0, tm - ahead), BF16_ROWS)
    h_ref[1 - slot, pl.ds(halo + r0, ahead), :] = _norm_mod_rows(xn_ref[pl.ds(r0, ahead), :], g_ref, scn_ref, shn_ref)
    h_ref[1 - slot, 0:halo, :] = h_ref[slot, tm:tm + halo, :]


def _ffn_up(x, g, sc, sh, w_up, layer, conv_w, conv_b, seq, tm_target=1024, tn_target=768):
    m, d = x.shape
    d_ff = w_up.shape[-1] // 2
    tm = _tile(seq, tm_target, BF16_ROWS)
    tn = _tile(d_ff, min(tn_target, d_ff // 2))
    n_i, n_j = m // tm, d_ff // tn
    halo = BF16_ROWS
    taps = conv_w.shape[0]
    ahead_x = lambda i, j: (_ahead_tile(i, j, n_i), 0)
    ahead_b = lambda i, j: ((_ahead_tile(i, j, n_i) * tm) // seq, 0, 0)
    return pl.pallas_call(
        functools.partial(_ffn_up_kernel, seq=seq, ahead=_ahead_rows(tm, n_j)),
        out_shape=jax.ShapeDtypeStruct((m, d_ff), BF16),
        grid=(n_i, n_j),
        in_specs=[
            pl.BlockSpec((tm, d), ahead_x),
            pl.BlockSpec((1, d), lambda i, j: (0, 0)),
            pl.BlockSpec((None, 1, d), ahead_b),
            pl.BlockSpec((None, 1, d), ahead_b),
            pl.BlockSpec((None, d, tn), lambda i, j: (layer, 0, j)),
            pl.BlockSpec((None, d, tn), lambda i, j: (layer, 0, n_j + j)),
            pl.BlockSpec((taps, tn), lambda i, j: (0, j)),
            pl.BlockSpec((1, tn), lambda i, j: (0, j)),
        ],
        out_specs=pl.BlockSpec((tm, tn), lambda i, j: (i, j)),
        scratch_shapes=[pltpu.VMEM((2, halo + tm, d), BF16), pltpu.VMEM((tm, LANES), F32)],
        compiler_params=_params("arbitrary", "arbitrary"),
        name="ffn_up",
    )(x, g, sc, sh, w_up, w_up, conv_w, conv_b)


def _final_norm_kernel(x_ref, g_ref, o_ref):
    x = x_ref[...]
    inv = lax.rsqrt(jnp.mean(x * x, axis=-1, keepdims=True) + NORM_EPS)
    o_ref[...] = (x * inv) * g_ref[...]


def _final_norm(x, g, tm_target=512):
    m, d = x.shape
    tm = _tile(m, tm_target, SUBLANES)
    return pl.pallas_call(
        _final_norm_kernel,
        out_shape=jax.ShapeDtypeStruct((m, d), F32),
        grid=(m // tm,),
        in_specs=[pl.BlockSpec((tm, d), lambda i: (i, 0)), pl.BlockSpec((1, d), lambda i: (0, 0))],
        out_specs=pl.BlockSpec((tm, d), lambda i: (i, 0)),
        compiler_params=_params("parallel"),
        name="final_norm",
    )(x, g)


def _compress_weights(pos, w1, w2):
    half = CMP_LEN // 2
    return (pos.reshape(2, half * HEAD_DIM),
            w1.reshape(2, half * HEAD_DIM, HEAD_DIM).astype(BF16),
            w2.astype(BF16))


def kernel(x, c, ada_w, ada_b, norm1_g, w_in, cmp_pos_k, cmp_w1_k, cmp_w2_k, cmp_pos_v, cmp_w1_v, cmp_w2_v,
           lru_conv_w, lru_conv_b, lru_wa, lru_ba, lru_wi, lru_bi, lru_lambda, proj_a, proj_b, w_out, norm2_g,
           ffn_up, ffn_conv_w, ffn_conv_b, ffn_down, final_g):
    batch, seq, d = x.shape
    depth = ada_w.shape[0]
    m = batch * seq
    assert seq % SLC_LEN == 0 and seq // SLC_LEN == LANES // 4 and CMP_STRIDE * LANES == seq
    n_attn = NSA_WIDTH + 6 * KV_WIDTH
    n_rest = 2 * LRU_WIDTH + 2 * d
    n_z = -(-(n_attn + n_rest + N_GATES) // (4 * LANES)) * (4 * LANES)
    z_cols = dict(lx=n_attn, ly=n_attn + LRU_WIDTH, ga=n_attn + 2 * LRU_WIDTH, gb=n_attn + 2 * LRU_WIDTH + d,
                  gates=n_attn + n_rest)

    mod = _ada(c, ada_w, ada_b)
    xf = x.reshape(m, d)
    q_scale = jnp.concatenate([jnp.full((1, NSA_WIDTH), HEAD_DIM ** -0.5 * LOG2_E, F32), jnp.ones((1, n_z - NSA_WIDTH), F32)],
                              axis=1)
    w16 = w_in.astype(BF16)
    w_all = jnp.concatenate([w16[:, :, :n_attn], w16[:, :, n_attn + N_GATES:],
                             jnp.pad(w16[:, :, n_attn:n_attn + N_GATES],
                                     ((0, 0), (0, 0), (0, n_z - n_attn - n_rest - N_GATES)))], axis=2)
    proj_a, proj_b, w_out, ffn_up, ffn_down, lru_wa, lru_wi = [
        w.astype(BF16) for w in (proj_a, proj_b, w_out, ffn_up, ffn_down, lru_wa, lru_wi)]

    for l in range(depth):
        sh1, sc1, g1, sh2, sc2, g2 = [mod[l, :, k * d:(k + 1) * d].reshape(batch, 1, d) for k in range(6)]
        z = _in_proj(xf, norm1_g[l].reshape(1, d), sc1, sh1, w_all, l, q_scale, BF16, seq)

        pk, w1k, w2k = _compress_weights(cmp_pos_k[l], cmp_w1_k[l], cmp_w2_k[l])
        pv, w1v, w2v = _compress_weights(cmp_pos_v[l], cmp_w1_v[l], cmp_w2_v[l])
        kv_cmp = _compress(z, batch, seq, jnp.stack([pk, pv]), jnp.stack([w1k, w1v]), jnp.stack([w2k, w2v]))

        o_a = _nsa(z, kv_cmp, z_cols["gates"], batch, seq)
        o_b = _lru(z, z_cols["lx"], z_cols["ly"], lru_conv_w[l], lru_conv_b[l].reshape(1, -1), lru_wa[l],
                   lru_ba[l].reshape(1, -1), lru_wi[l], lru_bi[l].reshape(1, -1),
                   lru_lambda[l].reshape(1, -1), batch, seq)
        merged = _merge(o_a, o_b, proj_a, proj_b, l, z, z_cols["ga"], z_cols["gb"], d)
        xf = _matmul_residual(merged, w_out, l, xf, g1, seq, "out_proj", tm_target=512, tn_target=2048)

        act = _ffn_up(xf, norm2_g[l].reshape(1, d), sc2, sh2, ffn_up, l, ffn_conv_w[l],
                      ffn_conv_b[l].reshape(1, -1), seq)
        xf = _matmul_residual(act, ffn_down, l, xf, g2, seq, "ffn_down", tm_target=512, tn_target=1024)

    return _final_norm(xf, final_g.reshape(1, d)).reshape(batch, seq, d)
```

```python
import functools

import jax
import jax.numpy as jnp
from jax import lax
from jax.experimental import pallas as pl
from jax.experimental.pallas import tpu as pltpu

F32 = jnp.float32
BF16 = jnp.bfloat16

NSA_HEADS = 16
NSA_KV_GROUPS = 4
HEADS_PER_GROUP = NSA_HEADS // NSA_KV_GROUPS
HEAD_DIM = 128
NSA_WIDTH = NSA_HEADS * HEAD_DIM
KV_WIDTH = NSA_KV_GROUPS * HEAD_DIM
CMP_LEN = 32
CMP_STRIDE = 16
SLC_LEN = 64
SLC_LEN_LOG2 = SLC_LEN.bit_length() - 1
SLC_TOP_N = 16
SLC_LOCAL = 2
FORCE_BONUS = 1e4
WINDOW = 512
LRU_WIDTH = 2048
LRU_BLOCK_DIM = 128
LRU_C = 8.0
NORM_EPS = 1e-6
LRU_SEG_PAD = 8
LRU_SCAN_UNROLL = 8
N_GATES = 3 * NSA_HEADS
GATES_PER_GROUP = 3 * HEADS_PER_GROUP

LANES = 128
SUBLANES = 8
BF16_ROWS = 16
VMEM_LIMIT_BYTES = 56 * 1024 * 1024

DENOM_FLOOR = 1e-30
MASK_VALUE = -1e30
LOG2_E = 1.4426950408889634

_NT = (((1,), (1,)), ((), ()))


def _params(*sem):
    return pltpu.CompilerParams(dimension_semantics=sem, vmem_limit_bytes=VMEM_LIMIT_BYTES)


def _tile(n, target, quantum=LANES):
    best = None
    t = quantum
    while t <= min(n, target):
        if n % t == 0:
            best = t
        t += quantum
    assert best is not None, (n, target, quantum)
    return best


def _dot(a, b):
    return jnp.dot(a, b, preferred_element_type=F32)


def _store_norm_mod(h_ref, row0, x_ref, g_ref, sc_ref, sh_ref, inv_ref):
    rows, d = x_ref.shape
    x = x_ref[...]
    inv = lax.rsqrt(jnp.mean(x * x, axis=-1, keepdims=True) + NORM_EPS)
    inv_ref[0:rows, :] = jnp.broadcast_to(inv, (rows, LANES))
    gain = g_ref[...]
    scale = 1.0 + sc_ref[...]
    shift = sh_ref[...]

    def slab(i, carry):
        r0 = pl.multiple_of(i * BF16_ROWS, BF16_ROWS)
        inv_s = jnp.concatenate([inv_ref[pl.ds(r0, BF16_ROWS), :]] * (d // LANES), axis=1)
        h = ((x_ref[pl.ds(r0, BF16_ROWS), :] * inv_s) * gain) * scale + shift
        h_ref[pl.ds(row0 + r0, BF16_ROWS), :] = h.astype(BF16)
        return carry

    lax.fori_loop(0, rows // BF16_ROWS, slab, 0, unroll=4 if rows >= 4 * BF16_ROWS else 1)


def _norm_mod_rows(x, g_ref, sc_ref, sh_ref):
    inv = lax.rsqrt(jnp.mean(x * x, axis=-1, keepdims=True) + NORM_EPS)
    return (((x * inv) * g_ref[...]) * (1.0 + sc_ref[...]) + sh_ref[...]).astype(BF16)


def _ahead_rows(tm, n_steps):
    return min(tm, -(-tm // ((n_steps - 1) * BF16_ROWS)) * BF16_ROWS)


def _ahead_tile(i, j, n_tiles):
    return jnp.where((i == 0) & (j == 0), 0, jnp.minimum(i + 1, n_tiles - 1))


def _ada_kernel(c_ref, w_ref, b_ref, o_ref):
    c = c_ref[...]
    act = (c * jax.nn.sigmoid(c)).astype(BF16)
    o_ref[...] = _dot(act, w_ref[...].astype(BF16)) + b_ref[...]


def _ada(c, ada_w, ada_b):
    depth, d, n = ada_w.shape
    b = c.shape[0]
    tn = _tile(n, 1024)
    return pl.pallas_call(
        _ada_kernel,
        out_shape=jax.ShapeDtypeStruct((depth, b, n), F32),
        grid=(depth, n // tn),
        in_specs=[
            pl.BlockSpec((b, d), lambda l, j: (0, 0)),
            pl.BlockSpec((None, d, tn), lambda l, j: (l, 0, j)),
            pl.BlockSpec((None, 1, tn), lambda l, j: (l, 0, j)),
        ],
        out_specs=pl.BlockSpec((None, b, tn), lambda l, j: (l, 0, j)),
        compiler_params=_params("parallel", "parallel"),
        name="ada",
    )(c, ada_w, ada_b.reshape(depth, 1, n))


def _in_proj_kernel(xn_ref, g_ref, scn_ref, shn_ref, w_ref, cs_ref, o_ref, h_ref, inv_ref, *, ahead):
    i = pl.program_id(0)
    j = pl.program_id(1)
    tm = o_ref.shape[0]
    slot = i % 2

    @pl.when((i == 0) & (j == 0))
    def _():
        _store_norm_mod(h_ref.at[0], 0, xn_ref, g_ref, scn_ref, shn_ref, inv_ref)

    o_ref[...] = (_dot(h_ref[slot], w_ref[...]) * cs_ref[...]).astype(o_ref.dtype)
    r0 = pl.multiple_of(jnp.clip((j - 1) * ahead, 0, tm - ahead), BF16_ROWS)
    h_ref[1 - slot, pl.ds(r0, ahead), :] = _norm_mod_rows(xn_ref[pl.ds(r0, ahead), :], g_ref, scn_ref, shn_ref)


def _in_proj(x, g, sc, sh, w, layer, colscale, out_dtype, seq, tm_target=1024, tn_target=1536):
    m, d = x.shape
    n = w.shape[-1]
    tm = _tile(seq, tm_target, BF16_ROWS)
    tn = _tile(n, min(tn_target, n // 2))
    n_i, n_j = m // tm, n // tn
    ahead_x = lambda i, j: (_ahead_tile(i, j, n_i), 0)
    ahead_b = lambda i, j: ((_ahead_tile(i, j, n_i) * tm) // seq, 0, 0)
    return pl.pallas_call(
        functools.partial(_in_proj_kernel, ahead=_ahead_rows(tm, n_j)),
        out_shape=jax.ShapeDtypeStruct((m, n), out_dtype),
        grid=(n_i, n_j),
        in_specs=[
            pl.BlockSpec((tm, d), ahead_x),
            pl.BlockSpec((1, d), lambda i, j: (0, 0)),
            pl.BlockSpec((None, 1, d), ahead_b),
            pl.BlockSpec((None, 1, d), ahead_b),
            pl.BlockSpec((None, d, tn), lambda i, j: (layer, 0, j)),
            pl.BlockSpec((1, tn), lambda i, j: (0, j)),
        ],
        out_specs=pl.BlockSpec((tm, tn), lambda i, j: (i, j)),
        scratch_shapes=[pltpu.VMEM((2, tm, d), BF16), pltpu.VMEM((tm, LANES), F32)],
        compiler_params=_params("arbitrary", "arbitrary"),
        name="in_proj",
    )(x, g, sc, sh, w, colscale)


def _compress_kernel(kv_ref, pos_ref, w1_ref, w2_ref, o_ref, tok_sc, c_sc):
    g, n, hd = o_ref.shape
    for gi in range(g):
        tok_sc[gi] = kv_ref[:, gi * hd:(gi + 1) * hd].astype(F32)
        for j in range(CMP_STRIDE):
            c_sc[gi * n:(gi + 1) * n, j * hd:(j + 1) * hd] = tok_sc[gi, pl.ds(j, n, stride=CMP_STRIDE), :]
    c = c_sc[...]
    lo = (c + pos_ref[0:1, :]).astype(BF16)
    hi = (c + pos_ref[1:2, :]).astype(BF16)
    first = _dot(lo, w1_ref[0])
    second = _dot(hi, w1_ref[1])
    hid = jnp.concatenate([first[gi * n:(gi + 1) * n] + pltpu.roll(second[gi * n:(gi + 1) * n], n - 1, axis=0)
                           for gi in range(g)], axis=0)
    out = _dot(jax.nn.gelu(hid).astype(BF16), w2_ref[...]).astype(o_ref.dtype)
    for gi in range(g):
        o_ref[gi] = out[gi * n:(gi + 1) * n]


def _compress(z, batch, seq, pos, w1, w2):
    g = NSA_KV_GROUPS
    hd = HEAD_DIM
    n = seq // CMP_STRIDE
    cw = CMP_STRIDE * hd
    kv0 = NSA_WIDTH // KV_WIDTH
    return pl.pallas_call(
        _compress_kernel,
        out_shape=jax.ShapeDtypeStruct((batch, 2, g, n, hd), BF16),
        grid=(batch, 2),
        in_specs=[
            pl.BlockSpec((seq, KV_WIDTH), lambda bi, s: (bi, kv0 + s)),
            pl.BlockSpec((None, 2, cw), lambda bi, s: (s, 0, 0)),
            pl.BlockSpec((None, 2, cw, hd), lambda bi, s: (s, 0, 0, 0)),
            pl.BlockSpec((None, hd, hd), lambda bi, s: (s, 0, 0)),
        ],
        out_specs=pl.BlockSpec((None, None, g, n, hd), lambda bi, s: (bi, s, 0, 0, 0)),
        scratch_shapes=[pltpu.VMEM((g, seq, hd), F32), pltpu.VMEM((g * n, cw), F32)],
        compiler_params=_params("parallel", "parallel"),
        name="compress",
    )(z, pos, w1, w2)


def _nsa_kernel(q_ref, ks_ref, vs_ref, kw_ref, vw_ref, kc_ref, vc_ref, zg_ref, o_ref,
                kaug_sc, vsaug_sc, vwaug_sc, vcaug_sc, ov_sc, bias_sc, qa_sc, gate_sc, s_sc, sw_sc, m_sc, acc_sc, out_sc,
                *, tq):
    hd = HEAD_DIM
    hpg = HEADS_PER_GROUP
    tk = tq
    n_blk = LANES // 4
    seq = ks_ref.shape[0]
    n_cmp = kc_ref.shape[0]
    grp = pl.program_id(1)
    qi = pl.program_id(2)
    t0 = qi * tq

    @pl.when(qi == 0)
    def _():
        kaug_sc[:, :hd] = ks_ref[...]
        vsaug_sc[:, :hd] = vs_ref[...]
        vwaug_sc[:, :hd] = vw_ref[...]
        vcaug_sc[:, :hd] = vc_ref[...]

    @pl.when((pl.program_id(0) == 0) & (grp == 0) & (qi == 0))
    def _():
        key_blk = lax.shift_right_logical(lax.broadcasted_iota(jnp.int32, (seq, LANES), 0), SLC_LEN_LOG2)
        lane_blk = lax.broadcasted_iota(jnp.int32, (seq, LANES), 1) & (n_blk - 1)
        kaug_sc[:, hd:] = jnp.where(key_blk == lane_blk, 1.0, 0.0).astype(BF16)
        ones = jnp.ones((seq, LANES), BF16)
        vsaug_sc[:, hd:] = ones
        vwaug_sc[:, hd:] = ones
        vcaug_sc[:, hd:] = jnp.ones((n_cmp, LANES), BF16)
        cstart = (lax.broadcasted_iota(jnp.int32, (n_blk, hpg * n_cmp), 1) & (n_cmp - 1)) * CMP_STRIDE
        bstart = lax.broadcasted_iota(jnp.int32, (n_blk, hpg * n_cmp), 0) * SLC_LEN
        ov_sc[...] = jnp.where((cstart < bstart + SLC_LEN) & (cstart + CMP_LEN > bstart), 1.0, 0.0).astype(BF16)
        r = lax.broadcasted_iota(jnp.int32, (tq, tk), 0)
        cidx = lax.broadcasted_iota(jnp.int32, (tq, tk), 1)
        bias_sc[0] = jnp.where(cidx <= r, 0.0, MASK_VALUE)
        bias_sc[1] = jnp.where(cidx > r, 0.0, MASK_VALUE)

    q = q_ref[...]
    qs = jnp.concatenate([q[:, h * hd:(h + 1) * hd] for h in range(hpg)], axis=0)
    qpos = t0 + (lax.broadcasted_iota(jnp.int32, (hpg * tq, 1), 0) & (tq - 1))

    def put_scores(dst, slot, k_ref, start, lhs):
        k_t = k_ref[pl.ds(pl.multiple_of(start, tq), tq), :]
        for h in range(hpg):
            dst[slot, h] = lax.dot_general(lhs(h), k_t, _NT, preferred_element_type=F32)

    def reset():
        m_sc[...] = jnp.full(m_sc.shape, MASK_VALUE, F32)
        acc_sc[...] = jnp.zeros(acc_sc.shape, F32)

    def fold_head(src, slot, h, v_t, bias):
        sc = src[slot, h]
        if bias is not None:
            sc = sc + bias
        m_prev = m_sc[h]
        m_new = jnp.maximum(m_prev, jnp.max(sc, axis=-1, keepdims=True))
        alpha = jnp.exp2(m_prev - m_new)
        pe = jnp.exp2(sc - jnp.concatenate([m_new] * (tq // LANES), axis=1))
        acc_sc[h] = jnp.concatenate([alpha, alpha], axis=1) * acc_sc[h] + _dot(pe.astype(BF16), v_t)
        m_sc[h] = m_new

    def fold(src, slot, v_ref, start, mask):
        v_t = v_ref[pl.ds(pl.multiple_of(start, tq), tq), :]
        bias = None if mask is None else bias_sc[mask]
        for h in range(hpg):
            fold_head(src, slot, h, v_t, bias)

    def finish(h):
        acc = acc_sc[h]
        return acc[:, :hd] / acc[:, hd:]

    q_head = lambda h: q[:, h * hd:(h + 1) * hd]
    s = lax.dot_general(qs, kc_ref[...], _NT, preferred_element_type=F32)
    st = lax.dot_general(kc_ref[...], qs, _NT, preferred_element_type=F32)
    put_scores(sw_sc, 0, kw_ref, t0, q_head)

    gt = jax.nn.sigmoid(pltpu.roll(zg_ref[...].astype(F32), (LANES - GATES_PER_GROUP * grp) & (LANES - 1), axis=1))

    for c in range(GATES_PER_GROUP):
        gate_sc[:, c * LANES:(c + 1) * LANES] = jnp.broadcast_to(gt[:, c:c + 1], (tq, LANES))

    def gate(h, branch):
        c = 3 * h + branch
        return gate_sc[:, c * LANES:(c + 1) * LANES]

    cmp_end = lax.broadcasted_iota(jnp.int32, (1, n_cmp), 1) * CMP_STRIDE + (CMP_LEN - 1)
    s = jnp.where(cmp_end <= qpos, s, -jnp.inf)
    mx = jnp.max(s, axis=-1, keepdims=True)
    mx = jnp.where(jnp.isfinite(mx), mx, 0.0)
    pe_cmp = jnp.exp2(s - mx).astype(BF16)

    end_t = lax.broadcasted_iota(jnp.int32, (n_cmp, 1), 0) * CMP_STRIDE + (CMP_LEN - 1)
    qpos_t = t0 + (lax.broadcasted_iota(jnp.int32, (1, hpg * tq), 1) & (tq - 1))
    st = jnp.where(end_t <= qpos_t, st, -jnp.inf)
    mt = jnp.max(st, axis=0, keepdims=True)
    mt = jnp.where(jnp.isfinite(mt), mt, 0.0)
    pt = jnp.exp2(st - mt)
    pt = pt / jnp.maximum(jnp.sum(pt, axis=0, keepdims=True), DENOM_FLOOR)
    p_heads = jnp.concatenate([pt[:, h * tq:(h + 1) * tq] for h in range(hpg)], axis=0).astype(BF16)
    imp = _dot(ov_sc[...], p_heads)
    acc = _dot(pe_cmp, vcaug_sc[...])
    put_scores(sw_sc, 1, kw_ref, jnp.maximum(t0 - tq, 0), q_head)

    blk = lax.broadcasted_iota(jnp.int32, (n_blk, tq), 0)
    cur = lax.shift_right_logical(t0 + lax.broadcasted_iota(jnp.int32, (n_blk, tq), 1), SLC_LEN_LOG2)
    valid = blk <= cur
    forced = (blk == 0) | (valid & (blk > cur - SLC_LOCAL))
    score = jnp.where(forced, imp + FORCE_BONUS, jnp.where(valid, imp, -1.0))
    key = lax.bitcast_convert_type(score, jnp.int32)
    rank = jnp.zeros((n_blk, tq), jnp.int32)
    for r in range(1, n_blk):
        other = pltpu.roll(key, r, axis=0)
        rank = rank + jnp.where(other + jnp.where(blk >= r, 1, 0) > key, 1, 0)
    sel_bias = jnp.where((rank < SLC_TOP_N) & valid, 0.0, MASK_VALUE)
    sel_rows = jnp.concatenate([sel_bias, jnp.zeros((LANES - n_blk, tq), F32)], axis=0).T.astype(BF16)
    for h in range(hpg):
        qa_sc[h] = jnp.concatenate([q_head(h), sel_rows], axis=1)
    o_cmp = acc[:, :hd] / jnp.maximum(acc[:, hd:], DENOM_FLOOR)
    for h in range(hpg):
        out_sc[:, h * hd:(h + 1) * hd] = gate(h, 0) * o_cmp[h * tq:(h + 1) * tq]

    assert WINDOW == tq
    reset()
    k_first = kaug_sc[0:tq, :]
    v_diag = vwaug_sc[pl.ds(pl.multiple_of(t0, tq), tq), :]
    for h in range(hpg):
        s_sc[0, h] = lax.dot_general(qa_sc[h], k_first, _NT, preferred_element_type=F32)
        fold_head(sw_sc, 0, h, v_diag, bias_sc[0])

    @pl.when(qi >= 1)
    def _():
        fold(sw_sc, 1, vwaug_sc, t0 - tq, 1)

    for h in range(hpg):
        out_sc[:, h * hd:(h + 1) * hd] += gate(h, 2) * finish(h)

    reset()

    def score_and_fold(put_slot, put_start, fold_slot, fold_start, mask):
        k_t = kaug_sc[pl.ds(pl.multiple_of(put_start, tq), tq), :]
        v_t = vsaug_sc[pl.ds(pl.multiple_of(fold_start, tq), tq), :]
        bias = None if mask is None else bias_sc[mask]
        for h in range(hpg):
            s_sc[put_slot, h] = lax.dot_general(qa_sc[h], k_t, _NT, preferred_element_type=F32)
            fold_head(s_sc, fold_slot, h, v_t, bias)

    def slc_pair(j, carry):
        e = 2 * j * tq
        score_and_fold(1, e + tq, 0, e, None)
        score_and_fold(0, e + 2 * tq, 1, e + tq, None)
        return carry

    lax.fori_loop(0, qi // 2, slc_pair, 0)

    @pl.when(qi % 2 == 1)
    def _():
        score_and_fold(1, t0, 0, t0 - tq, None)
        fold(s_sc, 1, vsaug_sc, t0, 0)

    @pl.when(qi % 2 == 0)
    def _():
        fold(s_sc, 0, vsaug_sc, t0, 0)

    for h in range(hpg):
        o_h = out_sc[:, h * hd:(h + 1) * hd] + gate(h, 1) * finish(h)
        o_ref[:, h * hd:(h + 1) * hd] = o_h.astype(o_ref.dtype)


def _nsa(z, kv_cmp, gates_col, batch, seq, tq=WINDOW):
    m = z.shape[0]
    assert gates_col % LANES == 0
    g = NSA_KV_GROUPS
    hd = HEAD_DIM
    hpg = HEADS_PER_GROUP
    n_cmp = kv_cmp.shape[3]
    nq = seq // tq
    qcols = NSA_WIDTH // hd
    assert (tq // 4) % SLC_LEN == 0 and n_cmp == LANES

    def kv_spec(slot):
        return pl.BlockSpec((seq, hd), lambda b, gi, i: (b, qcols + slot * g + gi))

    def cmp_spec(slot):
        return pl.BlockSpec((None, None, None, n_cmp, hd), lambda b, gi, i: (b, slot, gi, 0, 0))

    return pl.pallas_call(
        functools.partial(_nsa_kernel, tq=tq),
        out_shape=jax.ShapeDtypeStruct((m, NSA_WIDTH), BF16),
        grid=(batch, g, nq),
        in_specs=[
            pl.BlockSpec((tq, hpg * hd), lambda b, gi, i: (b * nq + i, gi)),
            kv_spec(2), kv_spec(3), kv_spec(4), kv_spec(5),
            cmp_spec(0), cmp_spec(1),
            pl.BlockSpec((tq, LANES), lambda b, gi, i: (b * nq + i, gates_col // LANES)),
        ],
        out_specs=pl.BlockSpec((tq, hpg * hd), lambda b, gi, i: (b * nq + i, gi)),
        scratch_shapes=[
            pltpu.VMEM((seq, 2 * hd), BF16),
            pltpu.VMEM((seq, 2 * hd), BF16),
            pltpu.VMEM((seq, 2 * hd), BF16),
            pltpu.VMEM((n_cmp, 2 * hd), BF16),
            pltpu.VMEM((LANES // 4, hpg * n_cmp), BF16),
            pltpu.VMEM((2, tq, tq), F32),
            pltpu.VMEM((hpg, tq, 2 * hd), BF16),
            pltpu.VMEM((tq, GATES_PER_GROUP * LANES), F32),
            pltpu.VMEM((2, hpg, tq, tq), F32),
            pltpu.VMEM((2, hpg, tq, tq), F32),
            pltpu.VMEM((hpg, tq, LANES), F32),
            pltpu.VMEM((hpg, tq, 2 * hd), F32),
            pltpu.VMEM((tq, hpg * hd), F32),
        ],
        compiler_params=_params("arbitrary", "arbitrary", "arbitrary"),
        name="nsa",
    )(z, z, z, z, z, kv_cmp, kv_cmp, z)


def _lru_kernel(lx_ref, ly_ref, cw_ref, cb_ref, wa_ref, ba_ref, wi_ref, bi_ref, lam_ref, o_ref, x_sc, a_sc, h_sc):
    seq, wt = lx_ref.shape
    x_sc[0:SUBLANES, :] = jnp.zeros((SUBLANES, wt), F32)
    x_sc[SUBLANES:, :] = lx_ref[...].astype(F32)
    taps = cw_ref.shape[0]
    u = cb_ref[...]
    for j in range(taps):
        back = taps - 1 - j
        u = u + x_sc[SUBLANES - back:SUBLANES - back + seq, :] * cw_ref[j:j + 1, :]
    ub = u.astype(BF16)
    r = jax.nn.sigmoid(_dot(ub, wa_ref[0]) + ba_ref[...])
    gate_i = jax.nn.sigmoid(_dot(ub, wi_ref[0]) + bi_ref[...])
    log_a = r * (LRU_C * jax.nn.log_sigmoid(lam_ref[...]))
    a = jnp.exp(log_a)
    y = 1.0 - a * a
    b = jnp.where(y > 0.0, y * lax.rsqrt(y), 0.0) * (gate_i * u)

    n_seg = SUBLANES
    seg = seq // n_seg
    stride = a_sc.shape[0] // n_seg
    for i in range(n_seg):
        a_sc[stride * i:stride * i + seg, :] = a[seg * i:seg * (i + 1), :]
        h_sc[stride * i:stride * i + seg, :] = b[seg * i:seg * (i + 1), :]

    def steps(jj, carry):
        h, p = carry
        for k in range(LRU_SCAN_UNROLL):
            rows = pl.ds(jj * LRU_SCAN_UNROLL + k, n_seg, stride=stride)
            a_j = a_sc[rows, :]
            h = a_j * h + h_sc[rows, :]
            p = a_j * p
            h_sc[rows, :] = h
            a_sc[rows, :] = p
        return h, p

    h_end, p_end = lax.fori_loop(0, seg // LRU_SCAN_UNROLL, steps,
                                 (jnp.zeros((n_seg, wt), F32), jnp.ones((n_seg, wt), F32)))
    enter = jnp.zeros((1, wt), F32)
    for i in range(n_seg):
        h_i = h_sc[stride * i:stride * i + seg, :] + a_sc[stride * i:stride * i + seg, :] * enter
        gated = jax.nn.gelu(ly_ref[seg * i:seg * (i + 1), :].astype(F32)) * h_i
        o_ref[seg * i:seg * (i + 1), :] = gated.astype(o_ref.dtype)
        enter = h_end[i:i + 1, :] + p_end[i:i + 1, :] * enter


def _lru(z, lx_col, ly_col, conv_w, conv_b, wa, ba, wi, bi, lam, batch, seq):
    m = z.shape[0]
    wt = LRU_BLOCK_DIM
    nw = LRU_WIDTH // wt
    assert lx_col % wt == 0 and ly_col % wt == 0
    taps = conv_w.shape[0]
    vec = pl.BlockSpec((1, wt), lambda b, j: (0, j))
    blk = pl.BlockSpec((1, wt, wt), lambda b, j: (j, 0, 0))
    return pl.pallas_call(
        _lru_kernel,
        out_shape=jax.ShapeDtypeStruct((m, LRU_WIDTH), BF16),
        grid=(batch, nw),
        in_specs=[
            pl.BlockSpec((seq, wt), lambda b, j: (b, lx_col // wt + j)),
            pl.BlockSpec((seq, wt), lambda b, j: (b, ly_col // wt + j)),
            pl.BlockSpec((taps, wt), lambda b, j: (0, j)),
            vec, blk, vec, blk, vec, vec,
        ],
        out_specs=pl.BlockSpec((seq, wt), lambda b, j: (b, j)),
        scratch_shapes=[pltpu.VMEM((SUBLANES + seq, wt), F32),
                        pltpu.VMEM((seq + SUBLANES * LRU_SEG_PAD, wt), F32),
                        pltpu.VMEM((seq + SUBLANES * LRU_SEG_PAD, wt), F32)],
        compiler_params=_params("parallel", "parallel"),
        name="lru",
    )(z, z, conv_w, conv_b, wa, ba, wi, bi, lam)


def _merge_kernel(oa_ref, ob_ref, pa_ref, pb_ref, ga_ref, gb_ref, o_ref):
    a = _dot(oa_ref[...], pa_ref[...])
    b = _dot(ob_ref[...], pb_ref[...])
    merged = jax.nn.sigmoid(ga_ref[...].astype(F32)) * a + jax.nn.sigmoid(gb_ref[...].astype(F32)) * b
    o_ref[...] = merged.astype(o_ref.dtype)


def _merge(o_a, o_b, proj_a, proj_b, layer, z, ga_col, gb_col, d_model, tm_target=1024, tn_target=1024):
    m, ka = o_a.shape
    kb = o_b.shape[1]
    tm = _tile(m, tm_target, BF16_ROWS)
    tn = _tile(d_model, tn_target)
    ga0 = ga_col // tn
    gb0 = gb_col // tn
    assert ga_col % tn == 0 and gb_col % tn == 0
    return pl.pallas_call(
        _merge_kernel,
        out_shape=jax.ShapeDtypeStruct((m, d_model), BF16),
        grid=(m // tm, d_model // tn),
        in_specs=[
            pl.BlockSpec((tm, ka), lambda i, j: (i, 0)),
            pl.BlockSpec((tm, kb), lambda i, j: (i, 0)),
            pl.BlockSpec((None, ka, tn), lambda i, j: (layer, 0, j)),
            pl.BlockSpec((None, kb, tn), lambda i, j: (layer, 0, j)),
            pl.BlockSpec((tm, tn), lambda i, j: (i, ga0 + j)),
            pl.BlockSpec((tm, tn), lambda i, j: (i, gb0 + j)),
        ],
        out_specs=pl.BlockSpec((tm, tn), lambda i, j: (i, j)),
        compiler_params=_params("parallel", "parallel"),
        name="merge",
    )(o_a, o_b, proj_a, proj_b, z, z)


def _matmul_residual_kernel(a_ref, w_ref, x_ref, g_ref, o_ref):
    o_ref[...] = x_ref[...] + g_ref[...] * _dot(a_ref[...], w_ref[...])


def _matmul_residual(a, w, layer, x, gate, seq, name, tm_target, tn_target):
    m, k = a.shape
    n = w.shape[-1]
    tm = _tile(seq, tm_target, BF16_ROWS)
    tn = _tile(n, tn_target)
    return pl.pallas_call(
        _matmul_residual_kernel,
        out_shape=jax.ShapeDtypeStruct((m, n), F32),
        grid=(n // tn, m // tm),
        in_specs=[
            pl.BlockSpec((tm, k), lambda j, i: (i, 0)),
            pl.BlockSpec((None, k, tn), lambda j, i: (layer, 0, j)),
            pl.BlockSpec((tm, tn), lambda j, i: (i, j)),
            pl.BlockSpec((None, 1, tn), lambda j, i: ((i * tm) // seq, 0, j)),
        ],
        out_specs=pl.BlockSpec((tm, tn), lambda j, i: (i, j)),
        compiler_params=_params("parallel", "parallel"),
        name=name,
    )(a, w, x, gate)


def _ffn_up_kernel(xn_ref, g_ref, scn_ref, shn_ref, wg_ref, wv_ref, cw_ref, cb_ref, o_ref, h_ref, inv_ref, *, seq, ahead):
    halo = BF16_ROWS
    tm = o_ref.shape[0]
    i = pl.program_id(0)
    j = pl.program_id(1)
    slot = i % 2

    @pl.when((i == 0) & (j == 0))
    def _():
        h_ref[0, 0:halo, :] = jnp.zeros((halo, h_ref.shape[2]), BF16)
        _store_norm_mod(h_ref.at[0], halo, xn_ref, g_ref, scn_ref, shn_ref, inv_ref)

    h = h_ref[slot]
    gate = _dot(h, wg_ref[...])
    val = _dot(h[halo:], wv_ref[...])
    seq_start = (i * tm) % seq == 0
    row = lax.broadcasted_iota(jnp.int32, (tm, 1), 0)
    taps = cw_ref.shape[0]
    conv = cb_ref[...]
    for k in range(taps):
        back = taps - 1 - k
        gk = gate[halo - back:halo - back + tm]
        if back:
            gk = jnp.where(row >= jnp.where(seq_start, back, 0), gk, 0.0)
        conv = conv + gk * cw_ref[k:k + 1, :]
    o_ref[...] = (jax.nn.gelu(conv) * val).astype(o_ref.dtype)

    r0 = pl.multiple_of(jnp.clip((j - 1) * ahead, 0, tm - ahead), BF16_ROWS)
    h_ref[1 - slot, pl.ds(halo + r0, ahead), :] = _norm_mod_rows(xn_ref[pl.ds(r0, ahead), :], g_ref, scn_ref, shn_ref)
    h_ref[1 - slot, 0:halo, :] = h_ref[slot, tm:tm + halo, :]


def _ffn_up(x, g, sc, sh, w_up, layer, conv_w, conv_b, seq, tm_target=1024, tn_target=768):
    m, d = x.shape
    d_ff = w_up.shape[-1] // 2
    tm = _tile(seq, tm_target, BF16_ROWS)
    tn = _tile(d_ff, min(tn_target, d_ff // 2))
    n_i, n_j = m // tm, d_ff // tn
    halo = BF16_ROWS
    taps = conv_w.shape[0]
    ahead_x = lambda i, j: (_ahead_tile(i, j, n_i), 0)
    ahead_b = lambda i, j: ((_ahead_tile(i, j, n_i) * tm) // seq, 0, 0)
    return pl.pallas_call(
        functools.partial(_ffn_up_kernel, seq=seq, ahead=_ahead_rows(tm, n_j)),
        out_shape=jax.ShapeDtypeStruct((m, d_ff), BF16),
        grid=(n_i, n_j),
        in_specs=[
            pl.BlockSpec((tm, d), ahead_x),
            pl.BlockSpec((1, d), lambda i, j: (0, 0)),
            pl.BlockSpec((None, 1, d), ahead_b),
            pl.BlockSpec((None, 1, d), ahead_b),
            pl.BlockSpec((None, d, tn), lambda i, j: (layer, 0, j)),
            pl.BlockSpec((None, d, tn), lambda i, j: (layer, 0, n_j + j)),
            pl.BlockSpec((taps, tn), lambda i, j: (0, j)),
            pl.BlockSpec((1, tn), lambda i, j: (0, j)),
        ],
        out_specs=pl.BlockSpec((tm, tn), lambda i, j: (i, j)),
        scratch_shapes=[pltpu.VMEM((2, halo + tm, d), BF16), pltpu.VMEM((tm, LANES), F32)],
        compiler_params=_params("arbitrary", "arbitrary"),
        name="ffn_up",
    )(x, g, sc, sh, w_up, w_up, conv_w, conv_b)


def _final_norm_kernel(x_ref, g_ref, o_ref):
    x = x_ref[...]
    inv = lax.rsqrt(jnp.mean(x * x, axis=-1, keepdims=True) + NORM_EPS)
    o_ref[...] = (x * inv) * g_ref[...]


def _final_norm(x, g, tm_target=512):
    m, d = x.shape
    tm = _tile(m, tm_target, SUBLANES)
    return pl.pallas_call(
        _final_norm_kernel,
        out_shape=jax.ShapeDtypeStruct((m, d), F32),
        grid=(m // tm,),
        in_specs=[pl.BlockSpec((tm, d), lambda i: (i, 0)), pl.BlockSpec((1, d), lambda i: (0, 0))],
        out_specs=pl.BlockSpec((tm, d), lambda i: (i, 0)),
        compiler_params=_params("parallel"),
        name="final_norm",
    )(x, g)


def _compress_weights(pos, w1, w2):
    half = CMP_LEN // 2
    return (pos.reshape(2, half * HEAD_DIM),
            w1.reshape(2, half * HEAD_DIM, HEAD_DIM).astype(BF16),
            w2.astype(BF16))


def kernel(x, c, ada_w, ada_b, norm1_g, w_in, cmp_pos_k, cmp_w1_k, cmp_w2_k, cmp_pos_v, cmp_w1_v, cmp_w2_v,
           lru_conv_w, lru_conv_b, lru_wa, lru_ba, lru_wi, lru_bi, lru_lambda, proj_a, proj_b, w_out, norm2_g,
           ffn_up, ffn_conv_w, ffn_conv_b, ffn_down, final_g):
    batch, seq, d = x.shape
    depth = ada_w.shape[0]
    m = batch * seq
    assert seq % SLC_LEN == 0 and seq // SLC_LEN == LANES // 4 and CMP_STRIDE * LANES == seq
    n_attn = NSA_WIDTH + 6 * KV_WIDTH
    n_rest = 2 * LRU_WIDTH + 2 * d
    n_z = -(-(n_attn + n_rest + N_GATES) // (4 * LANES)) * (4 * LANES)
    z_cols = dict(lx=n_attn, ly=n_attn + LRU_WIDTH, ga=n_attn + 2 * LRU_WIDTH, gb=n_attn + 2 * LRU_WIDTH + d,
                  gates=n_attn + n_rest)

    mod = _ada(c, ada_w, ada_b)
    xf = x.reshape(m, d)
    q_scale = jnp.concatenate([jnp.full((1, NSA_WIDTH), HEAD_DIM ** -0.5 * LOG2_E, F32), jnp.ones((1, n_z - NSA_WIDTH), F32)],
                              axis=1)
    w_all = jnp.concatenate([w_in[:, :, :n_attn], w_in[:, :, n_attn + N_GATES:],
                             jnp.pad(w_in[:, :, n_attn:n_attn + N_GATES],
                                     ((0, 0), (0, 0), (0, n_z - n_attn - n_rest - N_GATES)))], axis=2).astype(BF16)
    proj_a, proj_b, w_out, ffn_up, ffn_down, lru_wa, lru_wi = [
        w.astype(BF16) for w in (proj_a, proj_b, w_out, ffn_up, ffn_down, lru_wa, lru_wi)]

    for l in range(depth):
        sh1, sc1, g1, sh2, sc2, g2 = [mod[l, :, k * d:(k + 1) * d].reshape(batch, 1, d) for k in range(6)]
        z = _in_proj(xf, norm1_g[l].reshape(1, d), sc1, sh1, w_all, l, q_scale, BF16, seq)

        pk, w1k, w2k = _compress_weights(cmp_pos_k[l], cmp_w1_k[l], cmp_w2_k[l])
        pv, w1v, w2v = _compress_weights(cmp_pos_v[l], cmp_w1_v[l], cmp_w2_v[l])
        kv_cmp = _compress(z, batch, seq, jnp.stack([pk, pv]), jnp.stack([w1k, w1v]), jnp.stack([w2k, w2v]))

        o_a = _nsa(z, kv_cmp, z_cols["gates"], batch, seq)
        o_b = _lru(z, z_cols["lx"], z_cols["ly"], lru_conv_w[l], lru_conv_b[l].reshape(1, -1), lru_wa[l],
                   lru_ba[l].reshape(1, -1), lru_wi[l], lru_bi[l].reshape(1, -1),
                   lru_lambda[l].reshape(1, -1), batch, seq)
        merged = _merge(o_a, o_b, proj_a, proj_b, l, z, z_cols["ga"], z_cols["gb"], d)
        xf = _matmul_residual(merged, w_out, l, xf, g1, seq, "out_proj", tm_target=512, tn_target=2048)

        act = _ffn_up(xf, norm2_g[l].reshape(1, d), sc2, sh2, ffn_up, l, ffn_conv_w[l],
                      ffn_conv_b[l].reshape(1, -1), seq)
        xf = _matmul_residual(act, ffn_down, l, xf, g2, seq, "ffn_down", tm_target=512, tn_target=1024)

    return _final_norm(xf, final_g.reshape(1, d)).reshape(batch, seq, d)
```

```python
import functools

import jax
import jax.numpy as jnp
from jax import lax
from jax.experimental import pallas as pl
from jax.experimental.pallas import tpu as pltpu

F32 = jnp.float32
BF16 = jnp.bfloat16

NSA_HEADS = 16
NSA_KV_GROUPS = 4
HEADS_PER_GROUP = NSA_HEADS // NSA_KV_GROUPS
HEAD_DIM = 128
NSA_WIDTH = NSA_HEADS * HEAD_DIM
KV_WIDTH = NSA_KV_GROUPS * HEAD_DIM
CMP_LEN = 32
CMP_STRIDE = 16
SLC_LEN = 64
SLC_LEN_LOG2 = SLC_LEN.bit_length() - 1
SLC_TOP_N = 16
SLC_LOCAL = 2
FORCE_BONUS = 1e4
WINDOW = 512
LRU_WIDTH = 2048
LRU_BLOCK_DIM = 128
LRU_C = 8.0
NORM_EPS = 1e-6
LRU_SEG_PAD = 8
LRU_SCAN_UNROLL = 8
N_GATES = 3 * NSA_HEADS
GATES_PER_GROUP = 3 * HEADS_PER_GROUP

LANES = 128
SUBLANES = 8
BF16_ROWS = 16
VMEM_LIMIT_BYTES = 56 * 1024 * 1024

DENOM_FLOOR = 1e-30
MASK_VALUE = -1e30
LOG2_E = 1.4426950408889634

_NT = (((1,), (1,)), ((), ()))


def _params(*sem):
    return pltpu.CompilerParams(dimension_semantics=sem, vmem_limit_bytes=VMEM_LIMIT_BYTES)


def _tile(n, target, quantum=LANES):
    best = None
    t = quantum
    while t <= min(n, target):
        if n % t == 0:
            best = t
        t += quantum
    assert best is not None, (n, target, quantum)
    return best


def _dot(a, b):
    return jnp.dot(a, b, preferred_element_type=F32)


def _store_norm_mod(h_ref, row0, x_ref, g_ref, sc_ref, sh_ref, inv_ref):
    rows, d = x_ref.shape
    x = x_ref[...]
    inv = lax.rsqrt(jnp.mean(x * x, axis=-1, keepdims=True) + NORM_EPS)
    inv_ref[0:rows, :] = jnp.broadcast_to(inv, (rows, LANES))
    gain = g_ref[...]
    scale = 1.0 + sc_ref[...]
    shift = sh_ref[...]

    def slab(i, carry):
        r0 = pl.multiple_of(i * BF16_ROWS, BF16_ROWS)
        inv_s = jnp.concatenate([inv_ref[pl.ds(r0, BF16_ROWS), :]] * (d // LANES), axis=1)
        h = ((x_ref[pl.ds(r0, BF16_ROWS), :] * inv_s) * gain) * scale + shift
        h_ref[pl.ds(row0 + r0, BF16_ROWS), :] = h.astype(BF16)
        return carry

    lax.fori_loop(0, rows // BF16_ROWS, slab, 0, unroll=4 if rows >= 4 * BF16_ROWS else 1)


def _norm_mod_rows(x, g_ref, sc_ref, sh_ref):
    inv = lax.rsqrt(jnp.mean(x * x, axis=-1, keepdims=True) + NORM_EPS)
    return (((x * inv) * g_ref[...]) * (1.0 + sc_ref[...]) + sh_ref[...]).astype(BF16)


def _ahead_rows(tm, n_steps):
    return min(tm, -(-tm // ((n_steps - 1) * BF16_ROWS)) * BF16_ROWS)


def _ahead_tile(i, j, n_tiles):
    return jnp.where((i == 0) & (j == 0), 0, jnp.minimum(i + 1, n_tiles - 1))


def _ada_kernel(c_ref, w_ref, b_ref, o_ref):
    c = c_ref[...]
    act = (c * jax.nn.sigmoid(c)).astype(BF16)
    o_ref[...] = _dot(act, w_ref[...].astype(BF16)) + b_ref[...]


def _ada(c, ada_w, ada_b):
    depth, d, n = ada_w.shape
    b = c.shape[0]
    tn = _tile(n, 1024)
    return pl.pallas_call(
        _ada_kernel,
        out_shape=jax.ShapeDtypeStruct((depth, b, n), F32),
        grid=(depth, n // tn),
        in_specs=[
            pl.BlockSpec((b, d), lambda l, j: (0, 0)),
            pl.BlockSpec((None, d, tn), lambda l, j: (l, 0, j)),
            pl.BlockSpec((None, 1, tn), lambda l, j: (l, 0, j)),
        ],
        out_specs=pl.BlockSpec((None, b, tn), lambda l, j: (l, 0, j)),
        compiler_params=_params("parallel", "parallel"),
        name="ada",
    )(c, ada_w, ada_b.reshape(depth, 1, n))


def _in_proj_kernel(xn_ref, g_ref, scn_ref, shn_ref, w_ref, cs_ref, o_ref, h_ref, inv_ref, *, ahead):
    i = pl.program_id(0)
    j = pl.program_id(1)
    tm = o_ref.shape[0]
    slot = i % 2

    @pl.when((i == 0) & (j == 0))
    def _():
        _store_norm_mod(h_ref.at[0], 0, xn_ref, g_ref, scn_ref, shn_ref, inv_ref)

    o_ref[...] = (_dot(h_ref[slot], w_ref[...]) * cs_ref[...]).astype(o_ref.dtype)
    r0 = pl.multiple_of(jnp.clip((j - 1) * ahead, 0, tm - ahead), BF16_ROWS)
    h_ref[1 - slot, pl.ds(r0, ahead), :] = _norm_mod_rows(xn_ref[pl.ds(r0, ahead), :], g_ref, scn_ref, shn_ref)


def _in_proj(x, g, sc, sh, w, layer, colscale, out_dtype, seq, tm_target=1024, tn_target=1536):
    m, d = x.shape
    n = w.shape[-1]
    tm = _tile(seq, tm_target, BF16_ROWS)
    tn = _tile(n, min(tn_target, n // 2))
    n_i, n_j = m // tm, n // tn
    ahead_x = lambda i, j: (_ahead_tile(i, j, n_i), 0)
    ahead_b = lambda i, j: ((_ahead_tile(i, j, n_i) * tm) // seq, 0, 0)
    return pl.pallas_call(
        functools.partial(_in_proj_kernel, ahead=_ahead_rows(tm, n_j)),
        out_shape=jax.ShapeDtypeStruct((m, n), out_dtype),
        grid=(n_i, n_j),
        in_specs=[
            pl.BlockSpec((tm, d), ahead_x),
            pl.BlockSpec((1, d), lambda i, j: (0, 0)),
            pl.BlockSpec((None, 1, d), ahead_b),
            pl.BlockSpec((None, 1, d), ahead_b),
            pl.BlockSpec((None, d, tn), lambda i, j: (layer, 0, j)),
            pl.BlockSpec((1, tn), lambda i, j: (0, j)),
        ],
        out_specs=pl.BlockSpec((tm, tn), lambda i, j: (i, j)),
        scratch_shapes=[pltpu.VMEM((2, tm, d), BF16), pltpu.VMEM((tm, LANES), F32)],
        compiler_params=_params("arbitrary", "arbitrary"),
        name="in_proj",
    )(x, g, sc, sh, w, colscale)


def _compress_kernel(kv_ref, pos_ref, w1_ref, w2_ref, o_ref, tok_sc, c_sc):
    g, n, hd = o_ref.shape
    for gi in range(g):
        tok_sc[gi] = kv_ref[:, gi * hd:(gi + 1) * hd].astype(F32)
        for j in range(CMP_STRIDE):
            c_sc[gi * n:(gi + 1) * n, j * hd:(j + 1) * hd] = tok_sc[gi, pl.ds(j, n, stride=CMP_STRIDE), :]
    c = c_sc[...]
    lo = (c + pos_ref[0:1, :]).astype(BF16)
    hi = (c + pos_ref[1:2, :]).astype(BF16)
    first = _dot(lo, w1_ref[0])
    second = _dot(hi, w1_ref[1])
    hid = jnp.concatenate([first[gi * n:(gi + 1) * n] + pltpu.roll(second[gi * n:(gi + 1) * n], n - 1, axis=0)
                           for gi in range(g)], axis=0)
    out = _dot(jax.nn.gelu(hid).astype(BF16), w2_ref[...]).astype(o_ref.dtype)
    for gi in range(g):
        o_ref[gi] = out[gi * n:(gi + 1) * n]


def _compress(z, batch, seq, pos, w1, w2):
    g = NSA_KV_GROUPS
    hd = HEAD_DIM
    n = seq // CMP_STRIDE
    cw = CMP_STRIDE * hd
    kv0 = NSA_WIDTH // KV_WIDTH
    return pl.pallas_call(
        _compress_kernel,
        out_shape=jax.ShapeDtypeStruct((batch, 2, g, n, hd), BF16),
        grid=(batch, 2),
        in_specs=[
            pl.BlockSpec((seq, KV_WIDTH), lambda bi, s: (bi, kv0 + s)),
            pl.BlockSpec((None, 2, cw), lambda bi, s: (s, 0, 0)),
            pl.BlockSpec((None, 2, cw, hd), lambda bi, s: (s, 0, 0, 0)),
            pl.BlockSpec((None, hd, hd), lambda bi, s: (s, 0, 0)),
        ],
        out_specs=pl.BlockSpec((None, None, g, n, hd), lambda bi, s: (bi, s, 0, 0, 0)),
        scratch_shapes=[pltpu.VMEM((g, seq, hd), F32), pltpu.VMEM((g * n, cw), F32)],
        compiler_params=_params("parallel", "parallel"),
        name="compress",
    )(z, pos, w1, w2)


def _nsa_kernel(q_ref, ks_ref, vs_ref, kw_ref, vw_ref, kc_ref, vc_ref, zg_ref, o_ref,
                kaug_sc, vsaug_sc, vwaug_sc, vcaug_sc, ov_sc, bias_sc, qa_sc, gate_sc, s_sc, sw_sc, m_sc, acc_sc, out_sc,
                *, tq):
    hd = HEAD_DIM
    hpg = HEADS_PER_GROUP
    tk = tq
    n_blk = LANES // 4
    seq = ks_ref.shape[0]
    n_cmp = kc_ref.shape[0]
    grp = pl.program_id(1)
    qi = pl.program_id(2)
    t0 = qi * tq

    @pl.when(qi == 0)
    def _():
        kaug_sc[:, :hd] = ks_ref[...]
        key_blk = lax.shift_right_logical(lax.broadcasted_iota(jnp.int32, (seq, LANES), 0), SLC_LEN_LOG2)
        lane_blk = lax.broadcasted_iota(jnp.int32, (seq, LANES), 1) & (n_blk - 1)
        kaug_sc[:, hd:] = jnp.where(key_blk == lane_blk, 1.0, 0.0).astype(BF16)
        ones = jnp.ones((seq, LANES), BF16)
        vsaug_sc[:, :hd] = vs_ref[...]
        vsaug_sc[:, hd:] = ones
        vwaug_sc[:, :hd] = vw_ref[...]
        vwaug_sc[:, hd:] = ones
        vcaug_sc[:, :hd] = vc_ref[...]
        vcaug_sc[:, hd:] = jnp.ones((n_cmp, LANES), BF16)
        cstart = (lax.broadcasted_iota(jnp.int32, (n_blk, hpg * n_cmp), 1) & (n_cmp - 1)) * CMP_STRIDE
        bstart = lax.broadcasted_iota(jnp.int32, (n_blk, hpg * n_cmp), 0) * SLC_LEN
        ov_sc[...] = jnp.where((cstart < bstart + SLC_LEN) & (cstart + CMP_LEN > bstart), 1.0, 0.0).astype(BF16)
        r = lax.broadcasted_iota(jnp.int32, (tq, tk), 0)
        cidx = lax.broadcasted_iota(jnp.int32, (tq, tk), 1)
        bias_sc[0] = jnp.where(cidx <= r, 0.0, MASK_VALUE)
        bias_sc[1] = jnp.where(cidx > r, 0.0, MASK_VALUE)

    q = q_ref[...]
    qs = jnp.concatenate([q[:, h * hd:(h + 1) * hd] for h in range(hpg)], axis=0)
    qpos = t0 + (lax.broadcasted_iota(jnp.int32, (hpg * tq, 1), 0) & (tq - 1))

    def put_scores(dst, slot, k_ref, start, lhs):
        k_t = k_ref[pl.ds(pl.multiple_of(start, tq), tq), :]
        for h in range(hpg):
            dst[slot, h] = lax.dot_general(lhs(h), k_t, _NT, preferred_element_type=F32)

    def reset():
        m_sc[...] = jnp.full(m_sc.shape, MASK_VALUE, F32)
        acc_sc[...] = jnp.zeros(acc_sc.shape, F32)

    def fold_head(src, slot, h, v_t, bias):
        sc = src[slot, h]
        if bias is not None:
            sc = sc + bias
        m_prev = m_sc[h]
        m_new = jnp.maximum(m_prev, jnp.max(sc, axis=-1, keepdims=True))
        alpha = jnp.exp2(m_prev - m_new)
        pe = jnp.exp2(sc - jnp.concatenate([m_new] * (tq // LANES), axis=1))
        acc_sc[h] = jnp.concatenate([alpha, alpha], axis=1) * acc_sc[h] + _dot(pe.astype(BF16), v_t)
        m_sc[h] = m_new

    def fold(src, slot, v_ref, start, mask):
        v_t = v_ref[pl.ds(pl.multiple_of(start, tq), tq), :]
        bias = None if mask is None else bias_sc[mask]
        for h in range(hpg):
            fold_head(src, slot, h, v_t, bias)

    def finish(h):
        acc = acc_sc[h]
        return acc[:, :hd] / acc[:, hd:]

    q_head = lambda h: q[:, h * hd:(h + 1) * hd]
    s = lax.dot_general(qs, kc_ref[...], _NT, preferred_element_type=F32)
    st = lax.dot_general(kc_ref[...], qs, _NT, preferred_element_type=F32)
    put_scores(sw_sc, 0, kw_ref, t0, q_head)

    gt = jax.nn.sigmoid(pltpu.roll(zg_ref[...].astype(F32), (LANES - GATES_PER_GROUP * grp) & (LANES - 1), axis=1))

    for c in range(GATES_PER_GROUP):
        gate_sc[:, c * LANES:(c + 1) * LANES] = jnp.broadcast_to(gt[:, c:c + 1], (tq, LANES))

    def gate(h, branch):
        c = 3 * h + branch
        return gate_sc[:, c * LANES:(c + 1) * LANES]

    cmp_end = lax.broadcasted_iota(jnp.int32, (1, n_cmp), 1) * CMP_STRIDE + (CMP_LEN - 1)
    s = jnp.where(cmp_end <= qpos, s, -jnp.inf)
    mx = jnp.max(s, axis=-1, keepdims=True)
    mx = jnp.where(jnp.isfinite(mx), mx, 0.0)
    pe_cmp = jnp.exp2(s - mx).astype(BF16)

    end_t = lax.broadcasted_iota(jnp.int32, (n_cmp, 1), 0) * CMP_STRIDE + (CMP_LEN - 1)
    qpos_t = t0 + (lax.broadcasted_iota(jnp.int32, (1, hpg * tq), 1) & (tq - 1))
    st = jnp.where(end_t <= qpos_t, st, -jnp.inf)
    mt = jnp.max(st, axis=0, keepdims=True)
    mt = jnp.where(jnp.isfinite(mt), mt, 0.0)
    pt = jnp.exp2(st - mt)
    pt = pt / jnp.maximum(jnp.sum(pt, axis=0, keepdims=True), DENOM_FLOOR)
    p_heads = jnp.concatenate([pt[:, h * tq:(h + 1) * tq] for h in range(hpg)], axis=0).astype(BF16)
    imp = _dot(ov_sc[...], p_heads)
    acc = _dot(pe_cmp, vcaug_sc[...])
    put_scores(sw_sc, 1, kw_ref, jnp.maximum(t0 - tq, 0), q_head)

    blk = lax.broadcasted_iota(jnp.int32, (n_blk, tq), 0)
    cur = lax.shift_right_logical(t0 + lax.broadcasted_iota(jnp.int32, (n_blk, tq), 1), SLC_LEN_LOG2)
    valid = blk <= cur
    forced = (blk == 0) | (valid & (blk > cur - SLC_LOCAL))
    score = jnp.where(forced, imp + FORCE_BONUS, jnp.where(valid, imp, -1.0))
    key = lax.bitcast_convert_type(score, jnp.int32)
    rank = jnp.zeros((n_blk, tq), jnp.int32)
    for r in range(1, n_blk):
        other = pltpu.roll(key, r, axis=0)
        rank = rank + jnp.where(other + jnp.where(blk >= r, 1, 0) > key, 1, 0)
    sel_bias = jnp.where((rank < SLC_TOP_N) & valid, 0.0, MASK_VALUE)
    sel_rows = jnp.concatenate([sel_bias, jnp.zeros((LANES - n_blk, tq), F32)], axis=0).T.astype(BF16)
    for h in range(hpg):
        qa_sc[h] = jnp.concatenate([q_head(h), sel_rows], axis=1)
    o_cmp = acc[:, :hd] / jnp.maximum(acc[:, hd:], DENOM_FLOOR)
    for h in range(hpg):
        out_sc[:, h * hd:(h + 1) * hd] = gate(h, 0) * o_cmp[h * tq:(h + 1) * tq]

    assert WINDOW == tq
    reset()
    k_first = kaug_sc[0:tq, :]
    v_diag = vwaug_sc[pl.ds(pl.multiple_of(t0, tq), tq), :]
    for h in range(hpg):
        s_sc[0, h] = lax.dot_general(qa_sc[h], k_first, _NT, preferred_element_type=F32)
        fold_head(sw_sc, 0, h, v_diag, bias_sc[0])

    @pl.when(qi >= 1)
    def _():
        fold(sw_sc, 1, vwaug_sc, t0 - tq, 1)

    for h in range(hpg):
        out_sc[:, h * hd:(h + 1) * hd] += gate(h, 2) * finish(h)

    reset()

    def score_and_fold(put_slot, put_start, fold_slot, fold_start, mask):
        k_t = kaug_sc[pl.ds(pl.multiple_of(put_start, tq), tq), :]
        v_t = vsaug_sc[pl.ds(pl.multiple_of(fold_start, tq), tq), :]
        bias = None if mask is None else bias_sc[mask]
        for h in range(hpg):
            s_sc[put_slot, h] = lax.dot_general(qa_sc[h], k_t, _NT, preferred_element_type=F32)
            fold_head(s_sc, fold_slot, h, v_t, bias)

    def slc_pair(j, carry):
        e = 2 * j * tq
        score_and_fold(1, e + tq, 0, e, None)
        score_and_fold(0, e + 2 * tq, 1, e + tq, None)
        return carry

    lax.fori_loop(0, qi // 2, slc_pair, 0)

    @pl.when(qi % 2 == 1)
    def _():
        score_and_fold(1, t0, 0, t0 - tq, None)
        fold(s_sc, 1, vsaug_sc, t0, 0)

    @pl.when(qi % 2 == 0)
    def _():
        fold(s_sc, 0, vsaug_sc, t0, 0)

    for h in range(hpg):
        o_h = out_sc[:, h * hd:(h + 1) * hd] + gate(h, 1) * finish(h)
        o_ref[:, h * hd:(h + 1) * hd] = o_h.astype(o_ref.dtype)


def _nsa(z, kv_cmp, gates_col, batch, seq, tq=WINDOW):
    m = z.shape[0]
    assert gates_col % LANES == 0
    g = NSA_KV_GROUPS
    hd = HEAD_DIM
    hpg = HEADS_PER_GROUP
    n_cmp = kv_cmp.shape[3]
    nq = seq // tq
    qcols = NSA_WIDTH // hd
    assert (tq // 4) % SLC_LEN == 0 and n_cmp == LANES

    def kv_spec(slot):
        return pl.BlockSpec((seq, hd), lambda b, gi, i: (b, qcols + slot * g + gi))

    def cmp_spec(slot):
        return pl.BlockSpec((None, None, None, n_cmp, hd), lambda b, gi, i: (b, slot, gi, 0, 0))

    return pl.pallas_call(
        functools.partial(_nsa_kernel, tq=tq),
        out_shape=jax.ShapeDtypeStruct((m, NSA_WIDTH), BF16),
        grid=(batch, g, nq),
        in_specs=[
            pl.BlockSpec((tq, hpg * hd), lambda b, gi, i: (b * nq + i, gi)),
            kv_spec(2), kv_spec(3), kv_spec(4), kv_spec(5),
            cmp_spec(0), cmp_spec(1),
            pl.BlockSpec((tq, LANES), lambda b, gi, i: (b * nq + i, gates_col // LANES)),
        ],
        out_specs=pl.BlockSpec((tq, hpg * hd), lambda b, gi, i: (b * nq + i, gi)),
        scratch_shapes=[
            pltpu.VMEM((seq, 2 * hd), BF16),
            pltpu.VMEM((seq, 2 * hd), BF16),
            pltpu.VMEM((seq, 2 * hd), BF16),
            pltpu.VMEM((n_cmp, 2 * hd), BF16),
            pltpu.VMEM((LANES // 4, hpg * n_cmp), BF16),
            pltpu.VMEM((2, tq, tq), F32),
            pltpu.VMEM((hpg, tq, 2 * hd), BF16),
            pltpu.VMEM((tq, GATES_PER_GROUP * LANES), F32),
            pltpu.VMEM((2, hpg, tq, tq), F32),
            pltpu.VMEM((2, hpg, tq, tq), F32),
            pltpu.VMEM((hpg, tq, LANES), F32),
            pltpu.VMEM((hpg, tq, 2 * hd), F32),
            pltpu.VMEM((tq, hpg * hd), F32),
        ],
        compiler_params=_params("parallel", "parallel", "arbitrary"),
        name="nsa",
    )(z, z, z, z, z, kv_cmp, kv_cmp, z)


def _lru_kernel(lx_ref, ly_ref, cw_ref, cb_ref, wa_ref, ba_ref, wi_ref, bi_ref, lam_ref, o_ref, x_sc, a_sc, h_sc):
    seq, wt = lx_ref.shape
    x_sc[0:SUBLANES, :] = jnp.zeros((SUBLANES, wt), F32)
    x_sc[SUBLANES:, :] = lx_ref[...].astype(F32)
    taps = cw_ref.shape[0]
    u = cb_ref[...]
    for j in range(taps):
        back = taps - 1 - j
        u = u + x_sc[SUBLANES - back:SUBLANES - back + seq, :] * cw_ref[j:j + 1, :]
    ub = u.astype(BF16)
    r = jax.nn.sigmoid(_dot(ub, wa_ref[0]) + ba_ref[...])
    gate_i = jax.nn.sigmoid(_dot(ub, wi_ref[0]) + bi_ref[...])
    log_a = r * (LRU_C * jax.nn.log_sigmoid(lam_ref[...]))
    a = jnp.exp(log_a)
    y = 1.0 - a * a
    b = jnp.where(y > 0.0, y * lax.rsqrt(y), 0.0) * (gate_i * u)

    n_seg = SUBLANES
    seg = seq // n_seg
    stride = a_sc.shape[0] // n_seg
    for i in range(n_seg):
        a_sc[stride * i:stride * i + seg, :] = a[seg * i:seg * (i + 1), :]
        h_sc[stride * i:stride * i + seg, :] = b[seg * i:seg * (i + 1), :]

    def steps(jj, carry):
        h, p = carry
        for k in range(LRU_SCAN_UNROLL):
            rows = pl.ds(jj * LRU_SCAN_UNROLL + k, n_seg, stride=stride)
            a_j = a_sc[rows, :]
            h = a_j * h + h_sc[rows, :]
            p = a_j * p
            h_sc[rows, :] = h
            a_sc[rows, :] = p
        return h, p

    h_end, p_end = lax.fori_loop(0, seg // LRU_SCAN_UNROLL, steps,
                                 (jnp.zeros((n_seg, wt), F32), jnp.ones((n_seg, wt), F32)))
    enter = jnp.zeros((1, wt), F32)
    for i in range(n_seg):
        h_i = h_sc[stride * i:stride * i + seg, :] + a_sc[stride * i:stride * i + seg, :] * enter
        gated = jax.nn.gelu(ly_ref[seg * i:seg * (i + 1), :].astype(F32)) * h_i
        o_ref[seg * i:seg * (i + 1), :] = gated.astype(o_ref.dtype)
        enter = h_end[i:i + 1, :] + p_end[i:i + 1, :] * enter


def _lru(z, lx_col, ly_col, conv_w, conv_b, wa, ba, wi, bi, lam, batch, seq):
    m = z.shape[0]
    wt = LRU_BLOCK_DIM
    nw = LRU_WIDTH // wt
    assert lx_col % wt == 0 and ly_col % wt == 0
    taps = conv_w.shape[0]
    vec = pl.BlockSpec((1, wt), lambda b, j: (0, j))
    blk = pl.BlockSpec((1, wt, wt), lambda b, j: (j, 0, 0))
    return pl.pallas_call(
        _lru_kernel,
        out_shape=jax.ShapeDtypeStruct((m, LRU_WIDTH), BF16),
        grid=(batch, nw),
        in_specs=[
            pl.BlockSpec((seq, wt), lambda b, j: (b, lx_col // wt + j)),
            pl.BlockSpec((seq, wt), lambda b, j: (b, ly_col // wt + j)),
            pl.BlockSpec((taps, wt), lambda b, j: (0, j)),
            vec, blk, vec, blk, vec, vec,
        ],
        out_specs=pl.BlockSpec((seq, wt), lambda b, j: (b, j)),
        scratch_shapes=[pltpu.VMEM((SUBLANES + seq, wt), F32),
                        pltpu.VMEM((seq + SUBLANES * LRU_SEG_PAD, wt), F32),
                        pltpu.VMEM((seq + SUBLANES * LRU_SEG_PAD, wt), F32)],
        compiler_params=_params("parallel", "parallel"),
        name="lru",
    )(z, z, conv_w, conv_b, wa, ba, wi, bi, lam)


def _merge_kernel(oa_ref, ob_ref, pa_ref, pb_ref, ga_ref, gb_ref, o_ref):
    a = _dot(oa_ref[...], pa_ref[...])
    b = _dot(ob_ref[...], pb_ref[...])
    merged = jax.nn.sigmoid(ga_ref[...].astype(F32)) * a + jax.nn.sigmoid(gb_ref[...].astype(F32)) * b
    o_ref[...] = merged.astype(o_ref.dtype)


def _merge(o_a, o_b, proj_a, proj_b, layer, z, ga_col, gb_col, d_model, tm_target=1024, tn_target=1024):
    m, ka = o_a.shape
    kb = o_b.shape[1]
    tm = _tile(m, tm_target, BF16_ROWS)
    tn = _tile(d_model, tn_target)
    ga0 = ga_col // tn
    gb0 = gb_col // tn
    assert ga_col % tn == 0 and gb_col % tn == 0
    return pl.pallas_call(
        _merge_kernel,
        out_shape=jax.ShapeDtypeStruct((m, d_model), BF16),
        grid=(m // tm, d_model // tn),
        in_specs=[
            pl.BlockSpec((tm, ka), lambda i, j: (i, 0)),
            pl.BlockSpec((tm, kb), lambda i, j: (i, 0)),
            pl.BlockSpec((None, ka, tn), lambda i, j: (layer, 0, j)),
            pl.BlockSpec((None, kb, tn), lambda i, j: (layer, 0, j)),
            pl.BlockSpec((tm, tn), lambda i, j: (i, ga0 + j)),
            pl.BlockSpec((tm, tn), lambda i, j: (i, gb0 + j)),
        ],
        out_specs=pl.BlockSpec((tm, tn), lambda i, j: (i, j)),
        compiler_params=_params("parallel", "parallel"),
        name="merge",
    )(o_a, o_b, proj_a, proj_b, z, z)


def _matmul_residual_kernel(a_ref, w_ref, x_ref, g_ref, o_ref):
    o_ref[...] = x_ref[...] + g_ref[...] * _dot(a_ref[...], w_ref[...])


def _matmul_residual(a, w, layer, x, gate, seq, name, tm_target, tn_target):
    m, k = a.shape
    n = w.shape[-1]
    tm = _tile(seq, tm_target, BF16_ROWS)
    tn = _tile(n, tn_target)
    w_mode = dict(pipeline_mode=pl.Buffered(1)) if n == tn else {}
    return pl.pallas_call(
        _matmul_residual_kernel,
        out_shape=jax.ShapeDtypeStruct((m, n), F32),
        grid=(n // tn, m // tm),
        in_specs=[
            pl.BlockSpec((tm, k), lambda j, i: (i, 0)),
            pl.BlockSpec((None, k, tn), lambda j, i: (layer, 0, j), **w_mode),
            pl.BlockSpec((tm, tn), lambda j, i: (i, j)),
            pl.BlockSpec((None, 1, tn), lambda j, i: ((i * tm) // seq, 0, j)),
        ],
        out_specs=pl.BlockSpec((tm, tn), lambda j, i: (i, j)),
        compiler_params=_params("parallel", "parallel"),
        name=name,
    )(a, w, x, gate)


def _ffn_up_kernel(xn_ref, g_ref, scn_ref, shn_ref, wg_ref, wv_ref, cw_ref, cb_ref, o_ref, h_ref, inv_ref, *, seq, ahead):
    halo = BF16_ROWS
    tm = o_ref.shape[0]
    i = pl.program_id(0)
    j = pl.program_id(1)
    slot = i % 2

    @pl.when((i == 0) & (j == 0))
    def _():
        h_ref[0, 0:halo, :] = jnp.zeros((halo, h_ref.shape[2]), BF16)
        _store_norm_mod(h_ref.at[0], halo, xn_ref, g_ref, scn_ref, shn_ref, inv_ref)

    h = h_ref[slot]
    gate = _dot(h, wg_ref[...])
    val = _dot(h[halo:], wv_ref[...])
    seq_start = (i * tm) % seq == 0
    row = lax.broadcasted_iota(jnp.int32, (tm, 1), 0)
    taps = cw_ref.shape[0]
    conv = cb_ref[...]
    for k in range(taps):
        back = taps - 1 - k
        gk = gate[halo - back:halo - back + tm]
        if back:
            gk = jnp.where(row >= jnp.where(seq_start, back, 0), gk, 0.0)
        conv = conv + gk * cw_ref[k:k + 1, :]
    o_ref[...] = (jax.nn.gelu(conv) * val).astype(o_ref.dtype)

    r0 = pl.multiple_of(jnp.clip((j - 1) * ahead, 0, tm - ahead), BF16_ROWS)
    h_ref[1 - slot, pl.ds(halo + r0, ahead), :] = _norm_mod_rows(xn_ref[pl.ds(r0, ahead), :], g_ref, scn_ref, shn_ref)
    h_ref[1 - slot, 0:halo, :] = h_ref[slot, tm:tm + halo, :]


def _ffn_up(x, g, sc, sh, w_up, layer, conv_w, conv_b, seq, tm_target=1024, tn_target=768):
    m, d = x.shape
    d_ff = w_up.shape[-1] // 2
    tm = _tile(seq, tm_target, BF16_ROWS)
    tn = _tile(d_ff, min(tn_target, d_ff // 2))
    n_i, n_j = m // tm, d_ff // tn
    halo = BF16_ROWS
    taps = conv_w.shape[0]
    ahead_x = lambda i, j: (_ahead_tile(i, j, n_i), 0)
    ahead_b = lambda i, j: ((_ahead_tile(i, j, n_i) * tm) // seq, 0, 0)
    return pl.pallas_call(
        functools.partial(_ffn_up_kernel, seq=seq, ahead=_ahead_rows(tm, n_j)),
        out_shape=jax.ShapeDtypeStruct((m, d_ff), BF16),
        grid=(n_i, n_j),
        in_specs=[
            pl.BlockSpec((tm, d), ahead_x),
            pl.BlockSpec((1, d), lambda i, j: (0, 0)),
            pl.BlockSpec((None, 1, d), ahead_b),
            pl.BlockSpec((None, 1, d), ahead_b),
            pl.BlockSpec((None, d, tn), lambda i, j: (layer, 0, j)),
            pl.BlockSpec((None, d, tn), lambda i, j: (layer, 0, n_j + j)),
            pl.BlockSpec((taps, tn), lambda i, j: (0, j)),
            pl.BlockSpec((1, tn), lambda i, j: (0, j)),
        ],
        out_specs=pl.BlockSpec((tm, tn), lambda i, j: (i, j)),
        scratch_shapes=[pltpu.VMEM((2, halo + tm, d), BF16), pltpu.VMEM((tm, LANES), F32)],
        compiler_params=_params("arbitrary", "arbitrary"),
        name="ffn_up",
    )(x, g, sc, sh, w_up, w_up, conv_w, conv_b)


def _final_norm_kernel(x_ref, g_ref, o_ref):
    x = x_ref[...]
    inv = lax.rsqrt(jnp.mean(x * x, axis=-1, keepdims=True) + NORM_EPS)
    o_ref[...] = (x * inv) * g_ref[...]


def _final_norm(x, g, tm_target=512):
    m, d = x.shape
    tm = _tile(m, tm_target, SUBLANES)
    return pl.pallas_call(
        _final_norm_kernel,
        out_shape=jax.ShapeDtypeStruct((m, d), F32),
        grid=(m // tm,),
        in_specs=[pl.BlockSpec((tm, d), lambda i: (i, 0)), pl.BlockSpec((1, d), lambda i: (0, 0))],
        out_specs=pl.BlockSpec((tm, d), lambda i: (i, 0)),
        compiler_params=_params("parallel"),
        name="final_norm",
    )(x, g)


def _compress_weights(pos, w1, w2):
    half = CMP_LEN // 2
    return (pos.reshape(2, half * HEAD_DIM),
            w1.reshape(2, half * HEAD_DIM, HEAD_DIM).astype(BF16),
            w2.astype(BF16))


def kernel(x, c, ada_w, ada_b, norm1_g, w_in, cmp_pos_k, cmp_w1_k, cmp_w2_k, cmp_pos_v, cmp_w1_v, cmp_w2_v,
           lru_conv_w, lru_conv_b, lru_wa, lru_ba, lru_wi, lru_bi, lru_lambda, proj_a, proj_b, w_out, norm2_g,
           ffn_up, ffn_conv_w, ffn_conv_b, ffn_down, final_g):
    batch, seq, d = x.shape
    depth = ada_w.shape[0]
    m = batch * seq
    assert seq % SLC_LEN == 0 and seq // SLC_LEN == LANES // 4 and CMP_STRIDE * LANES == seq
    n_attn = NSA_WIDTH + 6 * KV_WIDTH
    n_rest = 2 * LRU_WIDTH + 2 * d
    n_z = -(-(n_attn + n_rest + N_GATES) // (4 * LANES)) * (4 * LANES)
    z_cols = dict(lx=n_attn, ly=n_attn + LRU_WIDTH, ga=n_attn + 2 * LRU_WIDTH, gb=n_attn + 2 * LRU_WIDTH + d,
                  gates=n_attn + n_rest)

    mod = _ada(c, ada_w, ada_b)
    xf = x.reshape(m, d)
    q_scale = jnp.concatenate([jnp.full((1, NSA_WIDTH), HEAD_DIM ** -0.5 * LOG2_E, F32), jnp.ones((1, n_z - NSA_WIDTH), F32)],
                              axis=1)
    w_all = jnp.concatenate([w_in[:, :, :n_attn], w_in[:, :, n_attn + N_GATES:],
                             jnp.pad(w_in[:, :, n_attn:n_attn + N_GATES],
                                     ((0, 0), (0, 0), (0, n_z - n_attn - n_rest - N_GATES)))], axis=2).astype(BF16)
    proj_a, proj_b, w_out, ffn_up, ffn_down, lru_wa, lru_wi = [
        w.astype(BF16) for w in (proj_a, proj_b, w_out, ffn_up, ffn_down, lru_wa, lru_wi)]

    for l in range(depth):
        sh1, sc1, g1, sh2, sc2, g2 = [mod[l, :, k * d:(k + 1) * d].reshape(batch, 1, d) for k in range(6)]
        z = _in_proj(xf, norm1_g[l].reshape(1, d), sc1, sh1, w_all, l, q_scale, BF16, seq)

        pk, w1k, w2k = _compress_weights(cmp_pos_k[l], cmp_w1_k[l], cmp_w2_k[l])
        pv, w1v, w2v = _compress_weights(cmp_pos_v[l], cmp_w1_v[l], cmp_w2_v[l])
        kv_cmp = _compress(z, batch, seq, jnp.stack([pk, pv]), jnp.stack([w1k, w1v]), jnp.stack([w2k, w2v]))

        o_a = _nsa(z, kv_cmp, z_cols["gates"], batch, seq)
        o_b = _lru(z, z_cols["lx"], z_cols["ly"], lru_conv_w[l], lru_conv_b[l].reshape(1, -1), lru_wa[l],
                   lru_ba[l].reshape(1, -1), lru_wi[l], lru_bi[l].reshape(1, -1),
                   lru_lambda[l].reshape(1, -1), batch, seq)
        merged = _merge(o_a, o_b, proj_a, proj_b, l, z, z_cols["ga"], z_cols["gb"], d)
        xf = _matmul_residual(merged, w_out, l, xf, g1, seq, "out_proj", tm_target=1024, tn_target=2048)

        act = _ffn_up(xf, norm2_g[l].reshape(1, d), sc2, sh2, ffn_up, l, ffn_conv_w[l],
                      ffn_conv_b[l].reshape(1, -1), seq)
        xf = _matmul_residual(act, ffn_down, l, xf, g2, seq, "ffn_down", tm_target=512, tn_target=1024)

    return _final_norm(xf, final_g.reshape(1, d)).reshape(batch, seq, d)
```
